```python
import math
import numpy as np
import jax, jax.numpy as jnp
from jax import lax

D_MODEL = 1024
BATCH = 32
SEQ = 2048
DEPTH = 2
DEC_BATCH = 2
DEC_SEQ = 8192
PAST_LEN = 128

GRID_W = 64
NA_HEADS = 8
NA_HEAD_DIM = 64
NA_WIDTH = NA_HEADS * NA_HEAD_DIM
NA_WIN_ROWS_MAX = 8
NA_WIN_COLS = 16
CONV_CH = 512
CONV_WIDTH = 31
DIFF_HEADS = 4
DIFF_HEAD_DIM = 64
DIFF_WIDTH = DIFF_HEADS * 2 * DIFF_HEAD_DIM
N_BRANCH = 3
BRANCH_W = 512
T5_BUCKETS = 32
T5_MAX_DIST = 128
Q_BLOCK = 128
FFN_HIDDEN = -(-8 * D_MODEL // (3 * 256)) * 256
EPS = 1e-6

NA_COLS = 3 * NA_WIDTH
CONV_COLS = 2 * CONV_CH
DIFF_COLS = 3 * DIFF_WIDTH
GATE_COLS = N_BRANCH * D_MODEL
OFF_CONV = NA_COLS
OFF_DIFF = OFF_CONV + CONV_COLS
OFF_GATE = OFF_DIFF + DIFF_COLS
IN_COLS = OFF_GATE + GATE_COLS

kernel_name = "hybrid_natten_conformer_diffattn_encoder"


def rms_norm(x, g):
    xf = x.astype(jnp.float32)
    y = xf * lax.rsqrt(jnp.mean(xf * xf, axis=-1, keepdims=True) + EPS)
    return (y * g.astype(jnp.float32)).astype(x.dtype)


def layer_norm(x, g, b):
    xf = x.astype(jnp.float32)
    mu = jnp.mean(xf, axis=-1, keepdims=True)
    xc = xf - mu
    y = xc * lax.rsqrt(jnp.mean(xc * xc, axis=-1, keepdims=True) + EPS)
    return (y * g.astype(jnp.float32) + b.astype(jnp.float32)).astype(x.dtype)


def t5_bucket(rel):
    nb = T5_BUCKETS // 2
    max_exact = nb // 2
    ret = jnp.where(rel > 0, nb, 0)
    n = jnp.abs(rel)
    nf = jnp.maximum(n, 1).astype(jnp.float32)
    large = max_exact + (jnp.log(nf / max_exact) / math.log(T5_MAX_DIST / max_exact)
                         * (nb - max_exact)).astype(jnp.int32)
    large = jnp.minimum(large, nb - 1)
    return ret + jnp.where(n < max_exact, n, large)


def neighbourhood_attention(q, k, v, rpb):
    B, S, H, dh = q.shape
    rows = S // GRID_W
    kr = min(NA_WIN_ROWS_MAX, rows)
    kw = NA_WIN_COLS
    qg = q.reshape(B, rows, GRID_W, H, dh)
    kg = k.reshape(B, rows, GRID_W, H, dh)
    vg = v.reshape(B, rows, GRID_W, H, dh)
    col = np.arange(GRID_W)
    col_start = np.clip(col - kw // 2, 0, GRID_W - kw)
    col_idx = col_start[:, None] + np.arange(kw)[None, :]
    col_off = col_idx - col[:, None] + (NA_WIN_COLS - 1)
    col_bias = rpb[:, :, col_off]
    scale = dh ** -0.5

    def one_row(args):
        r, q_row = args
        rs = jnp.clip(r - kr // 2, 0, rows - kr)
        kb = lax.dynamic_slice_in_dim(kg, rs, kr, axis=1)[:, :, col_idx]
        vb = lax.dynamic_slice_in_dim(vg, rs, kr, axis=1)[:, :, col_idx]
        row_off = rs + jnp.arange(kr) - r + (NA_WIN_ROWS_MAX - 1)
        bias = jnp.take(col_bias, row_off, axis=1).transpose(0, 2, 1, 3)
        s = (jnp.einsum('bwhd,bawkhd->bhwak', q_row, kb).astype(jnp.float32) * scale
             + bias[None].astype(jnp.float32))
        p = jax.nn.softmax(s.reshape(B, H, GRID_W, kr * kw), axis=-1)
        p = p.reshape(s.shape).astype(v.dtype)
        return jnp.einsum('bhwak,bawkhd->bwhd', p, vb)

    out = lax.map(one_row, (jnp.arange(rows), qg.transpose(1, 0, 2, 3, 4)))
    return out.transpose(1, 0, 2, 3, 4).reshape(B, S, H * dh)


def conformer_conv(u, dw_w, dw_b, ln_g, ln_b):
    a, g = jnp.split(u, 2, axis=-1)
    x = a * jax.nn.sigmoid(g)
    x = lax.conv_general_dilated(
        x, dw_w[:, None, :], window_strides=(1,),
        padding=[(CONV_WIDTH // 2, CONV_WIDTH // 2)],
        dimension_numbers=('NWC', 'WIO', 'NWC'),
        feature_group_count=CONV_CH) + dw_b
    x = layer_norm(x, ln_g, ln_b)
    return jax.nn.silu(x)


def diff_attention(q, k, v, t5_table, lam, subln_g, lam_init):
    B, S, H, _, dh = q.shape
    lf = lam.astype(jnp.float32)
    lam_full = jnp.exp(jnp.sum(lf[0] * lf[1])) - jnp.exp(jnp.sum(lf[2] * lf[3])) + lam_init
    nblk = S // Q_BLOCK
    qb = q.reshape(B, nblk, Q_BLOCK, H, 2, dh).transpose(1, 0, 2, 3, 4, 5)
    kpos = jnp.arange(S)
    scale = dh ** -0.5

    def one_block(args):
        i, q_blk = args
        qpos = i * Q_BLOCK + jnp.arange(Q_BLOCK)
        bias = t5_table[t5_bucket(kpos[None, :] - qpos[:, None])]
        bias = bias.transpose(2, 0, 1).astype(jnp.float32)
        s = jnp.einsum('bqhcd,bkhcd->cbhqk', q_blk, k).astype(jnp.float32) * scale + bias
        p = jax.nn.softmax(s, axis=-1)
        a = (p[0] - lam_full * p[1]).astype(v.dtype)
        return jnp.einsum('bhqk,bkhe->bqhe', a, v)

    o = lax.map(one_block, (jnp.arange(nblk), qb))
    o = o.transpose(1, 0, 2, 3, 4).reshape(B, S, H, 2 * dh)
    o = rms_norm(o, subln_g) * (1.0 - lam_init)
    return o.reshape(B, S, H * 2 * dh)


def trunk(x, w_in, b_gate, na_rpb, conv_dw_w, conv_dw_b, conv_ln_g, conv_ln_b,
          diff_lambda, diff_subln_g, t5_bias, w_branch, w_out,
          ln_mix_pre, ln_mix_post, ln_ffn_pre, ln_ffn_post, w_ffn_in, w_ffn_out):
    B, S, _ = x.shape
    for l in range(DEPTH):
        lam_init = 0.8 - 0.6 * math.exp(-0.3 * l)
        h = rms_norm(x, ln_mix_pre[l])
        u = h @ w_in[l]
        na_q, na_k, na_v = jnp.split(u[..., :NA_COLS].reshape(B, S, 3, NA_HEADS, NA_HEAD_DIM), 3, axis=2)
        o_na = neighbourhood_attention(na_q[:, :, 0], na_k[:, :, 0], na_v[:, :, 0], na_rpb[l])
        o_cv = conformer_conv(u[..., OFF_CONV:OFF_DIFF], conv_dw_w[l], conv_dw_b[l], conv_ln_g[l], conv_ln_b[l])
        du = u[..., OFF_DIFF:OFF_GATE]
        d_q = du[..., :DIFF_WIDTH].reshape(B, S, DIFF_HEADS, 2, DIFF_HEAD_DIM)
        d_k = du[..., DIFF_WIDTH:2 * DIFF_WIDTH].reshape(B, S, DIFF_HEADS, 2, DIFF_HEAD_DIM)
        d_v = du[..., 2 * DIFF_WIDTH:].reshape(B, S, DIFF_HEADS, 2 * DIFF_HEAD_DIM)
        o_df = diff_attention(d_q, d_k, d_v, t5_bias, diff_lambda[l], diff_subln_g[l], lam_init)
        gates = jax.nn.sigmoid((u[..., OFF_GATE:] + b_gate[l]).reshape(B, S, N_BRANCH, D_MODEL))
        br = jnp.stack([o_na, o_cv, o_df], axis=2)
        proj = jnp.einsum('bsnc,ncd->bsnd', br, w_branch[l])
        merged = jnp.sum(gates * proj, axis=2)
        x = x + rms_norm(merged @ w_out[l], ln_mix_post[l])
        hf = rms_norm(x, ln_ffn_pre[l])
        gu = hf @ w_ffn_in[l]
        g, up = jnp.split(gu, 2, axis=-1)
        x = x + rms_norm((jax.nn.silu(g) * up) @ w_ffn_out[l], ln_ffn_post[l])
    return x


def setup_inputs(seed: int = 0) -> dict:
    key = jax.random.key(seed)
    ks = jax.random.split(key, 20)
    nrm = lambda k, shape, s: jax.random.normal(k, shape, jnp.float32) * s
    gain = lambda k, shape: 1.0 + 0.05 * jax.random.normal(k, shape, jnp.float32)
    return {
        "x_prompt": nrm(ks[0], (BATCH, SEQ, D_MODEL), 1.0),
        "x_sample": nrm(ks[1], (DEC_BATCH, DEC_SEQ, D_MODEL), 1.0),
        "w_in": nrm(ks[2], (DEPTH, D_MODEL, IN_COLS), D_MODEL ** -0.5),
        "b_gate": nrm(ks[3], (DEPTH, GATE_COLS), 0.1),
        "na_rpb": nrm(ks[4], (DEPTH, NA_HEADS, 2 * NA_WIN_ROWS_MAX - 1, 2 * NA_WIN_COLS - 1), 0.2),
        "conv_dw_w": nrm(ks[5], (DEPTH, CONV_WIDTH, CONV_CH), CONV_WIDTH ** -0.5),
        "conv_dw_b": nrm(ks[6], (DEPTH, CONV_CH), 0.02),
        "conv_ln_g": gain(ks[7], (DEPTH, CONV_CH)),
        "conv_ln_b": nrm(ks[8], (DEPTH, CONV_CH), 0.02),
        "diff_lambda": nrm(ks[9], (DEPTH, 4, DIFF_HEAD_DIM), 0.1),
        "diff_subln_g": gain(ks[10], (DEPTH, 2 * DIFF_HEAD_DIM)),
        "t5_bias": nrm(ks[11], (T5_BUCKETS, DIFF_HEADS), 0.2),
        "w_branch": nrm(ks[12], (DEPTH, N_BRANCH, BRANCH_W, D_MODEL), BRANCH_W ** -0.5),
        "w_out": nrm(ks[13], (DEPTH, D_MODEL, D_MODEL), D_MODEL ** -0.5),
        "ln_mix_pre": gain(ks[14], (DEPTH, D_MODEL)),
        "ln_mix_post": gain(ks[15], (DEPTH, D_MODEL)),
        "ln_ffn_pre": gain(ks[16], (DEPTH, D_MODEL)),
        "ln_ffn_post": gain(ks[17], (DEPTH, D_MODEL)),
        "w_ffn_in": nrm(ks[18], (DEPTH, D_MODEL, 2 * FFN_HIDDEN), D_MODEL ** -0.5),
        "w_ffn_out": nrm(ks[19], (DEPTH, FFN_HIDDEN, D_MODEL), FFN_HIDDEN ** -0.5),
    }


def reference(x_prompt, x_sample, w_in, b_gate, na_rpb, conv_dw_w, conv_dw_b, conv_ln_g, conv_ln_b,
              diff_lambda, diff_subln_g, t5_bias, w_branch, w_out,
              ln_mix_pre, ln_mix_post, ln_ffn_pre, ln_ffn_post, w_ffn_in, w_ffn_out):
    y_prompt = trunk(x_prompt, w_in, b_gate, na_rpb, conv_dw_w, conv_dw_b, conv_ln_g, conv_ln_b,
                     diff_lambda, diff_subln_g, t5_bias, w_branch, w_out,
                     ln_mix_pre, ln_mix_post, ln_ffn_pre, ln_ffn_post, w_ffn_in, w_ffn_out)
    y_sample = trunk(x_sample, w_in, b_gate, na_rpb, conv_dw_w, conv_dw_b, conv_ln_g, conv_ln_b,
                     diff_lambda, diff_subln_g, t5_bias, w_branch, w_out,
                     ln_mix_pre, ln_mix_post, ln_ffn_pre, ln_ffn_post, w_ffn_in, w_ffn_out)
    return (y_prompt, y_sample)
```

```python
import functools
import math

import numpy as np
import jax
import jax.numpy as jnp
from jax import lax
from jax.experimental import pallas as pl
from jax.experimental.pallas import tpu as pltpu

F32 = jnp.float32
BF16 = jnp.bfloat16

D_MODEL = 1024
GRID_W = 64
NA_HEADS = 8
NA_HEAD_DIM = 64
NA_WIDTH = NA_HEADS * NA_HEAD_DIM
NA_WIN_ROWS_MAX = 8
NA_WIN_COLS = 16
CONV_CH = 512
CONV_WIDTH = 31
DIFF_HEADS = 4
DIFF_HEAD_DIM = 64
DIFF_WIDTH = DIFF_HEADS * 2 * DIFF_HEAD_DIM
N_BRANCH = 3
BRANCH_W = 512
T5_BUCKETS = 32
T5_MAX_DIST = 128
FFN_HIDDEN = 2816
EPS = 1e-6

OFF_CONV = 3 * NA_WIDTH
OFF_DIFF = OFF_CONV + 2 * CONV_CH
OFF_GATE = OFF_DIFF + 3 * DIFF_WIDTH
IN_COLS = OFF_GATE + N_BRANCH * D_MODEL

LANES = 128
VMEM_LIMIT = 56 * 1024 * 1024

IN_TM = 512
POST_TM = 256
NA_R = 4
NA_KR = 12
CONV_TS = 512
CONV_HALO = 16
CONV_RC = 64
DIFF_T = 512
NEG = -1e30


def _rms(x, g):
    return x * lax.rsqrt(jnp.mean(x * x, axis=-1, keepdims=True) + EPS) * g


def _const_spec(shape, index_map):
    return pl.BlockSpec(shape, index_map, pipeline_mode=pl.Buffered(1))


def _in_proj_kernel(x_ref, g_ref, w_ref, bg_ref,
                    naq_ref, nak_ref, nav_ref, cv_ref, dq_ref, dk_ref, dv_ref, gate_ref):
    h = _rms(x_ref[...], g_ref[...]).astype(BF16)

    def proj(c0, c1):
        return jnp.dot(h, w_ref[:, c0:c1], preferred_element_type=F32)

    naq_ref[...] = (proj(0, NA_WIDTH) * (NA_HEAD_DIM ** -0.5)).astype(BF16)
    nak_ref[...] = proj(NA_WIDTH, 2 * NA_WIDTH).astype(BF16)
    nav_ref[...] = proj(2 * NA_WIDTH, 3 * NA_WIDTH).astype(BF16)
    cv_ref[...] = proj(OFF_CONV, OFF_DIFF).astype(BF16)
    dq_ref[...] = (proj(OFF_DIFF, OFF_DIFF + DIFF_WIDTH) * (DIFF_HEAD_DIM ** -0.5)).astype(BF16)
    dk_ref[...] = proj(OFF_DIFF + DIFF_WIDTH, OFF_DIFF + 2 * DIFF_WIDTH).astype(BF16)
    dv_ref[...] = proj(OFF_DIFF + 2 * DIFF_WIDTH, OFF_GATE).astype(BF16)
    for b in range(N_BRANCH):
        c0 = OFF_GATE + b * D_MODEL
        gate = jax.nn.sigmoid(proj(c0, c0 + D_MODEL) + bg_ref[:, b * D_MODEL:(b + 1) * D_MODEL])
        gate_ref[:, b * D_MODEL:(b + 1) * D_MODEL] = gate.astype(BF16)


def _in_proj(x, g_pre, w_in, b_gate, layer):
    n_tok = x.shape[0]
    tm = IN_TM
    assert n_tok % tm == 0
    widths = (NA_WIDTH, NA_WIDTH, NA_WIDTH, 2 * CONV_CH, DIFF_WIDTH, DIFF_WIDTH, DIFF_WIDTH,
              N_BRANCH * D_MODEL)
    return pl.pallas_call(
        _in_proj_kernel,
        grid=(n_tok // tm,),
        in_specs=[
            pl.BlockSpec((tm, D_MODEL), lambda i: (i, 0)),
            _const_spec((None, 1, D_MODEL), lambda i: (layer, 0, 0)),
            _const_spec((None, D_MODEL, IN_COLS), lambda i: (layer, 0, 0)),
            _const_spec((None, 1, N_BRANCH * D_MODEL), lambda i: (layer, 0, 0)),
        ],
        out_specs=[pl.BlockSpec((tm, w), lambda i: (i, 0)) for w in widths],
        out_shape=[jax.ShapeDtypeStruct((n_tok, w), BF16) for w in widths],
        compiler_params=pltpu.CompilerParams(
            dimension_semantics=("arbitrary",), vmem_limit_bytes=VMEM_LIMIT),
        name="in_proj",
    )(x, g_pre, w_in, b_gate)


def _na_plan(rows):
    assert rows % NA_R == 0 and rows >= NA_KR
    kr = min(NA_WIN_ROWS_MAX, rows)
    n_groups = rows // NA_R
    variants, var_idx, win_start = [], [], []
    for g in range(n_groups):
        r0 = g * NA_R
        ws = int(np.clip(r0 - kr // 2, 0, rows - NA_KR))
        ro = -np.ones((NA_R, NA_KR), np.int32)
        for i in range(NA_R):
            r = r0 + i
            rs = int(np.clip(r - kr // 2, 0, rows - kr))
            assert ws <= rs and rs + kr <= ws + NA_KR
            for a in range(NA_KR):
                if rs <= ws + a < rs + kr:
                    ro[i, a] = ws + a - r + (NA_WIN_ROWS_MAX - 1)
        for v, known in enumerate(variants):
            if np.array_equal(known, ro):
                var_idx.append(v)
                break
        else:
            var_idx.append(len(variants))
            variants.append(ro)
        win_start.append(ws)
    return np.stack(variants), np.asarray(var_idx, np.int32), np.asarray(win_start, np.int32)


def _na_bias_tiles(rpb, row_off):
    col = np.arange(GRID_W)
    col_start = np.clip(col - NA_WIN_COLS // 2, 0, GRID_W - NA_WIN_COLS)
    kc = col[None, :]
    col_ok = (kc >= col_start[:, None]) & (kc < col_start[:, None] + NA_WIN_COLS)
    col_off = np.where(col_ok, kc - col[:, None] + (NA_WIN_COLS - 1), 0)
    row_ok = row_off >= 0
    ro = np.maximum(row_off, 0)
    t = rpb[:, ro[:, :, :, None, None], col_off[None, None, None, :, :]]
    ok = row_ok[:, :, :, None, None] & col_ok[None, None, None, :, :]
    t = jnp.where(ok[None], t, NEG)
    n_var = row_off.shape[0]
    return t.transpose(1, 0, 2, 4, 3, 5).reshape(n_var, NA_HEADS, NA_R * GRID_W, NA_KR * GRID_W)


def _na_kernel(var_ref, ws_ref, q_ref, k_ref, v_ref, bias_ref, o_ref):
    del var_ref
    n_q = NA_R * GRID_W
    n_k = NA_KR * GRID_W
    k0 = pl.multiple_of(ws_ref[pl.program_id(1)] * GRID_W, GRID_W)
    low_half = lax.broadcasted_iota(jnp.int32, (n_q, LANES), 1) < NA_HEAD_DIM
    for hp in range(NA_HEADS // 2):
        cols = slice(hp * LANES, (hp + 1) * LANES)
        q2 = q_ref[:, cols]
        k2 = k_ref[pl.ds(k0, n_k), cols]
        v2 = v_ref[pl.ds(k0, n_k), cols]
        halves = []
        for half in range(2):
            keep = low_half if half == 0 else jnp.logical_not(low_half)
            qm = jnp.where(keep, q2, jnp.zeros_like(q2))
            s = lax.dot_general(qm, k2, (((1,), (1,)), ((), ())), preferred_element_type=F32)
            s = s + bias_ref[2 * hp + half]
            m = jnp.max(s, axis=-1, keepdims=True)
            p = jnp.exp(s - m)
            l = jnp.sum(p, axis=-1, keepdims=True)
            o = jnp.dot(p.astype(BF16), v2, preferred_element_type=F32)
            halves.append(o / l)
        o_ref[:, cols] = jnp.where(low_half, halves[0], halves[1]).astype(BF16)


def _na(q, k, v, rpb):
    bsz, seq, _ = q.shape
    rows = seq // GRID_W
    row_off, var_idx, win_start = _na_plan(rows)
    bias = _na_bias_tiles(rpb, row_off)
    n_q = NA_R * GRID_W
    n_k = NA_KR * GRID_W
    grid_spec = pltpu.PrefetchScalarGridSpec(
        num_scalar_prefetch=2,
        grid=(bsz, rows // NA_R),
        in_specs=[
            pl.BlockSpec((None, n_q, NA_WIDTH), lambda b, g, var, ws: (b, g, 0)),
            pl.BlockSpec((None, seq, NA_WIDTH), lambda b, g, var, ws: (b, 0, 0)),
            pl.BlockSpec((None, seq, NA_WIDTH), lambda b, g, var, ws: (b, 0, 0)),
            pl.BlockSpec((None, NA_HEADS, n_q, n_k), lambda b, g, var, ws: (var[g], 0, 0, 0)),
        ],
        out_specs=pl.BlockSpec((None, n_q, NA_WIDTH), lambda b, g, var, ws: (b, g, 0)),
    )
    return pl.pallas_call(
        _na_kernel,
        grid_spec=grid_spec,
        out_shape=jax.ShapeDtypeStruct((bsz, seq, NA_WIDTH), BF16),
        compiler_params=pltpu.CompilerParams(
            dimension_semantics=("arbitrary", "arbitrary"), vmem_limit_bytes=VMEM_LIMIT),
        name="na",
    )(jnp.asarray(var_idx), jnp.asarray(win_start), q, k, v, bias)


def _glu(u):
    a = u[:, :CONV_CH].astype(F32)
    g = u[:, CONV_CH:].astype(F32)
    return a * jax.nn.sigmoid(g)


def _conv_kernel(u_ref, ul_ref, ur_ref, w_ref, b_ref, lg_ref, lb_ref, o_ref, xs_ref):
    t = pl.program_id(1)
    ts = CONV_TS
    xs_ref[0:CONV_HALO, :] = jnp.where(t > 0, _glu(ul_ref[...]), 0.0)
    xs_ref[CONV_HALO:CONV_HALO + ts, :] = _glu(u_ref[...])
    xs_ref[CONV_HALO + ts:, :] = jnp.where(t < pl.num_programs(1) - 1, _glu(ur_ref[...]), 0.0)
    first_tap = CONV_HALO - CONV_WIDTH // 2

    def chunk(rc, carry):
        r0 = pl.multiple_of(rc * CONV_RC, CONV_RC)
        win = xs_ref.at[pl.ds(r0, CONV_RC + 2 * CONV_HALO), :]
        acc = jnp.zeros((CONV_RC, CONV_CH), F32)
        for j in range(CONV_WIDTH):
            acc = acc + w_ref[j:j + 1, :] * win[first_tap + j:first_tap + j + CONV_RC, :]
        acc = acc + b_ref[...]
        mu = jnp.mean(acc, axis=-1, keepdims=True)
        xc = acc - mu
        y = xc * lax.rsqrt(jnp.mean(xc * xc, axis=-1, keepdims=True) + EPS)
        y = y * lg_ref[...] + lb_ref[...]
        o_ref[pl.ds(r0, CONV_RC), :] = (y * jax.nn.sigmoid(y)).astype(BF16)
        return carry

    lax.fori_loop(0, ts // CONV_RC, chunk, 0)


def _conv(u, dw_w, dw_b, ln_g, ln_b, layer):
    bsz, seq, _ = u.shape
    ts = CONV_TS
    assert seq % ts == 0 and ts % CONV_HALO == 0 and CONV_HALO >= CONV_WIDTH // 2
    n_t = seq // ts
    per_tile = ts // CONV_HALO
    n_halo = seq // CONV_HALO
    vec = lambda: _const_spec((None, 1, CONV_CH), lambda b, t: (layer, 0, 0))
    return pl.pallas_call(
        _conv_kernel,
        grid=(bsz, n_t),
        in_specs=[
            pl.BlockSpec((None, ts, 2 * CONV_CH), lambda b, t: (b, t, 0)),
            pl.BlockSpec((None, CONV_HALO, 2 * CONV_CH),
                         lambda b, t: (b, jnp.maximum(t * per_tile - 1, 0), 0)),
            pl.BlockSpec((None, CONV_HALO, 2 * CONV_CH),
                         lambda b, t: (b, jnp.minimum((t + 1) * per_tile, n_halo - 1), 0)),
            _const_spec((None, CONV_WIDTH, CONV_CH), lambda b, t: (layer, 0, 0)),
            vec(), vec(), vec(),
        ],
        out_specs=pl.BlockSpec((None, ts, CONV_CH), lambda b, t: (b, t, 0)),
        out_shape=jax.ShapeDtypeStruct((bsz, seq, CONV_CH), BF16),
        scratch_shapes=[pltpu.VMEM((ts + 2 * CONV_HALO, CONV_CH), F32)],
        compiler_params=pltpu.CompilerParams(
            dimension_semantics=("arbitrary", "arbitrary"), vmem_limit_bytes=VMEM_LIMIT),
        name="conv",
    )(u, u, u, dw_w, dw_b, ln_g, ln_b)


def _t5_bucket(rel):
    nb = T5_BUCKETS // 2
    max_exact = nb // 2
    ret = jnp.where(rel > 0, nb, 0)
    n = jnp.abs(rel)
    nf = jnp.maximum(n, 1).astype(jnp.float32)
    large = max_exact + (jnp.log(nf / max_exact) / math.log(T5_MAX_DIST / max_exact)
                         * (nb - max_exact)).astype(jnp.int32)
    large = jnp.minimum(large, nb - 1)
    return ret + jnp.where(n < max_exact, n, large)


def _diff_bias(t5_bias):
    t = DIFF_T
    assert t >= T5_MAX_DIST
    i = np.arange(t)[:, None]
    j = np.arange(t)[None, :]
    rel = np.stack([d * t + j - i for d in (-1, 0, 1)]).astype(np.int32)
    near = t5_bias[_t5_bucket(jnp.asarray(rel))].transpose(3, 0, 1, 2)
    far = t5_bias[_t5_bucket(jnp.asarray([-T5_MAX_DIST, T5_MAX_DIST], jnp.int32))]
    return near, far.reshape(-1)


def _diff_kernel(lam_ref, far_ref, q_ref, k_ref, v_ref, near_ref, g_ref, o_ref, *, lam_init, n_k):
    t = DIFF_T
    h = pl.program_id(1)
    qi = pl.program_id(2)
    q2 = q_ref[...]
    low_half = lax.broadcasted_iota(jnp.int32, (t, LANES), 1) < DIFF_HEAD_DIM
    zero = jnp.zeros_like(q2)
    qm = (jnp.where(low_half, q2, zero), jnp.where(low_half, zero, q2))

    def block(ki, carry, near, shift):
        r0 = pl.multiple_of(ki * t, t)
        kk = k_ref[pl.ds(r0, t), :]
        vv = v_ref[pl.ds(r0, t), :]
        out = []
        for c in range(2):
            m_old, l_old, acc_old = carry[c]
            s = lax.dot_general(qm[c], kk, (((1,), (1,)), ((), ())), preferred_element_type=F32)
            if near:
                s = s + near_ref[ki - qi + 1]
                m_new = jnp.maximum(m_old, jnp.max(s, axis=-1, keepdims=True))
                p = jnp.exp(s - m_new)
            else:
                m_new = jnp.maximum(m_old, jnp.max(s, axis=-1, keepdims=True) + shift)
                p = jnp.exp(s - (m_new - shift))
            alpha = jnp.exp(m_old - m_new)
            l_new = alpha * l_old + jnp.sum(p, axis=-1, keepdims=True)
            acc_new = alpha * acc_old + jnp.dot(p.astype(BF16), vv, preferred_element_type=F32)
            out.append((m_new, l_new, acc_new))
        return tuple(out)

    init = tuple((jnp.full((t, 1), NEG, F32), jnp.zeros((t, 1), F32), jnp.zeros((t, LANES), F32))
                 for _ in range(2))
    lo = jnp.maximum(qi - 1, 0)
    hi = jnp.minimum(qi + 2, n_k)
    carry = lax.fori_loop(0, lo, lambda ki, c: block(ki, c, False, far_ref[h]), init)
    carry = lax.fori_loop(lo, hi, lambda ki, c: block(ki, c, True, None), carry)
    carry = lax.fori_loop(hi, n_k, lambda ki, c: block(ki, c, False, far_ref[DIFF_HEADS + h]), carry)
    (_, l0, a0), (_, l1, a1) = carry
    o = a0 / l0 - lam_ref[0] * (a1 / l1)
    o = _rms(o, g_ref[...]) * (1.0 - lam_init)
    o_ref[...] = o.astype(BF16)


def _diff(q, k, v, near, far, lam_full, subln_g, lam_init, layer):
    bsz, seq, _ = q.shape
    t = DIFF_T
    assert seq % t == 0
    n_k = seq // t
    smem = pl.BlockSpec(memory_space=pltpu.SMEM)
    return pl.pallas_call(
        functools.partial(_diff_kernel, lam_init=lam_init, n_k=n_k),
        grid=(bsz, DIFF_HEADS, n_k),
        in_specs=[
            smem, smem,
            pl.BlockSpec((None, t, LANES), lambda b, h, i: (b, i, h)),
            pl.BlockSpec((None, seq, LANES), lambda b, h, i: (b, 0, h)),
            pl.BlockSpec((None, seq, LANES), lambda b, h, i: (b, 0, h)),
            pl.BlockSpec((None, 3, t, t), lambda b, h, i: (h, 0, 0, 0)),
            _const_spec((None, 1, LANES), lambda b, h, i: (layer, 0, 0)),
        ],
        out_specs=pl.BlockSpec((None, t, LANES), lambda b, h, i: (b, i, h)),
        out_shape=jax.ShapeDtypeStruct((bsz, seq, DIFF_WIDTH), BF16),
        compiler_params=pltpu.CompilerParams(
            dimension_semantics=("arbitrary", "arbitrary", "arbitrary"),
            vmem_limit_bytes=VMEM_LIMIT),
        name="diff",
    )(lam_full, far, q, k, v, near, subln_g)


_FFN_CHUNKS = ((0, 768), (768, 1536), (1536, 2304), (2304, FFN_HIDDEN))


def _post_kernel(x_ref, ona_ref, ocv_ref, odf_ref, gate_ref, wb_ref, wo_ref,
                 g_mix_ref, g_pre_ref, g_post_ref, wfi_ref, wfo_ref, o_ref, act_ref):
    merged = None
    for b, br_ref in enumerate((ona_ref, ocv_ref, odf_ref)):
        proj = jnp.dot(br_ref[...], wb_ref[b], preferred_element_type=F32)
        term = gate_ref[:, b * D_MODEL:(b + 1) * D_MODEL].astype(F32) * proj
        merged = term if merged is None else merged + term
    y = jnp.dot(merged.astype(BF16), wo_ref[...], preferred_element_type=F32)
    x1 = x_ref[...] + _rms(y, g_mix_ref[...])
    hf = _rms(x1, g_pre_ref[...]).astype(BF16)
    for c0, c1 in _FFN_CHUNKS:
        gate = jnp.dot(hf, wfi_ref[:, c0:c1], preferred_element_type=F32)
        up = jnp.dot(hf, wfi_ref[:, FFN_HIDDEN + c0:FFN_HIDDEN + c1], preferred_element_type=F32)
        act_ref[:, c0:c1] = (gate * jax.nn.sigmoid(gate) * up).astype(BF16)
    z = jnp.dot(act_ref[...], wfo_ref[...], preferred_element_type=F32)
    o_ref[...] = x1 + _rms(z, g_post_ref[...])


def _post(x, o_na, o_cv, o_df, gates, w_branch, w_out, g_mix, g_pre, g_post, w_ffn_in, w_ffn_out,
          layer):
    n_tok = x.shape[0]
    tm = POST_TM
    assert n_tok % tm == 0
    vec = lambda: _const_spec((None, 1, D_MODEL), lambda i: (layer, 0, 0))
    br = lambda: pl.BlockSpec((tm, BRANCH_W), lambda i: (i, 0))
    return pl.pallas_call(
        _post_kernel,
        grid=(n_tok // tm,),
        in_specs=[
            pl.BlockSpec((tm, D_MODEL), lambda i: (i, 0)),
            br(), br(), br(),
            pl.BlockSpec((tm, N_BRANCH * D_MODEL), lambda i: (i, 0)),
            _const_spec((None, N_BRANCH, BRANCH_W, D_MODEL), lambda i: (layer, 0, 0, 0)),
            _const_spec((None, D_MODEL, D_MODEL), lambda i: (layer, 0, 0)),
            vec(), vec(), vec(),
            _const_spec((None, D_MODEL, 2 * FFN_HIDDEN), lambda i: (layer, 0, 0)),
            _const_spec((None, FFN_HIDDEN, D_MODEL), lambda i: (layer, 0, 0)),
        ],
        out_specs=pl.BlockSpec((tm, D_MODEL), lambda i: (i, 0)),
        out_shape=jax.ShapeDtypeStruct((n_tok, D_MODEL), F32),
        scratch_shapes=[pltpu.VMEM((tm, FFN_HIDDEN), BF16)],
        compiler_params=pltpu.CompilerParams(
            dimension_semantics=("arbitrary",), vmem_limit_bytes=VMEM_LIMIT),
        name="post",
    )(x, o_na, o_cv, o_df, gates, w_branch, w_out, g_mix, g_pre, g_post, w_ffn_in, w_ffn_out)


def _trunk(x, p):
    bsz, seq, _ = x.shape
    n_tok = bsz * seq
    depth = p["w_in"].shape[0]
    x = x.reshape(n_tok, D_MODEL)
    for l in range(depth):
        lam_init = 0.8 - 0.6 * math.exp(-0.3 * l)
        naq, nak, nav, cvu, dq, dk, dv, gates = _in_proj(x, p["ln_mix_pre"], p["w_in"], p["b_gate"], l)
        seq3 = lambda a: a.reshape(bsz, seq, a.shape[-1])
        o_na = _na(seq3(naq), seq3(nak), seq3(nav), p["na_rpb"][l])
        o_cv = _conv(seq3(cvu), p["conv_dw_w"], p["conv_dw_b"], p["conv_ln_g"], p["conv_ln_b"], l)
        o_df = _diff(seq3(dq), seq3(dk), seq3(dv), p["diff_near"], p["diff_far"], p["lam_full"][l],
                     p["diff_subln_g"], lam_init, l)
        flat = lambda a: a.reshape(n_tok, a.shape[-1])
        x = _post(x, flat(o_na), flat(o_cv), flat(o_df), gates, p["w_branch"], p["w_out"],
                  p["ln_mix_post"], p["ln_ffn_pre"], p["ln_ffn_post"], p["w_ffn_in"], p["w_ffn_out"], l)
    return x.reshape(bsz, seq, D_MODEL)


def _prepare(w_in, b_gate, na_rpb, conv_dw_w, conv_dw_b, conv_ln_g, conv_ln_b,
             diff_lambda, diff_subln_g, t5_bias, w_branch, w_out,
             ln_mix_pre, ln_mix_post, ln_ffn_pre, ln_ffn_post, w_ffn_in, w_ffn_out):
    depth = w_in.shape[0]
    row = lambda a: a.reshape(depth, 1, a.shape[-1])
    lf = diff_lambda.astype(F32)
    lam_init = jnp.asarray([0.8 - 0.6 * math.exp(-0.3 * l) for l in range(depth)], F32)
    lam_full = (jnp.exp(jnp.sum(lf[:, 0] * lf[:, 1], axis=-1))
                - jnp.exp(jnp.sum(lf[:, 2] * lf[:, 3], axis=-1)) + lam_init)
    near, far = _diff_bias(t5_bias)
    return dict(
        w_in=w_in.astype(BF16), b_gate=row(b_gate), na_rpb=na_rpb,
        conv_dw_w=conv_dw_w, conv_dw_b=row(conv_dw_b), conv_ln_g=row(conv_ln_g), conv_ln_b=row(conv_ln_b),
        lam_full=lam_full.reshape(depth, 1), diff_subln_g=row(diff_subln_g),
        diff_near=near, diff_far=far,
        w_branch=w_branch.astype(BF16), w_out=w_out.astype(BF16),
        ln_mix_pre=row(ln_mix_pre), ln_mix_post=row(ln_mix_post),
        ln_ffn_pre=row(ln_ffn_pre), ln_ffn_post=row(ln_ffn_post),
        w_ffn_in=w_ffn_in.astype(BF16), w_ffn_out=w_ffn_out.astype(BF16),
    )


def kernel(x_prompt, x_sample, w_in, b_gate, na_rpb, conv_dw_w, conv_dw_b, conv_ln_g, conv_ln_b,
           diff_lambda, diff_subln_g, t5_bias, w_branch, w_out,
           ln_mix_pre, ln_mix_post, ln_ffn_pre, ln_ffn_post, w_ffn_in, w_ffn_out):
    p = _prepare(w_in, b_gate, na_rpb, conv_dw_w, conv_dw_b, conv_ln_g, conv_ln_b,
                 diff_lambda, diff_subln_g, t5_bias, w_branch, w_out,
                 ln_mix_pre, ln_mix_post, ln_ffn_pre, ln_ffn_post, w_ffn_in, w_ffn_out)
    return (_trunk(x_prompt, p), _trunk(x_sample, p))
```

```python
import functools
import math

import numpy as np
import jax
import jax.numpy as jnp
from jax import lax
from jax.experimental import pallas as pl
from jax.experimental.pallas import tpu as pltpu

F32 = jnp.float32
BF16 = jnp.bfloat16

D_MODEL = 1024
GRID_W = 64
NA_HEADS = 8
NA_HEAD_DIM = 64
NA_WIDTH = NA_HEADS * NA_HEAD_DIM
NA_WIN_ROWS_MAX = 8
NA_WIN_COLS = 16
CONV_CH = 512
CONV_WIDTH = 31
DIFF_HEADS = 4
DIFF_HEAD_DIM = 64
DIFF_WIDTH = DIFF_HEADS * 2 * DIFF_HEAD_DIM
N_BRANCH = 3
BRANCH_W = 512
T5_BUCKETS = 32
T5_MAX_DIST = 128
FFN_HIDDEN = 2816
EPS = 1e-6

OFF_CONV = 3 * NA_WIDTH
OFF_DIFF = OFF_CONV + 2 * CONV_CH
OFF_GATE = OFF_DIFF + 3 * DIFF_WIDTH
IN_COLS = OFF_GATE + N_BRANCH * D_MODEL

LANES = 128
SUBLANES = 8
VMEM_LIMIT = 56 * 1024 * 1024

IN_TM = 512
POST_TM = 256
NA_R = 4
NA_KR = 12
CONV_TS = 512
CONV_HALO = 16
CONV_RC = 128
DIFF_T = 512
NEG = -1e30
LOG2E = math.log2(math.e)

_EXACT = lax.Precision.HIGHEST


def _rms(x, g):
    return x * lax.rsqrt(jnp.mean(x * x, axis=-1, keepdims=True) + EPS) * g


def _const_spec(shape, index_map):
    return pl.BlockSpec(shape, index_map, pipeline_mode=pl.Buffered(1))


def _in_proj_kernel(x_ref, g_ref, w_ref, bg_ref,
                    naq_ref, nak_ref, nav_ref, cv_ref, dqt_ref, dk_ref, dvt_ref, gate_ref):
    h = _rms(x_ref[...], g_ref[...]).astype(BF16)

    def proj(c0, c1):
        return jnp.dot(h, w_ref[:, c0:c1], preferred_element_type=F32)

    naq_ref[...] = (proj(0, NA_WIDTH) * (NA_HEAD_DIM ** -0.5)).astype(BF16)
    nak_ref[...] = proj(NA_WIDTH, 2 * NA_WIDTH).astype(BF16)
    nav_ref[...] = proj(2 * NA_WIDTH, 3 * NA_WIDTH).astype(BF16)
    cv_ref[...] = proj(OFF_CONV, OFF_DIFF).astype(BF16)
    dqt = (proj(OFF_DIFF, OFF_DIFF + DIFF_WIDTH) * (DIFF_HEAD_DIM ** -0.5 * LOG2E)).T
    comp0 = lax.broadcasted_iota(jnp.int32, dqt.shape, 0) % (2 * DIFF_HEAD_DIM) < DIFF_HEAD_DIM
    dqt_ref[:, :IN_TM] = jnp.where(comp0, dqt, 0.0).astype(BF16)
    dqt_ref[:, IN_TM:] = jnp.where(comp0, 0.0, dqt).astype(BF16)
    dk_ref[...] = proj(OFF_DIFF + DIFF_WIDTH, OFF_DIFF + 2 * DIFF_WIDTH).astype(BF16)
    dvt_ref[...] = proj(OFF_DIFF + 2 * DIFF_WIDTH, OFF_GATE).T.astype(BF16)
    for b in range(N_BRANCH):
        c0 = OFF_GATE + b * D_MODEL
        gate = jax.nn.sigmoid(proj(c0, c0 + D_MODEL) + bg_ref[:, b * D_MODEL:(b + 1) * D_MODEL])
        gate_ref[:, b * D_MODEL:(b + 1) * D_MODEL] = gate.astype(BF16)


def _in_proj(x, g_pre, w_in, b_gate, layer):
    n_tok = x.shape[0]
    tm = IN_TM
    assert n_tok % tm == 0
    n_t = n_tok // tm
    tok = lambda w: (pl.BlockSpec((tm, w), lambda i: (i, 0)), jax.ShapeDtypeStruct((n_tok, w), BF16))
    tok_t = lambda w, c: (pl.BlockSpec((None, w, c), lambda i: (i, 0, 0)),
                          jax.ShapeDtypeStruct((n_t, w, c), BF16))
    outs = [tok(NA_WIDTH), tok(NA_WIDTH), tok(NA_WIDTH), tok(2 * CONV_CH),
            tok_t(DIFF_WIDTH, 2 * tm), tok(DIFF_WIDTH), tok_t(DIFF_WIDTH, tm), tok(N_BRANCH * D_MODEL)]
    return pl.pallas_call(
        _in_proj_kernel,
        grid=(n_t,),
        in_specs=[
            pl.BlockSpec((tm, D_MODEL), lambda i: (i, 0)),
            _const_spec((None, 1, D_MODEL), lambda i: (layer, 0, 0)),
            _const_spec((None, D_MODEL, IN_COLS), lambda i: (layer, 0, 0)),
            _const_spec((None, 1, N_BRANCH * D_MODEL), lambda i: (layer, 0, 0)),
        ],
        out_specs=[o[0] for o in outs],
        out_shape=[o[1] for o in outs],
        compiler_params=pltpu.CompilerParams(
            dimension_semantics=("arbitrary",), vmem_limit_bytes=VMEM_LIMIT),
        name="in_proj",
    )(x, g_pre, w_in, b_gate)


def _na_plan(rows):
    assert rows % NA_R == 0 and rows >= NA_KR
    kr = min(NA_WIN_ROWS_MAX, rows)
    n_groups = rows // NA_R
    variants, var_idx, win_start = [], [], []
    for g in range(n_groups):
        r0 = g * NA_R
        ws = int(np.clip(r0 - kr // 2, 0, rows - NA_KR))
        ro = -np.ones((NA_R, NA_KR), np.int32)
        for i in range(NA_R):
            r = r0 + i
            rs = int(np.clip(r - kr // 2, 0, rows - kr))
            assert ws <= rs and rs + kr <= ws + NA_KR
            for a in range(NA_KR):
                if rs <= ws + a < rs + kr:
                    ro[i, a] = ws + a - r + (NA_WIN_ROWS_MAX - 1)
        for v, known in enumerate(variants):
            if np.array_equal(known, ro):
                var_idx.append(v)
                break
        else:
            var_idx.append(len(variants))
            variants.append(ro)
        win_start.append(ws)
    return np.stack(variants), np.asarray(var_idx, np.int32), np.asarray(win_start, np.int32)


def _na_bias_tiles(rpb, row_off):
    n_var = row_off.shape[0]
    n_rows = 2 * NA_WIN_ROWS_MAX - 1
    n_cols = 2 * NA_WIN_COLS - 1
    col = np.arange(GRID_W)
    col_start = np.clip(col - NA_WIN_COLS // 2, 0, GRID_W - NA_WIN_COLS)
    kc = col[None, :]
    col_ok = (kc >= col_start[:, None]) & (kc < col_start[:, None] + NA_WIN_COLS)
    col_off = kc - col[:, None] + (NA_WIN_COLS - 1)
    col_hot = (col_ok[None] & (col_off[None] == np.arange(n_cols)[:, None, None])).astype(np.float32)
    row_hot = (row_off.reshape(-1)[:, None] == np.arange(n_rows)[None, :]).astype(np.float32)
    t = jnp.einsum("xr,hrc->hxc", row_hot, rpb, precision=_EXACT)
    t = jnp.einsum("hxc,cqk->hxqk", t, col_hot, precision=_EXACT)
    ok = (row_off >= 0).reshape(-1)[:, None, None] & col_ok[None]
    t = jnp.where(ok[None], t, NEG).reshape(NA_HEADS, n_var, NA_R, NA_KR, GRID_W, GRID_W)
    return t.transpose(1, 0, 2, 4, 3, 5).reshape(n_var, NA_HEADS, NA_R * GRID_W, NA_KR * GRID_W)


def _na_kernel(var_ref, ws_ref, q_ref, k_ref, v_ref, bias_ref, o_ref):
    del var_ref
    n_q = NA_R * GRID_W
    n_k = NA_KR * GRID_W
    k0 = pl.multiple_of(ws_ref[pl.program_id(1)] * GRID_W, GRID_W)
    low_half = lax.broadcasted_iota(jnp.int32, (n_q, LANES), 1) < NA_HEAD_DIM
    for hp in range(NA_HEADS // 2):
        cols = slice(hp * LANES, (hp + 1) * LANES)
        q2 = q_ref[:, cols]
        k2 = k_ref[pl.ds(k0, n_k), cols]
        v2 = v_ref[pl.ds(k0, n_k), cols]
        halves = []
        for half in range(2):
            keep = low_half if half == 0 else jnp.logical_not(low_half)
            qm = jnp.where(keep, q2, jnp.zeros_like(q2))
            s = lax.dot_general(qm, k2, (((1,), (1,)), ((), ())), preferred_element_type=F32)
            s = s + bias_ref[2 * hp + half]
            m = jnp.max(s, axis=-1, keepdims=True)
            p = jnp.exp(s - m)
            l = jnp.sum(p, axis=-1, keepdims=True)
            o = jnp.dot(p.astype(BF16), v2, preferred_element_type=F32)
            halves.append(o / l)
        o_ref[:, cols] = jnp.where(low_half, halves[0], halves[1]).astype(BF16)


def _na(q, k, v, rpb):
    bsz, seq, _ = q.shape
    rows = seq // GRID_W
    row_off, var_idx, win_start = _na_plan(rows)
    bias = _na_bias_tiles(rpb, row_off)
    n_q = NA_R * GRID_W
    n_k = NA_KR * GRID_W
    grid_spec = pltpu.PrefetchScalarGridSpec(
        num_scalar_prefetch=2,
        grid=(bsz, rows // NA_R),
        in_specs=[
            pl.BlockSpec((None, n_q, NA_WIDTH), lambda b, g, var, ws: (b, g, 0)),
            pl.BlockSpec((None, seq, NA_WIDTH), lambda b, g, var, ws: (b, 0, 0)),
            pl.BlockSpec((None, seq, NA_WIDTH), lambda b, g, var, ws: (b, 0, 0)),
            pl.BlockSpec((None, NA_HEADS, n_q, n_k), lambda b, g, var, ws: (var[g], 0, 0, 0)),
        ],
        out_specs=pl.BlockSpec((None, n_q, NA_WIDTH), lambda b, g, var, ws: (b, g, 0)),
    )
    return pl.pallas_call(
        _na_kernel,
        grid_spec=grid_spec,
        out_shape=jax.ShapeDtypeStruct((bsz, seq, NA_WIDTH), BF16),
        compiler_params=pltpu.CompilerParams(
            dimension_semantics=("arbitrary", "arbitrary"), vmem_limit_bytes=VMEM_LIMIT),
        name="na",
    )(jnp.asarray(var_idx), jnp.asarray(win_start), q, k, v, bias)


def _glu(u):
    a = u[:, :CONV_CH].astype(F32)
    g = u[:, CONV_CH:].astype(F32)
    return a * jax.nn.sigmoid(g)


def _conv_kernel(u_ref, ul_ref, ur_ref, w_ref, b_ref, lg_ref, lb_ref, o_ref, xs_ref):
    t = pl.program_id(1)
    ts = CONV_TS
    xs_ref[0:CONV_HALO, :] = jnp.where(t > 0, _glu(ul_ref[...]), 0.0)
    xs_ref[CONV_HALO:CONV_HALO + ts, :] = _glu(u_ref[...])
    xs_ref[CONV_HALO + ts:, :] = jnp.where(t < pl.num_programs(1) - 1, _glu(ur_ref[...]), 0.0)
    first = CONV_HALO - CONV_WIDTH // 2
    ext = CONV_RC + SUBLANES

    def chunk(rc, carry):
        r0 = pl.multiple_of(rc * CONV_RC, CONV_RC)
        acc = None
        for s in range(SUBLANES):
            part = None
            for j in range(s, CONV_WIDTH, SUBLANES):
                term = w_ref[j:j + 1, :] * xs_ref[pl.ds(r0 + (j - s), ext), :]
                part = term if part is None else part + term
            shift = first + s
            part = part[shift:shift + CONV_RC, :]
            acc = part if acc is None else acc + part
        acc = acc + b_ref[...]
        mu = jnp.mean(acc, axis=-1, keepdims=True)
        xc = acc - mu
        y = xc * lax.rsqrt(jnp.mean(xc * xc, axis=-1, keepdims=True) + EPS)
        y = y * lg_ref[...] + lb_ref[...]
        o_ref[pl.ds(r0, CONV_RC), :] = (y * jax.nn.sigmoid(y)).astype(BF16)
        return carry

    lax.fori_loop(0, ts // CONV_RC, chunk, 0)


def _conv(u, dw_w, dw_b, ln_g, ln_b, layer):
    bsz, seq, _ = u.shape
    ts = CONV_TS
    assert seq % ts == 0 and ts % CONV_HALO == 0 and ts % CONV_RC == 0
    assert CONV_HALO >= CONV_WIDTH // 2 and 2 * CONV_HALO >= SUBLANES + (CONV_WIDTH - 1) // SUBLANES * SUBLANES
    assert CONV_HALO - CONV_WIDTH // 2 + SUBLANES - 1 + CONV_RC <= CONV_RC + SUBLANES
    n_t = seq // ts
    per_tile = ts // CONV_HALO
    n_halo = seq // CONV_HALO
    vec = lambda: _const_spec((None, 1, CONV_CH), lambda b, t: (layer, 0, 0))
    return pl.pallas_call(
        _conv_kernel,
        grid=(bsz, n_t),
        in_specs=[
            pl.BlockSpec((None, ts, 2 * CONV_CH), lambda b, t: (b, t, 0)),
            pl.BlockSpec((None, CONV_HALO, 2 * CONV_CH),
                         lambda b, t: (b, jnp.maximum(t * per_tile - 1, 0), 0)),
            pl.BlockSpec((None, CONV_HALO, 2 * CONV_CH),
                         lambda b, t: (b, jnp.minimum((t + 1) * per_tile, n_halo - 1), 0)),
            _const_spec((None, CONV_WIDTH, CONV_CH), lambda b, t: (layer, 0, 0)),
            vec(), vec(), vec(),
        ],
        out_specs=pl.BlockSpec((None, ts, CONV_CH), lambda b, t: (b, t, 0)),
        out_shape=jax.ShapeDtypeStruct((bsz, seq, CONV_CH), BF16),
        scratch_shapes=[pltpu.VMEM((ts + 2 * CONV_HALO, CONV_CH), F32)],
        compiler_params=pltpu.CompilerParams(
            dimension_semantics=("arbitrary", "arbitrary"), vmem_limit_bytes=VMEM_LIMIT),
        name="conv",
    )(u, u, u, dw_w, dw_b, ln_g, ln_b)


def _t5_bucket(rel):
    nb = T5_BUCKETS // 2
    max_exact = nb // 2
    ret = jnp.where(rel > 0, nb, 0)
    n = jnp.abs(rel)
    nf = jnp.maximum(n, 1).astype(jnp.float32)
    large = max_exact + (jnp.log(nf / max_exact) / math.log(T5_MAX_DIST / max_exact)
                         * (nb - max_exact)).astype(jnp.int32)
    large = jnp.minimum(large, nb - 1)
    return ret + jnp.where(n < max_exact, n, large)


def _toeplitz(g, n):
    lead = g.shape[:-1]
    flat = jnp.tile(g, (1,) * len(lead) + (n,))
    return flat[..., :n * (2 * n - 1)].reshape(lead + (n, 2 * n - 1))[..., :n]


def _diff_bias(t5_bias):
    t = DIFF_T
    assert t >= T5_MAX_DIST
    m = np.arange(2 * t)
    q_minus_k = np.where(m < t, m, m - 2 * t)
    rel = np.stack([d * t - q_minus_k for d in (-1, 0, 1)]).astype(np.int32)
    g = t5_bias[_t5_bucket(jnp.asarray(rel))]
    near = _toeplitz(g.transpose(2, 0, 1), t)
    far = t5_bias[_t5_bucket(jnp.asarray([-T5_MAX_DIST, T5_MAX_DIST], jnp.int32))]
    far = jnp.broadcast_to(far.T[:, :, None, None], (DIFF_HEADS, 2, t, t))
    return jnp.concatenate([far[:, :1], near, far[:, 1:]], axis=1) * LOG2E


def _diff_kernel(lam_ref, qt_ref, k_ref, vt_ref, bias_ref, g_ref, o_ref,
                 s_ref, smax_ref, p_ref, alpha_ref, m_ref, l_ref, acc_ref, *, lam_init, n_k):
    t = DIFF_T
    n_pairs = n_k * n_k

    def scores(e, par):
        qi, ki = e // n_k, e % n_k
        kk = k_ref[pl.ds(pl.multiple_of(ki * t, t), t), :]
        s = jnp.dot(kk, qt_ref[qi], preferred_element_type=F32)
        bias = bias_ref[jnp.clip(ki - qi, -2, 2) + 2]
        for c in range(2):
            sc = s[:, c * t:(c + 1) * t] + bias
            s_ref[par, :, c * t:(c + 1) * t] = sc
            smax_ref[par, :, c * t:(c + 1) * t] = jnp.max(sc, axis=0, keepdims=True)

    def softmax(e, par):
        qi, ki = e // n_k, e % n_k
        m_old = jnp.where(ki == 0, NEG, m_ref[...])
        m_new = jnp.maximum(m_old, smax_ref[par])
        alpha = jnp.exp2(m_old - m_new)
        p = jnp.exp2(s_ref[par] - m_new)
        m_ref[...] = m_new
        l_ref[qi % 2] = alpha * l_ref[qi % 2] + jnp.sum(p, axis=0, keepdims=True)
        alpha_ref[par] = alpha
        p_ref[par] = p.astype(BF16)

    def accumulate(e, par):
        ki = e % n_k
        pv = jnp.dot(vt_ref[ki], p_ref[par], preferred_element_type=F32)
        acc_ref[...] = alpha_ref[par] * acc_ref[...] + pv

    def finalize(qi):
        l = l_ref[qi % 2]
        ot = (acc_ref[:, :t] / l[:, :t] - lam_ref[0] * (acc_ref[:, t:] / l[:, t:]))
        o = _rms(ot.T, g_ref[...]) * (1.0 - lam_init)
        o_ref[pl.ds(pl.multiple_of(qi * t, t), t), :] = o.astype(BF16)

    acc_ref[...] = jnp.zeros(acc_ref.shape, F32)
    l_ref[...] = jnp.zeros(l_ref.shape, F32)
    alpha_ref[1] = jnp.zeros(alpha_ref.shape[1:], F32)
    p_ref[1] = jnp.zeros(p_ref.shape[1:], BF16)
    scores(0, 0)

    def step(e, par):
        accumulate(jnp.maximum(e - 1, 0), 1 - par)
        softmax(e, par)
        scores(jnp.minimum(e + 1, n_pairs - 1), 1 - par)

        @pl.when((e % n_k == 0) & (e > 0))
        def _():
            finalize(e // n_k - 1)

    def two_steps(j, carry):
        step(2 * j, 0)
        step(2 * j + 1, 1)
        return carry

    lax.fori_loop(0, n_pairs // 2, two_steps, 0)
    accumulate(n_pairs - 1, 1)
    finalize(n_k - 1)


def _diff(qt2, k, vt, bias, lam_full, subln_g, lam_init, layer):
    bsz, seq, _ = k.shape
    t = DIFF_T
    assert seq % t == 0 and IN_TM == t
    n_k = seq // t
    assert (n_k * n_k) % 2 == 0
    smem = pl.BlockSpec(memory_space=pltpu.SMEM)
    return pl.pallas_call(
        functools.partial(_diff_kernel, lam_init=lam_init, n_k=n_k),
        grid=(DIFF_HEADS, bsz),
        in_specs=[
            smem,
            pl.BlockSpec((n_k, LANES, 2 * t), lambda h, b: (b, h, 0)),
            pl.BlockSpec((None, seq, LANES), lambda h, b: (b, 0, h)),
            pl.BlockSpec((n_k, LANES, t), lambda h, b: (b, h, 0)),
            pl.BlockSpec((None, 5, t, t), lambda h, b: (h, 0, 0, 0)),
            _const_spec((None, 1, LANES), lambda h, b: (layer, 0, 0)),
        ],
        out_specs=pl.BlockSpec((None, seq, LANES), lambda h, b: (b, 0, h)),
        out_shape=jax.ShapeDtypeStruct((bsz, seq, DIFF_WIDTH), BF16),
        scratch_shapes=[
            pltpu.VMEM((2, t, 2 * t), F32),
            pltpu.VMEM((2, 1, 2 * t), F32),
            pltpu.VMEM((2, t, 2 * t), BF16),
            pltpu.VMEM((2, 1, 2 * t), F32),
            pltpu.VMEM((1, 2 * t), F32),
            pltpu.VMEM((2, 1, 2 * t), F32),
            pltpu.VMEM((LANES, 2 * t), F32),
        ],
        compiler_params=pltpu.CompilerParams(
            dimension_semantics=("arbitrary", "arbitrary"), vmem_limit_bytes=VMEM_LIMIT),
        name="diff",
    )(lam_full, qt2, k, vt, bias, subln_g)


_FFN_CHUNKS = ((0, 768), (768, 1536), (1536, 2304), (2304, FFN_HIDDEN))


def _post_kernel(x_ref, ona_ref, ocv_ref, odf_ref, gate_ref, wb_ref, wo_ref,
                 g_mix_ref, g_pre_ref, g_post_ref, wfi_ref, wfo_ref, o_ref, act_ref):
    merged = None
    for b, br_ref in enumerate((ona_ref, ocv_ref, odf_ref)):
        proj = jnp.dot(br_ref[...], wb_ref[b], preferred_element_type=F32)
        term = gate_ref[:, b * D_MODEL:(b + 1) * D_MODEL].astype(F32) * proj
        merged = term if merged is None else merged + term
    y = jnp.dot(merged.astype(BF16), wo_ref[...], preferred_element_type=F32)
    x1 = x_ref[...] + _rms(y, g_mix_ref[...])
    hf = _rms(x1, g_pre_ref[...]).astype(BF16)
    for c0, c1 in _FFN_CHUNKS:
        gate = jnp.dot(hf, wfi_ref[:, c0:c1], preferred_element_type=F32)
        up = jnp.dot(hf, wfi_ref[:, FFN_HIDDEN + c0:FFN_HIDDEN + c1], preferred_element_type=F32)
        act_ref[:, c0:c1] = (gate * jax.nn.sigmoid(gate) * up).astype(BF16)
    z = jnp.dot(act_ref[...], wfo_ref[...], preferred_element_type=F32)
    o_ref[...] = x1 + _rms(z, g_post_ref[...])


def _post(x, o_na, o_cv, o_df, gates, w_branch, w_out, g_mix, g_pre, g_post, w_ffn_in, w_ffn_out,
          layer):
    n_tok = x.shape[0]
    tm = POST_TM
    assert n_tok % tm == 0
    vec = lambda: _const_spec((None, 1, D_MODEL), lambda i: (layer, 0, 0))
    br = lambda: pl.BlockSpec((tm, BRANCH_W), lambda i: (i, 0))
    return pl.pallas_call(
        _post_kernel,
        grid=(n_tok // tm,),
        in_specs=[
            pl.BlockSpec((tm, D_MODEL), lambda i: (i, 0)),
            br(), br(), br(),
            pl.BlockSpec((tm, N_BRANCH * D_MODEL), lambda i: (i, 0)),
            _const_spec((None, N_BRANCH, BRANCH_W, D_MODEL), lambda i: (layer, 0, 0, 0)),
            _const_spec((None, D_MODEL, D_MODEL), lambda i: (layer, 0, 0)),
            vec(), vec(), vec(),
            _const_spec((None, D_MODEL, 2 * FFN_HIDDEN), lambda i: (layer, 0, 0)),
            _const_spec((None, FFN_HIDDEN, D_MODEL), lambda i: (layer, 0, 0)),
        ],
        out_specs=pl.BlockSpec((tm, D_MODEL), lambda i: (i, 0)),
        out_shape=jax.ShapeDtypeStruct((n_tok, D_MODEL), F32),
        scratch_shapes=[pltpu.VMEM((tm, FFN_HIDDEN), BF16)],
        compiler_params=pltpu.CompilerParams(
            dimension_semantics=("arbitrary",), vmem_limit_bytes=VMEM_LIMIT),
        name="post",
    )(x, o_na, o_cv, o_df, gates, w_branch, w_out, g_mix, g_pre, g_post, w_ffn_in, w_ffn_out)


def _trunk(x, p):
    bsz, seq, _ = x.shape
    n_tok = bsz * seq
    depth = p["w_in"].shape[0]
    x = x.reshape(n_tok, D_MODEL)
    for l in range(depth):
        lam_init = 0.8 - 0.6 * math.exp(-0.3 * l)
        naq, nak, nav, cvu, dqt, dk, dvt, gates = _in_proj(x, p["ln_mix_pre"], p["w_in"], p["b_gate"], l)
        seq3 = lambda a: a.reshape(bsz, seq, a.shape[-1])
        o_na = _na(seq3(naq), seq3(nak), seq3(nav), p["na_rpb"][l])
        o_cv = _conv(seq3(cvu), p["conv_dw_w"], p["conv_dw_b"], p["conv_ln_g"], p["conv_ln_b"], l)
        o_df = _diff(dqt, seq3(dk), dvt, p["diff_bias"], p["lam_full"][l], p["diff_subln_g"],
                     lam_init, l)
        flat = lambda a: a.reshape(n_tok, a.shape[-1])
        x = _post(x, flat(o_na), flat(o_cv), flat(o_df), gates, p["w_branch"], p["w_out"],
                  p["ln_mix_post"], p["ln_ffn_pre"], p["ln_ffn_post"], p["w_ffn_in"], p["w_ffn_out"], l)
    return x.reshape(bsz, seq, D_MODEL)


def _prepare(w_in, b_gate, na_rpb, conv_dw_w, conv_dw_b, conv_ln_g, conv_ln_b,
             diff_lambda, diff_subln_g, t5_bias, w_branch, w_out,
             ln_mix_pre, ln_mix_post, ln_ffn_pre, ln_ffn_post, w_ffn_in, w_ffn_out):
    depth = w_in.shape[0]
    row = lambda a: a.reshape(depth, 1, a.shape[-1])
    lf = diff_lambda.astype(F32)
    lam_init = jnp.asarray([0.8 - 0.6 * math.exp(-0.3 * l) for l in range(depth)], F32)
    lam_full = (jnp.exp(jnp.sum(lf[:, 0] * lf[:, 1], axis=-1))
                - jnp.exp(jnp.sum(lf[:, 2] * lf[:, 3], axis=-1)) + lam_init)
    return dict(
        w_in=w_in.astype(BF16), b_gate=row(b_gate), na_rpb=na_rpb,
        conv_dw_w=conv_dw_w, conv_dw_b=row(conv_dw_b), conv_ln_g=row(conv_ln_g), conv_ln_b=row(conv_ln_b),
        lam_full=lam_full.reshape(depth, 1), diff_subln_g=row(diff_subln_g),
        diff_bias=_diff_bias(t5_bias),
        w_branch=w_branch.astype(BF16), w_out=w_out.astype(BF16),
        ln_mix_pre=row(ln_mix_pre), ln_mix_post=row(ln_mix_post),
        ln_ffn_pre=row(ln_ffn_pre), ln_ffn_post=row(ln_ffn_post),
        w_ffn_in=w_ffn_in.astype(BF16), w_ffn_out=w_ffn_out.astype(BF16),
    )


def kernel(x_prompt, x_sample, w_in, b_gate, na_rpb, conv_dw_w, conv_dw_b, conv_ln_g, conv_ln_b,
           diff_lambda, diff_subln_g, t5_bias, w_branch, w_out,
           ln_mix_pre, ln_mix_post, ln_ffn_pre, ln_ffn_post, w_ffn_in, w_ffn_out):
    p = _prepare(w_in, b_gate, na_rpb, conv_dw_w, conv_dw_b, conv_ln_g, conv_ln_b,
                 diff_lambda, diff_subln_g, t5_bias, w_branch, w_out,
                 ln_mix_pre, ln_mix_post, ln_ffn_pre, ln_ffn_post, w_ffn_in, w_ffn_out)
    return (_trunk(x_prompt, p), _trunk(x_sample, p))
```

```python
import functools
import math

import numpy as np
import jax
import jax.numpy as jnp
from jax import lax
from jax.experimental import pallas as pl
from jax.experimental.pallas import tpu as pltpu

F32 = jnp.float32
BF16 = jnp.bfloat16

D_MODEL = 1024
GRID_W = 64
NA_HEADS = 8
NA_HEAD_DIM = 64
NA_WIDTH = NA_HEADS * NA_HEAD_DIM
NA_WIN_ROWS_MAX = 8
NA_WIN_COLS = 16
CONV_CH = 512
CONV_WIDTH = 31
DIFF_HEADS = 4
DIFF_HEAD_DIM = 64
DIFF_WIDTH = DIFF_HEADS * 2 * DIFF_HEAD_DIM
N_BRANCH = 3
BRANCH_W = 512
T5_BUCKETS = 32
T5_MAX_DIST = 128
FFN_HIDDEN = 2816
EPS = 1e-6

OFF_CONV = 3 * NA_WIDTH
OFF_DIFF = OFF_CONV + 2 * CONV_CH
OFF_GATE = OFF_DIFF + 3 * DIFF_WIDTH
IN_COLS = OFF_GATE + N_BRANCH * D_MODEL

LANES = 128
SUBLANES = 8
VMEM_LIMIT = 56 * 1024 * 1024

IN_TM = 512
POST_TM = 512
NA_R = 4
NA_KR = 12
CONV_TS = 512
CONV_HALO = 16
CONV_RC = 128
DIFF_T = 512
DIFF_STRIP = 256
DIFF_ONES = 16
DIFF_UNROLL = 2
NEG = -1e30
LOG2E = math.log2(math.e)

_EXACT = lax.Precision.HIGHEST


def _rms(x, g):
    return x * lax.rsqrt(jnp.mean(x * x, axis=-1, keepdims=True) + EPS) * g


def _const_spec(shape, index_map):
    return pl.BlockSpec(shape, index_map, pipeline_mode=pl.Buffered(1))


def _in_proj_kernel(x_ref, g_ref, w_ref, bg_ref,
                    naq_ref, nak_ref, nav_ref, cv_ref, dqt_ref, dk_ref, dvt_ref, gate_ref):
    h = _rms(x_ref[...], g_ref[...]).astype(BF16)

    def proj(c0, c1):
        return jnp.dot(h, w_ref[:, c0:c1], preferred_element_type=F32)

    naq_ref[...] = (proj(0, NA_WIDTH) * (NA_HEAD_DIM ** -0.5 * LOG2E)).astype(BF16)
    nak_ref[...] = proj(NA_WIDTH, 2 * NA_WIDTH).astype(BF16)
    nav_ref[...] = proj(2 * NA_WIDTH, 3 * NA_WIDTH).astype(BF16)
    cv_ref[...] = proj(OFF_CONV, OFF_DIFF).astype(BF16)
    dqt = (proj(OFF_DIFF, OFF_DIFF + DIFF_WIDTH) * (DIFF_HEAD_DIM ** -0.5 * LOG2E)).T
    comp0 = lax.broadcasted_iota(jnp.int32, dqt.shape, 0) % (2 * DIFF_HEAD_DIM) < DIFF_HEAD_DIM
    dqt_ref[:, :IN_TM] = jnp.where(comp0, dqt, 0.0).astype(BF16)
    dqt_ref[:, IN_TM:] = jnp.where(comp0, 0.0, dqt).astype(BF16)
    dk_ref[...] = proj(OFF_DIFF + DIFF_WIDTH, OFF_DIFF + 2 * DIFF_WIDTH).astype(BF16)
    dvt_ref[...] = proj(OFF_DIFF + 2 * DIFF_WIDTH, OFF_GATE).T.astype(BF16)
    for b in range(N_BRANCH):
        c0 = OFF_GATE + b * D_MODEL
        gate = jax.nn.sigmoid(proj(c0, c0 + D_MODEL) + bg_ref[:, b * D_MODEL:(b + 1) * D_MODEL])
        gate_ref[:, b * D_MODEL:(b + 1) * D_MODEL] = gate.astype(BF16)


def _in_proj(x, g_pre, w_in, b_gate, layer):
    n_tok = x.shape[0]
    tm = IN_TM
    assert n_tok % tm == 0
    n_t = n_tok // tm
    tok = lambda w: (pl.BlockSpec((tm, w), lambda i: (i, 0)), jax.ShapeDtypeStruct((n_tok, w), BF16))
    tok_t = lambda w, c: (pl.BlockSpec((None, w, c), lambda i: (i, 0, 0)),
                          jax.ShapeDtypeStruct((n_t, w, c), BF16))
    outs = [tok(NA_WIDTH), tok(NA_WIDTH), tok(NA_WIDTH), tok(2 * CONV_CH),
            tok_t(DIFF_WIDTH, 2 * tm), tok(DIFF_WIDTH), tok_t(DIFF_WIDTH, tm), tok(N_BRANCH * D_MODEL)]
    return pl.pallas_call(
        _in_proj_kernel,
        grid=(n_t,),
        in_specs=[
            pl.BlockSpec((tm, D_MODEL), lambda i: (i, 0)),
            _const_spec((None, 1, D_MODEL), lambda i: (layer, 0, 0)),
            _const_spec((None, D_MODEL, IN_COLS), lambda i: (layer, 0, 0)),
            _const_spec((None, 1, N_BRANCH * D_MODEL), lambda i: (layer, 0, 0)),
        ],
        out_specs=[o[0] for o in outs],
        out_shape=[o[1] for o in outs],
        compiler_params=pltpu.CompilerParams(
            dimension_semantics=("arbitrary",), vmem_limit_bytes=VMEM_LIMIT),
        name="in_proj",
    )(x, g_pre, w_in, b_gate)


def _na_plan(rows):
    assert rows % NA_R == 0 and rows >= NA_KR
    kr = min(NA_WIN_ROWS_MAX, rows)
    n_groups = rows // NA_R
    variants, var_idx, win_start = [], [], []
    for g in range(n_groups):
        r0 = g * NA_R
        ws = int(np.clip(r0 - kr // 2, 0, rows - NA_KR))
        ro = -np.ones((NA_R, NA_KR), np.int32)
        for i in range(NA_R):
            r = r0 + i
            rs = int(np.clip(r - kr // 2, 0, rows - kr))
            assert ws <= rs and rs + kr <= ws + NA_KR
            for a in range(NA_KR):
                if rs <= ws + a < rs + kr:
                    ro[i, a] = ws + a - r + (NA_WIN_ROWS_MAX - 1)
        for v, known in enumerate(variants):
            if np.array_equal(known, ro):
                var_idx.append(v)
                break
        else:
            var_idx.append(len(variants))
            variants.append(ro)
        win_start.append(ws)
    return np.stack(variants), np.asarray(var_idx, np.int32), np.asarray(win_start, np.int32)


def _na_bias_tiles(rpb, row_off):
    n_var = row_off.shape[0]
    n_rows = 2 * NA_WIN_ROWS_MAX - 1
    n_cols = 2 * NA_WIN_COLS - 1
    col = np.arange(GRID_W)
    col_start = np.clip(col - NA_WIN_COLS // 2, 0, GRID_W - NA_WIN_COLS)
    kc = col[None, :]
    col_ok = (kc >= col_start[:, None]) & (kc < col_start[:, None] + NA_WIN_COLS)
    col_off = kc - col[:, None] + (NA_WIN_COLS - 1)
    col_hot = (col_ok[None] & (col_off[None] == np.arange(n_cols)[:, None, None])).astype(np.float32)
    row_hot = (row_off.reshape(-1)[:, None] == np.arange(n_rows)[None, :]).astype(np.float32)
    t = jnp.einsum("xr,hrc->hxc", row_hot, rpb, precision=_EXACT)
    t = jnp.einsum("hxc,cqk->hxqk", t, col_hot, precision=_EXACT)
    ok = (row_off >= 0).reshape(-1)[:, None, None] & col_ok[None]
    t = jnp.where(ok[None], t * LOG2E, NEG).reshape(NA_HEADS, n_var, NA_R, NA_KR, GRID_W, GRID_W)
    return t.transpose(1, 0, 2, 4, 3, 5).reshape(n_var, NA_HEADS, NA_R * GRID_W, NA_KR * GRID_W)


def _na_kernel(var_ref, ws_ref, q_ref, k_ref, v_ref, bias_ref, o_ref):
    del var_ref
    n_q = NA_R * GRID_W
    n_k = NA_KR * GRID_W
    k0 = pl.multiple_of(ws_ref[pl.program_id(1)] * GRID_W, GRID_W)
    low_half = lax.broadcasted_iota(jnp.int32, (n_q, LANES), 1) < NA_HEAD_DIM
    for hp in range(NA_HEADS // 2):
        cols = slice(hp * LANES, (hp + 1) * LANES)
        q2 = q_ref[:, cols]
        k2 = k_ref[pl.ds(k0, n_k), cols]
        v2 = jnp.concatenate([v_ref[pl.ds(k0, n_k), cols], jnp.ones((n_k, LANES), BF16)], axis=1)
        halves = []
        for half in range(2):
            keep = low_half if half == 0 else jnp.logical_not(low_half)
            qm = jnp.where(keep, q2, jnp.zeros_like(q2))
            s = lax.dot_general(qm, k2, (((1,), (1,)), ((), ())), preferred_element_type=F32)
            s = s + bias_ref[2 * hp + half]
            p = jnp.exp2(s - jnp.max(s, axis=-1, keepdims=True))
            o = jnp.dot(p.astype(BF16), v2, preferred_element_type=F32)
            halves.append(o[:, :LANES] / o[:, LANES:])
        o_ref[:, cols] = jnp.where(low_half, halves[0], halves[1]).astype(BF16)


def _na(q, k, v, rpb):
    bsz, seq, _ = q.shape
    rows = seq // GRID_W
    row_off, var_idx, win_start = _na_plan(rows)
    bias = _na_bias_tiles(rpb, row_off)
    n_q = NA_R * GRID_W
    n_k = NA_KR * GRID_W
    grid_spec = pltpu.PrefetchScalarGridSpec(
        num_scalar_prefetch=2,
        grid=(bsz, rows // NA_R),
        in_specs=[
            pl.BlockSpec((None, n_q, NA_WIDTH), lambda b, g, var, ws: (b, g, 0)),
            pl.BlockSpec((None, seq, NA_WIDTH), lambda b, g, var, ws: (b, 0, 0)),
            pl.BlockSpec((None, seq, NA_WIDTH), lambda b, g, var, ws: (b, 0, 0)),
            pl.BlockSpec((None, NA_HEADS, n_q, n_k), lambda b, g, var, ws: (var[g], 0, 0, 0)),
        ],
        out_specs=pl.BlockSpec((None, n_q, NA_WIDTH), lambda b, g, var, ws: (b, g, 0)),
    )
    return pl.pallas_call(
        _na_kernel,
        grid_spec=grid_spec,
        out_shape=jax.ShapeDtypeStruct((bsz, seq, NA_WIDTH), BF16),
        compiler_params=pltpu.CompilerParams(
            dimension_semantics=("arbitrary", "arbitrary"), vmem_limit_bytes=VMEM_LIMIT),
        name="na",
    )(jnp.asarray(var_idx), jnp.asarray(win_start), q, k, v, bias)


def _glu(u):
    a = u[:, :CONV_CH].astype(F32)
    g = u[:, CONV_CH:].astype(F32)
    return a * jax.nn.sigmoid(g)


def _conv_kernel(u_ref, ul_ref, ur_ref, w_ref, b_ref, lg_ref, lb_ref, o_ref, xs_ref):
    t = pl.program_id(1)
    ts = CONV_TS
    xs_ref[0:CONV_HALO, :] = jnp.where(t > 0, _glu(ul_ref[...]), 0.0)
    xs_ref[CONV_HALO:CONV_HALO + ts, :] = _glu(u_ref[...])
    xs_ref[CONV_HALO + ts:, :] = jnp.where(t < pl.num_programs(1) - 1, _glu(ur_ref[...]), 0.0)
    first = CONV_HALO - CONV_WIDTH // 2
    ext = CONV_RC + SUBLANES

    def chunk(rc, carry):
        r0 = pl.multiple_of(rc * CONV_RC, CONV_RC)
        acc = None
        for s in range(SUBLANES):
            part = None
            for j in range(s, CONV_WIDTH, SUBLANES):
                term = w_ref[j:j + 1, :] * xs_ref[pl.ds(r0 + (j - s), ext), :]
                part = term if part is None else part + term
            shift = first + s
            part = part[shift:shift + CONV_RC, :]
            acc = part if acc is None else acc + part
        acc = acc + b_ref[...]
        mu = jnp.mean(acc, axis=-1, keepdims=True)
        xc = acc - mu
        y = xc * lax.rsqrt(jnp.mean(xc * xc, axis=-1, keepdims=True) + EPS)
        y = y * lg_ref[...] + lb_ref[...]
        o_ref[pl.ds(r0, CONV_RC), :] = (y * jax.nn.sigmoid(y)).astype(BF16)
        return carry

    lax.fori_loop(0, ts // CONV_RC, chunk, 0)


def _conv(u, dw_w, dw_b, ln_g, ln_b, layer):
    bsz, seq, _ = u.shape
    ts = CONV_TS
    assert seq % ts == 0 and ts % CONV_HALO == 0 and ts % CONV_RC == 0
    assert CONV_HALO >= CONV_WIDTH // 2 and 2 * CONV_HALO >= SUBLANES + (CONV_WIDTH - 1) // SUBLANES * SUBLANES
    assert CONV_HALO - CONV_WIDTH // 2 + SUBLANES - 1 + CONV_RC <= CONV_RC + SUBLANES
    n_t = seq // ts
    per_tile = ts // CONV_HALO
    n_halo = seq // CONV_HALO
    vec = lambda: _const_spec((None, 1, CONV_CH), lambda b, t: (layer, 0, 0))
    return pl.pallas_call(
        _conv_kernel,
        grid=(bsz, n_t),
        in_specs=[
            pl.BlockSpec((None, ts, 2 * CONV_CH), lambda b, t: (b, t, 0)),
            pl.BlockSpec((None, CONV_HALO, 2 * CONV_CH),
                         lambda b, t: (b, jnp.maximum(t * per_tile - 1, 0), 0)),
            pl.BlockSpec((None, CONV_HALO, 2 * CONV_CH),
                         lambda b, t: (b, jnp.minimum((t + 1) * per_tile, n_halo - 1), 0)),
            _const_spec((None, CONV_WIDTH, CONV_CH), lambda b, t: (layer, 0, 0)),
            vec(), vec(), vec(),
        ],
        out_specs=pl.BlockSpec((None, ts, CONV_CH), lambda b, t: (b, t, 0)),
        out_shape=jax.ShapeDtypeStruct((bsz, seq, CONV_CH), BF16),
        scratch_shapes=[pltpu.VMEM((ts + 2 * CONV_HALO, CONV_CH), F32)],
        compiler_params=pltpu.CompilerParams(
            dimension_semantics=("arbitrary", "arbitrary"), vmem_limit_bytes=VMEM_LIMIT),
        name="conv",
    )(u, u, u, dw_w, dw_b, ln_g, ln_b)


def _t5_bucket(rel):
    nb = T5_BUCKETS // 2
    max_exact = nb // 2
    ret = jnp.where(rel > 0, nb, 0)
    n = jnp.abs(rel)
    nf = jnp.maximum(n, 1).astype(jnp.float32)
    large = max_exact + (jnp.log(nf / max_exact) / math.log(T5_MAX_DIST / max_exact)
                         * (nb - max_exact)).astype(jnp.int32)
    large = jnp.minimum(large, nb - 1)
    return ret + jnp.where(n < max_exact, n, large)


def _toeplitz(g, n):
    lead = g.shape[:-1]
    flat = jnp.tile(g, (1,) * len(lead) + (n,))
    return flat[..., :n * (2 * n - 1)].reshape(lead + (n, 2 * n - 1))[..., :n]


def _diff_bias(t5_bias):
    t = DIFF_T
    assert t >= T5_MAX_DIST
    m = np.arange(2 * t)
    q_minus_k = np.where(m < t, m, m - 2 * t)
    rel = np.stack([d * t - q_minus_k for d in (-1, 0, 1)]).astype(np.int32)
    g = t5_bias[_t5_bucket(jnp.asarray(rel))]
    near = _toeplitz(g.transpose(2, 0, 1), t)
    far = t5_bias[_t5_bucket(jnp.asarray([-T5_MAX_DIST, T5_MAX_DIST], jnp.int32))]
    far = jnp.broadcast_to(far.T[:, :, None, None], (DIFF_HEADS, 2, t, t))
    return jnp.concatenate([far[:, :1], near, far[:, 1:]], axis=1) * LOG2E


def _diff_kernel(lam_ref, qt_ref, k_ref, vt_ref, bias_ref, g_ref, o_ref,
                 s_ref, smax_ref, p_ref, alpha_ref, m_ref, acc_ref, vt1_ref, *, lam_init, n_k):
    t = DIFF_T
    n_pairs = n_k * n_k

    strips = [slice(c0, c0 + DIFF_STRIP) for c0 in range(0, 2 * t, DIFF_STRIP)]

    def scores(e, par, cols):
        qi, ki = e // n_k, e % n_k
        kk = k_ref[pl.ds(pl.multiple_of(ki * t, t), t), :]
        s = jnp.dot(kk, qt_ref[qi, :, cols], preferred_element_type=F32)
        q0 = cols.start % t
        s = s + bias_ref[jnp.clip(ki - qi, -2, 2) + 2, :, q0:q0 + DIFF_STRIP]
        s_ref[par, :, cols] = s
        smax_ref[par, :, cols] = jnp.max(s, axis=0, keepdims=True)

    def softmax(e, par, cols):
        m_old = jnp.where(e % n_k == 0, NEG, m_ref[:, cols])
        m_new = jnp.maximum(m_old, smax_ref[par, :, cols])
        m_ref[:, cols] = m_new
        alpha_ref[par, :, cols] = jnp.exp2(m_old - m_new)
        p_ref[par, :, cols] = jnp.exp2(s_ref[par, :, cols] - m_new).astype(BF16)

    def accumulate(e, par, cols):
        pv = jnp.dot(vt1_ref[e % n_k], p_ref[par, :, cols], preferred_element_type=F32)
        qi = e // n_k
        acc_ref[qi, :, cols] = alpha_ref[par, :, cols] * acc_ref[qi, :, cols] + pv

    def finalize(qi, carry):
        inv_l = 1.0 / acc_ref[qi, LANES:LANES + 1, :]
        ot = (acc_ref[qi, :LANES, :t] * inv_l[:, :t]
              - lam_ref[0] * (acc_ref[qi, :LANES, t:] * inv_l[:, t:]))
        o = _rms(ot.T, g_ref[...]) * (1.0 - lam_init)
        o_ref[pl.ds(pl.multiple_of(qi * t, t), t), :] = o.astype(BF16)
        return carry

    vt1_ref[:, :LANES, :] = vt_ref[...]
    vt1_ref[:, LANES:, :] = jnp.ones((n_k, DIFF_ONES, t), BF16)
    acc_ref[...] = jnp.zeros(acc_ref.shape, F32)
    alpha_ref[1] = jnp.zeros(alpha_ref.shape[1:], F32)
    p_ref[1] = jnp.zeros(p_ref.shape[1:], BF16)
    for cols in strips:
        scores(0, 0, cols)

    def step(e, par):
        for cols in strips:
            accumulate(jnp.maximum(e - 1, 0), 1 - par, cols)
        for cols in strips:
            softmax(e, par, cols)
        for cols in strips:
            scores(jnp.minimum(e + 1, n_pairs - 1), 1 - par, cols)

    def steps(j, carry):
        for u in range(DIFF_UNROLL):
            step(DIFF_UNROLL * j + u, u % 2)
        return carry

    lax.fori_loop(0, n_pairs // DIFF_UNROLL, steps, 0)
    for cols in strips:
        accumulate(n_pairs - 1, 1, cols)
    lax.fori_loop(0, n_k, finalize, 0)


def _diff(qt2, k, vt, bias, lam_full, subln_g, lam_init, layer):
    bsz, seq, _ = k.shape
    t = DIFF_T
    assert seq % t == 0 and IN_TM == t
    n_k = seq // t
    assert DIFF_UNROLL % 2 == 0 and (n_k * n_k) % DIFF_UNROLL == 0
    smem = pl.BlockSpec(memory_space=pltpu.SMEM)
    return pl.pallas_call(
        functools.partial(_diff_kernel, lam_init=lam_init, n_k=n_k),
        grid=(DIFF_HEADS, bsz),
        in_specs=[
            smem,
            pl.BlockSpec((n_k, LANES, 2 * t), lambda h, b: (b, h, 0)),
            pl.BlockSpec((None, seq, LANES), lambda h, b: (b, 0, h)),
            pl.BlockSpec((n_k, LANES, t), lambda h, b: (b, h, 0)),
            pl.BlockSpec((None, 5, t, t), lambda h, b: (h, 0, 0, 0)),
            _const_spec((None, 1, LANES), lambda h, b: (layer, 0, 0)),
        ],
        out_specs=pl.BlockSpec((None, seq, LANES), lambda h, b: (b, 0, h)),
        out_shape=jax.ShapeDtypeStruct((bsz, seq, DIFF_WIDTH), BF16),
        scratch_shapes=[
            pltpu.VMEM((2, t, 2 * t), F32),
            pltpu.VMEM((2, 1, 2 * t), F32),
            pltpu.VMEM((2, t, 2 * t), BF16),
            pltpu.VMEM((2, 1, 2 * t), F32),
            pltpu.VMEM((1, 2 * t), F32),
            pltpu.VMEM((n_k, LANES + DIFF_ONES, 2 * t), F32),
            pltpu.VMEM((n_k, LANES + DIFF_ONES, t), BF16),
        ],
        compiler_params=pltpu.CompilerParams(
            dimension_semantics=("arbitrary", "arbitrary"), vmem_limit_bytes=VMEM_LIMIT),
        name="diff",
    )(lam_full, qt2, k, vt, bias, subln_g)


_FFN_CHUNKS = ((0, 768), (768, 1536), (1536, 2304), (2304, FFN_HIDDEN))


def _post_kernel(x_ref, ona_ref, ocv_ref, odf_ref, gate_ref, wb_ref, wo_ref,
                 g_mix_ref, g_pre_ref, g_post_ref, wfi_ref, wfo_ref, o_ref, act_ref):
    merged = None
    for b, br_ref in enumerate((ona_ref, ocv_ref, odf_ref)):
        proj = jnp.dot(br_ref[...], wb_ref[b], preferred_element_type=F32)
        term = gate_ref[:, b * D_MODEL:(b + 1) * D_MODEL].astype(F32) * proj
        merged = term if merged is None else merged + term
    y = jnp.dot(merged.astype(BF16), wo_ref[...], preferred_element_type=F32)
    x1 = x_ref[...] + _rms(y, g_mix_ref[...])
    hf = _rms(x1, g_pre_ref[...]).astype(BF16)
    for c0, c1 in _FFN_CHUNKS:
        gate = jnp.dot(hf, wfi_ref[:, c0:c1], preferred_element_type=F32)
        up = jnp.dot(hf, wfi_ref[:, FFN_HIDDEN + c0:FFN_HIDDEN + c1], preferred_element_type=F32)
        act_ref[:, c0:c1] = (gate * jax.nn.sigmoid(gate) * up).astype(BF16)
    z = jnp.dot(act_ref[...], wfo_ref[...], preferred_element_type=F32)
    o_ref[...] = x1 + _rms(z, g_post_ref[...])


def _post(x, o_na, o_cv, o_df, gates, w_branch, w_out, g_mix, g_pre, g_post, w_ffn_in, w_ffn_out,
          layer):
    n_tok = x.shape[0]
    tm = POST_TM
    assert n_tok % tm == 0
    vec = lambda: _const_spec((None, 1, D_MODEL), lambda i: (layer, 0, 0))
    br = lambda: pl.BlockSpec((tm, BRANCH_W), lambda i: (i, 0))
    return pl.pallas_call(
        _post_kernel,
        grid=(n_tok // tm,),
        in_specs=[
            pl.BlockSpec((tm, D_MODEL), lambda i: (i, 0)),
            br(), br(), br(),
            pl.BlockSpec((tm, N_BRANCH * D_MODEL), lambda i: (i, 0)),
            _const_spec((None, N_BRANCH, BRANCH_W, D_MODEL), lambda i: (layer, 0, 0, 0)),
            _const_spec((None, D_MODEL, D_MODEL), lambda i: (layer, 0, 0)),
            vec(), vec(), vec(),
            _const_spec((None, D_MODEL, 2 * FFN_HIDDEN), lambda i: (layer, 0, 0)),
            _const_spec((None, FFN_HIDDEN, D_MODEL), lambda i: (layer, 0, 0)),
        ],
        out_specs=pl.BlockSpec((tm, D_MODEL), lambda i: (i, 0)),
        out_shape=jax.ShapeDtypeStruct((n_tok, D_MODEL), F32),
        scratch_shapes=[pltpu.VMEM((tm, FFN_HIDDEN), BF16)],
        compiler_params=pltpu.CompilerParams(
            dimension_semantics=("arbitrary",), vmem_limit_bytes=VMEM_LIMIT),
        name="post",
    )(x, o_na, o_cv, o_df, gates, w_branch, w_out, g_mix, g_pre, g_post, w_ffn_in, w_ffn_out)


def _trunk(x, p):
    bsz, seq, _ = x.shape
    n_tok = bsz * seq
    depth = p["w_in"].shape[0]
    x = x.reshape(n_tok, D_MODEL)
    for l in range(depth):
        lam_init = 0.8 - 0.6 * math.exp(-0.3 * l)
        naq, nak, nav, cvu, dqt, dk, dvt, gates = _in_proj(x, p["ln_mix_pre"], p["w_in"], p["b_gate"], l)
        seq3 = lambda a: a.reshape(bsz, seq, a.shape[-1])
        o_na = _na(seq3(naq), seq3(nak), seq3(nav), p["na_rpb"][l])
        o_cv = _conv(seq3(cvu), p["conv_dw_w"], p["conv_dw_b"], p["conv_ln_g"], p["conv_ln_b"], l)
        o_df = _diff(dqt, seq3(dk), dvt, p["diff_bias"], p["lam_full"][l], p["diff_subln_g"],
                     lam_init, l)
        flat = lambda a: a.reshape(n_tok, a.shape[-1])
        x = _post(x, flat(o_na), flat(o_cv), flat(o_df), gates, p["w_branch"], p["w_out"],
                  p["ln_mix_post"], p["ln_ffn_pre"], p["ln_ffn_post"], p["w_ffn_in"], p["w_ffn_out"], l)
    return x.reshape(bsz, seq, D_MODEL)


def _prepare(w_in, b_gate, na_rpb, conv_dw_w, conv_dw_b, conv_ln_g, conv_ln_b,
             diff_lambda, diff_subln_g, t5_bias, w_branch, w_out,
             ln_mix_pre, ln_mix_post, ln_ffn_pre, ln_ffn_post, w_ffn_in, w_ffn_out):
    depth = w_in.shape[0]
    row = lambda a: a.reshape(depth, 1, a.shape[-1])
    lf = diff_lambda.astype(F32)
    lam_init = jnp.asarray([0.8 - 0.6 * math.exp(-0.3 * l) for l in range(depth)], F32)
    lam_full = (jnp.exp(jnp.sum(lf[:, 0] * lf[:, 1], axis=-1))
                - jnp.exp(jnp.sum(lf[:, 2] * lf[:, 3], axis=-1)) + lam_init)
    return dict(
        w_in=w_in.astype(BF16), b_gate=row(b_gate), na_rpb=na_rpb,
        conv_dw_w=conv_dw_w, conv_dw_b=row(conv_dw_b), conv_ln_g=row(conv_ln_g), conv_ln_b=row(conv_ln_b),
        lam_full=lam_full.reshape(depth, 1), diff_subln_g=row(diff_subln_g),
        diff_bias=_diff_bias(t5_bias),
        w_branch=w_branch.astype(BF16), w_out=w_out.astype(BF16),
        ln_mix_pre=row(ln_mix_pre), ln_mix_post=row(ln_mix_post),
        ln_ffn_pre=row(ln_ffn_pre), ln_ffn_post=row(ln_ffn_post),
        w_ffn_in=w_ffn_in.astype(BF16), w_ffn_out=w_ffn_out.astype(BF16),
    )


def kernel(x_prompt, x_sample, w_in, b_gate, na_rpb, conv_dw_w, conv_dw_b, conv_ln_g, conv_ln_b,
           diff_lambda, diff_subln_g, t5_bias, w_branch, w_out,
           ln_mix_pre, ln_mix_post, ln_ffn_pre, ln_ffn_post, w_ffn_in, w_ffn_out):
    p = _prepare(w_in, b_gate, na_rpb, conv_dw_w, conv_dw_b, conv_ln_g, conv_ln_b,
                 diff_lambda, diff_subln_g, t5_bias, w_branch, w_out,
                 ln_mix_pre, ln_mix_post, ln_ffn_pre, ln_ffn_post, w_ffn_in, w_ffn_out)
    return (_trunk(x_prompt, p), _trunk(x_sample, p))
```

```python
import functools
import math

import numpy as np
import jax
import jax.numpy as jnp
from jax import lax
from jax.experimental import pallas as pl
from jax.experimental.pallas import tpu as pltpu

F32 = jnp.float32
BF16 = jnp.bfloat16

D_MODEL = 1024
GRID_W = 64
NA_HEADS = 8
NA_HEAD_DIM = 64
NA_WIDTH = NA_HEADS * NA_HEAD_DIM
NA_WIN_ROWS_MAX = 8
NA_WIN_COLS = 16
CONV_CH = 512
CONV_WIDTH = 31
DIFF_HEADS = 4
DIFF_HEAD_DIM = 64
DIFF_WIDTH = DIFF_HEADS * 2 * DIFF_HEAD_DIM
N_BRANCH = 3
BRANCH_W = 512
T5_BUCKETS = 32
T5_MAX_DIST = 128
FFN_HIDDEN = 2816
EPS = 1e-6

OFF_CONV = 3 * NA_WIDTH
OFF_DIFF = OFF_CONV + 2 * CONV_CH
OFF_GATE = OFF_DIFF + 3 * DIFF_WIDTH
IN_COLS = OFF_GATE + N_BRANCH * D_MODEL

LANES = 128
SUBLANES = 8
VMEM_LIMIT = 56 * 1024 * 1024

IN_TM = 512
POST_TM = 512
NA_R = 4
NA_KR = 12
CONV_TS = 512
CONV_HALO = 16
CONV_RC = 128
DIFF_T = 512
DIFF_STRIP = 256
DIFF_ONES = 16
DIFF_UNROLL = 2
NEG = -1e30
LOG2E = math.log2(math.e)

_EXACT = lax.Precision.HIGHEST


def _rms(x, g):
    return x * lax.rsqrt(jnp.mean(x * x, axis=-1, keepdims=True) + EPS) * g


def _const_spec(shape, index_map):
    return pl.BlockSpec(shape, index_map, pipeline_mode=pl.Buffered(1))


def _in_proj_kernel(x_ref, g_ref, w_ref, bg_ref,
                    naq_ref, nak_ref, nav_ref, cv_ref, dqt_ref, dk_ref, dvt_ref, gate_ref):
    h = _rms(x_ref[...], g_ref[...]).astype(BF16)

    def proj(c0, c1):
        return jnp.dot(h, w_ref[:, c0:c1], preferred_element_type=F32)

    naq_ref[...] = (proj(0, NA_WIDTH) * (NA_HEAD_DIM ** -0.5 * LOG2E)).astype(BF16)
    nak_ref[...] = proj(NA_WIDTH, 2 * NA_WIDTH).astype(BF16)
    nav_ref[...] = proj(2 * NA_WIDTH, 3 * NA_WIDTH).astype(BF16)
    cv_ref[...] = proj(OFF_CONV, OFF_DIFF).astype(BF16)
    dqt = (proj(OFF_DIFF, OFF_DIFF + DIFF_WIDTH) * (DIFF_HEAD_DIM ** -0.5 * LOG2E)).T
    comp0 = lax.broadcasted_iota(jnp.int32, dqt.shape, 0) % (2 * DIFF_HEAD_DIM) < DIFF_HEAD_DIM
    dqt_ref[:, :IN_TM] = jnp.where(comp0, dqt, 0.0).astype(BF16)
    dqt_ref[:, IN_TM:] = jnp.where(comp0, 0.0, dqt).astype(BF16)
    dk_ref[...] = proj(OFF_DIFF + DIFF_WIDTH, OFF_DIFF + 2 * DIFF_WIDTH).astype(BF16)
    dvt_ref[...] = proj(OFF_DIFF + 2 * DIFF_WIDTH, OFF_GATE).T.astype(BF16)
    for b in range(N_BRANCH):
        c0 = OFF_GATE + b * D_MODEL
        gate = jax.nn.sigmoid(proj(c0, c0 + D_MODEL) + bg_ref[:, b * D_MODEL:(b + 1) * D_MODEL])
        gate_ref[:, b * D_MODEL:(b + 1) * D_MODEL] = gate.astype(BF16)


def _in_proj(x, g_pre, w_in, b_gate, layer):
    n_tok = x.shape[0]
    tm = IN_TM
    assert n_tok % tm == 0
    n_t = n_tok // tm
    tok = lambda w: (pl.BlockSpec((tm, w), lambda i: (i, 0)), jax.ShapeDtypeStruct((n_tok, w), BF16))
    tok_t = lambda w, c: (pl.BlockSpec((None, w, c), lambda i: (i, 0, 0)),
                          jax.ShapeDtypeStruct((n_t, w, c), BF16))
    outs = [tok(NA_WIDTH), tok(NA_WIDTH), tok(NA_WIDTH), tok(2 * CONV_CH),
            tok_t(DIFF_WIDTH, 2 * tm), tok(DIFF_WIDTH), tok_t(DIFF_WIDTH, tm), tok(N_BRANCH * D_MODEL)]
    return pl.pallas_call(
        _in_proj_kernel,
        grid=(n_t,),
        in_specs=[
            pl.BlockSpec((tm, D_MODEL), lambda i: (i, 0)),
            _const_spec((None, 1, D_MODEL), lambda i: (layer, 0, 0)),
            _const_spec((None, D_MODEL, IN_COLS), lambda i: (layer, 0, 0)),
            _const_spec((None, 1, N_BRANCH * D_MODEL), lambda i: (layer, 0, 0)),
        ],
        out_specs=[o[0] for o in outs],
        out_shape=[o[1] for o in outs],
        compiler_params=pltpu.CompilerParams(
            dimension_semantics=("arbitrary",), vmem_limit_bytes=VMEM_LIMIT),
        name="in_proj",
    )(x, g_pre, w_in, b_gate)


def _na_plan(rows):
    assert rows % NA_R == 0 and rows >= NA_KR
    kr = min(NA_WIN_ROWS_MAX, rows)
    n_groups = rows // NA_R
    variants, var_idx, win_start = [], [], []
    for g in range(n_groups):
        r0 = g * NA_R
        ws = int(np.clip(r0 - kr // 2, 0, rows - NA_KR))
        ro = -np.ones((NA_R, NA_KR), np.int32)
        for i in range(NA_R):
            r = r0 + i
            rs = int(np.clip(r - kr // 2, 0, rows - kr))
            assert ws <= rs and rs + kr <= ws + NA_KR
            for a in range(NA_KR):
                if rs <= ws + a < rs + kr:
                    ro[i, a] = ws + a - r + (NA_WIN_ROWS_MAX - 1)
        for v, known in enumerate(variants):
            if np.array_equal(known, ro):
                var_idx.append(v)
                break
        else:
            var_idx.append(len(variants))
            variants.append(ro)
        win_start.append(ws)
    return np.stack(variants), np.asarray(var_idx, np.int32), np.asarray(win_start, np.int32)


def _na_bias_tiles(rpb, row_off):
    n_var = row_off.shape[0]
    n_rows = 2 * NA_WIN_ROWS_MAX - 1
    n_cols = 2 * NA_WIN_COLS - 1
    col = np.arange(GRID_W)
    col_start = np.clip(col - NA_WIN_COLS // 2, 0, GRID_W - NA_WIN_COLS)
    kc = col[None, :]
    col_ok = (kc >= col_start[:, None]) & (kc < col_start[:, None] + NA_WIN_COLS)
    col_off = kc - col[:, None] + (NA_WIN_COLS - 1)
    col_hot = (col_ok[None] & (col_off[None] == np.arange(n_cols)[:, None, None])).astype(np.float32)
    row_hot = (row_off.reshape(-1)[:, None] == np.arange(n_rows)[None, :]).astype(np.float32)
    t = jnp.einsum("xr,hrc->hxc", row_hot, rpb, precision=_EXACT)
    t = jnp.einsum("hxc,cqk->hxqk", t, col_hot, precision=_EXACT)
    ok = (row_off >= 0).reshape(-1)[:, None, None] & col_ok[None]
    t = jnp.where(ok[None], t * LOG2E, NEG).reshape(NA_HEADS, n_var, NA_R, NA_KR, GRID_W, GRID_W)
    return t.transpose(1, 0, 2, 4, 3, 5).reshape(n_var, NA_HEADS, NA_R * GRID_W, NA_KR * GRID_W)


def _na_kernel(var_ref, ws_ref, q_ref, k_ref, v_ref, bias_ref, o_ref):
    del var_ref
    n_q = NA_R * GRID_W
    n_k = NA_KR * GRID_W
    k0 = pl.multiple_of(ws_ref[pl.program_id(1)] * GRID_W, GRID_W)
    low_half = lax.broadcasted_iota(jnp.int32, (n_q, LANES), 1) < NA_HEAD_DIM
    for hp in range(NA_HEADS // 2):
        cols = slice(hp * LANES, (hp + 1) * LANES)
        q2 = q_ref[:, cols]
        k2 = k_ref[pl.ds(k0, n_k), cols]
        v2 = jnp.concatenate([v_ref[pl.ds(k0, n_k), cols], jnp.ones((n_k, LANES), BF16)], axis=1)
        halves = []
        for half in range(2):
            keep = low_half if half == 0 else jnp.logical_not(low_half)
            qm = jnp.where(keep, q2, jnp.zeros_like(q2))
            s = lax.dot_general(qm, k2, (((1,), (1,)), ((), ())), preferred_element_type=F32)
            s = s + bias_ref[2 * hp + half]
            p = jnp.exp2(s - jnp.max(s, axis=-1, keepdims=True))
            o = jnp.dot(p.astype(BF16), v2, preferred_element_type=F32)
            halves.append(o[:, :LANES] / o[:, LANES:])
        o_ref[:, cols] = jnp.where(low_half, halves[0], halves[1]).astype(BF16)


def _na(q, k, v, rpb, tile_cache):
    bsz, seq, _ = q.shape
    rows = seq // GRID_W
    row_off, var_idx, win_start = _na_plan(rows)
    key = row_off.tobytes()
    if key not in tile_cache:
        tile_cache[key] = _na_bias_tiles(rpb, row_off)
    bias = tile_cache[key]
    n_q = NA_R * GRID_W
    n_k = NA_KR * GRID_W
    grid_spec = pltpu.PrefetchScalarGridSpec(
        num_scalar_prefetch=2,
        grid=(bsz, rows // NA_R),
        in_specs=[
            pl.BlockSpec((None, n_q, NA_WIDTH), lambda b, g, var, ws: (b, g, 0)),
            pl.BlockSpec((None, seq, NA_WIDTH), lambda b, g, var, ws: (b, 0, 0)),
            pl.BlockSpec((None, seq, NA_WIDTH), lambda b, g, var, ws: (b, 0, 0)),
            pl.BlockSpec((None, NA_HEADS, n_q, n_k), lambda b, g, var, ws: (var[g], 0, 0, 0)),
        ],
        out_specs=pl.BlockSpec((None, n_q, NA_WIDTH), lambda b, g, var, ws: (b, g, 0)),
    )
    return pl.pallas_call(
        _na_kernel,
        grid_spec=grid_spec,
        out_shape=jax.ShapeDtypeStruct((bsz, seq, NA_WIDTH), BF16),
        compiler_params=pltpu.CompilerParams(
            dimension_semantics=("arbitrary", "arbitrary"), vmem_limit_bytes=VMEM_LIMIT),
        name="na",
    )(jnp.asarray(var_idx), jnp.asarray(win_start), q, k, v, bias)


def _glu(u):
    a = u[:, :CONV_CH].astype(F32)
    g = u[:, CONV_CH:].astype(F32)
    return a * jax.nn.sigmoid(g)


def _conv_kernel(u_ref, ul_ref, ur_ref, w_ref, b_ref, lg_ref, lb_ref, o_ref, xs_ref):
    t = pl.program_id(1)
    ts = CONV_TS
    xs_ref[0:CONV_HALO, :] = jnp.where(t > 0, _glu(ul_ref[...]), 0.0)
    xs_ref[CONV_HALO:CONV_HALO + ts, :] = _glu(u_ref[...])
    xs_ref[CONV_HALO + ts:, :] = jnp.where(t < pl.num_programs(1) - 1, _glu(ur_ref[...]), 0.0)
    first = CONV_HALO - CONV_WIDTH // 2
    ext = CONV_RC + SUBLANES

    def chunk(rc, carry):
        r0 = pl.multiple_of(rc * CONV_RC, CONV_RC)
        groups = []
        for c0 in range(0, CONV_CH, LANES):
            slab = xs_ref[pl.ds(r0, CONV_RC + 2 * CONV_HALO), c0:c0 + LANES]
            acc = None
            for s in range(SUBLANES):
                part = None
                for j in range(s, CONV_WIDTH, SUBLANES):
                    term = w_ref[j:j + 1, c0:c0 + LANES] * slab[j - s:j - s + ext]
                    part = term if part is None else part + term
                shift = first + s
                part = part[shift:shift + CONV_RC]
                acc = part if acc is None else acc + part
            groups.append(acc)
        acc = jnp.concatenate(groups, axis=1) + b_ref[...]
        mu = jnp.mean(acc, axis=-1, keepdims=True)
        xc = acc - mu
        y = xc * lax.rsqrt(jnp.mean(xc * xc, axis=-1, keepdims=True) + EPS)
        y = y * lg_ref[...] + lb_ref[...]
        o_ref[pl.ds(r0, CONV_RC), :] = (y * jax.nn.sigmoid(y)).astype(BF16)
        return carry

    lax.fori_loop(0, ts // CONV_RC, chunk, 0)


def _conv(u, dw_w, dw_b, ln_g, ln_b, layer):
    bsz, seq, _ = u.shape
    ts = CONV_TS
    assert seq % ts == 0 and ts % CONV_HALO == 0 and ts % CONV_RC == 0
    assert CONV_HALO >= CONV_WIDTH // 2 and 2 * CONV_HALO >= SUBLANES + (CONV_WIDTH - 1) // SUBLANES * SUBLANES
    assert CONV_HALO - CONV_WIDTH // 2 + SUBLANES - 1 + CONV_RC <= CONV_RC + SUBLANES
    n_t = seq // ts
    per_tile = ts // CONV_HALO
    n_halo = seq // CONV_HALO
    vec = lambda: _const_spec((None, 1, CONV_CH), lambda b, t: (layer, 0, 0))
    return pl.pallas_call(
        _conv_kernel,
        grid=(bsz, n_t),
        in_specs=[
            pl.BlockSpec((None, ts, 2 * CONV_CH), lambda b, t: (b, t, 0)),
            pl.BlockSpec((None, CONV_HALO, 2 * CONV_CH),
                         lambda b, t: (b, jnp.maximum(t * per_tile - 1, 0), 0)),
            pl.BlockSpec((None, CONV_HALO, 2 * CONV_CH),
                         lambda b, t: (b, jnp.minimum((t + 1) * per_tile, n_halo - 1), 0)),
            _const_spec((None, CONV_WIDTH, CONV_CH), lambda b, t: (layer, 0, 0)),
            vec(), vec(), vec(),
        ],
        out_specs=pl.BlockSpec((None, ts, CONV_CH), lambda b, t: (b, t, 0)),
        out_shape=jax.ShapeDtypeStruct((bsz, seq, CONV_CH), BF16),
        scratch_shapes=[pltpu.VMEM((ts + 2 * CONV_HALO, CONV_CH), F32)],
        compiler_params=pltpu.CompilerParams(
            dimension_semantics=("arbitrary", "arbitrary"), vmem_limit_bytes=VMEM_LIMIT),
        name="conv",
    )(u, u, u, dw_w, dw_b, ln_g, ln_b)


def _t5_bucket(rel):
    nb = T5_BUCKETS // 2
    max_exact = nb // 2
    ret = jnp.where(rel > 0, nb, 0)
    n = jnp.abs(rel)
    nf = jnp.maximum(n, 1).astype(jnp.float32)
    large = max_exact + (jnp.log(nf / max_exact) / math.log(T5_MAX_DIST / max_exact)
                         * (nb - max_exact)).astype(jnp.int32)
    large = jnp.minimum(large, nb - 1)
    return ret + jnp.where(n < max_exact, n, large)


def _toeplitz(g, n):
    lead = g.shape[:-1]
    flat = jnp.tile(g, (1,) * len(lead) + (n,))
    return flat[..., :n * (2 * n - 1)].reshape(lead + (n, 2 * n - 1))[..., :n]


def _diff_bias(t5_bias):
    t = DIFF_T
    assert t >= T5_MAX_DIST
    m = np.arange(2 * t)
    q_minus_k = np.where(m < t, m, m - 2 * t)
    rel = np.stack([d * t - q_minus_k for d in (-1, 0, 1)]).astype(np.int32)
    g = t5_bias[_t5_bucket(jnp.asarray(rel))]
    near = _toeplitz(g.transpose(2, 0, 1), t)
    far = t5_bias[_t5_bucket(jnp.asarray([-T5_MAX_DIST, T5_MAX_DIST], jnp.int32))]
    far = jnp.broadcast_to(far.T[:, :, None, None], (DIFF_HEADS, 2, t, t))
    return jnp.concatenate([far[:, :1], near, far[:, 1:]], axis=1) * LOG2E


def _diff_kernel(lam_ref, qt_ref, k_ref, vt_ref, bias_ref, g_ref, o_ref,
                 s_ref, smax_ref, p_ref, alpha_ref, m_ref, acc_ref, vt1_ref, *, lam_init, n_k, two_stage):
    t = DIFF_T
    n_pairs = n_k * n_k

    strips = [slice(c0, c0 + DIFF_STRIP) for c0 in range(0, 2 * t, DIFF_STRIP)]

    def scores(e, par, cols):
        qi, ki = e // n_k, e % n_k
        kk = k_ref[pl.ds(pl.multiple_of(ki * t, t), t), :]
        s = jnp.dot(kk, qt_ref[qi, :, cols], preferred_element_type=F32)
        q0 = cols.start % t
        s = s + bias_ref[jnp.clip(ki - qi, -2, 2) + 2, :, q0:q0 + DIFF_STRIP]
        s_ref[par, :, cols] = s
        smax_ref[par, :, cols] = jnp.max(s, axis=0, keepdims=True)

    def softmax(e, par, cols):
        m_old = jnp.where(e % n_k == 0, NEG, m_ref[:, cols])
        m_new = jnp.maximum(m_old, smax_ref[par, :, cols])
        m_ref[:, cols] = m_new
        alpha_ref[par, :, cols] = jnp.exp2(m_old - m_new)
        p_ref[par, :, cols] = jnp.exp2(s_ref[par, :, cols] - m_new).astype(BF16)

    def accumulate(e, par, cols):
        pv = jnp.dot(vt1_ref[e % n_k], p_ref[par, :, cols], preferred_element_type=F32)
        qi = e // n_k
        acc_ref[qi, :, cols] = alpha_ref[par, :, cols] * acc_ref[qi, :, cols] + pv

    def finalize(qi, carry):
        inv_l = 1.0 / acc_ref[qi, LANES:LANES + 1, :]
        ot = (acc_ref[qi, :LANES, :t] * inv_l[:, :t]
              - lam_ref[0] * (acc_ref[qi, :LANES, t:] * inv_l[:, t:]))
        o = _rms(ot.T, g_ref[...]) * (1.0 - lam_init)
        o_ref[pl.ds(pl.multiple_of(qi * t, t), t), :] = o.astype(BF16)
        return carry

    vt1_ref[:, :LANES, :] = vt_ref[...]
    vt1_ref[:, LANES:, :] = jnp.ones((n_k, DIFF_ONES, t), BF16)
    acc_ref[...] = jnp.zeros(acc_ref.shape, F32)
    alpha_ref[1] = jnp.zeros(alpha_ref.shape[1:], F32)
    p_ref[1] = jnp.zeros(p_ref.shape[1:], BF16)
    for cols in strips:
        scores(0, 0, cols)

    def softmax_accumulate(e, par, cols):
        qi = e // n_k
        m_old = jnp.where(e % n_k == 0, NEG, m_ref[:, cols])
        m_new = jnp.maximum(m_old, smax_ref[par, :, cols])
        m_ref[:, cols] = m_new
        p = jnp.exp2(s_ref[par, :, cols] - m_new).astype(BF16)
        pv = jnp.dot(vt1_ref[e % n_k], p, preferred_element_type=F32)
        acc_ref[qi, :, cols] = jnp.exp2(m_old - m_new) * acc_ref[qi, :, cols] + pv

    def step(e, par):
        if two_stage:
            for cols in strips:
                softmax_accumulate(e, par, cols)
        else:
            for cols in strips:
                accumulate(jnp.maximum(e - 1, 0), 1 - par, cols)
            for cols in strips:
                softmax(e, par, cols)
        for cols in strips:
            scores(jnp.minimum(e + 1, n_pairs - 1), 1 - par, cols)

    def steps(j, carry):
        for u in range(DIFF_UNROLL):
            step(DIFF_UNROLL * j + u, u % 2)
        return carry

    lax.fori_loop(0, n_pairs // DIFF_UNROLL, steps, 0)
    if not two_stage:
        for cols in strips:
            accumulate(n_pairs - 1, 1, cols)
    lax.fori_loop(0, n_k, finalize, 0)


def _diff(qt2, k, vt, bias, lam_full, subln_g, lam_init, layer):
    bsz, seq, _ = k.shape
    t = DIFF_T
    assert seq % t == 0 and IN_TM == t
    n_k = seq // t
    assert DIFF_UNROLL % 2 == 0 and (n_k * n_k) % DIFF_UNROLL == 0
    smem = pl.BlockSpec(memory_space=pltpu.SMEM)
    return pl.pallas_call(
        functools.partial(_diff_kernel, lam_init=lam_init, n_k=n_k, two_stage=layer % 2 == 1),
        grid=(DIFF_HEADS, bsz),
        in_specs=[
            smem,
            pl.BlockSpec((n_k, LANES, 2 * t), lambda h, b: (b, h, 0)),
            pl.BlockSpec((None, seq, LANES), lambda h, b: (b, 0, h)),
            pl.BlockSpec((n_k, LANES, t), lambda h, b: (b, h, 0)),
            pl.BlockSpec((None, 5, t, t), lambda h, b: (h, 0, 0, 0)),
            _const_spec((None, 1, LANES), lambda h, b: (layer, 0, 0)),
        ],
        out_specs=pl.BlockSpec((None, seq, LANES), lambda h, b: (b, 0, h)),
        out_shape=jax.ShapeDtypeStruct((bsz, seq, DIFF_WIDTH), BF16),
        scratch_shapes=[
            pltpu.VMEM((2, t, 2 * t), F32),
            pltpu.VMEM((2, 1, 2 * t), F32),
            pltpu.VMEM((2, t, 2 * t), BF16),
            pltpu.VMEM((2, 1, 2 * t), F32),
            pltpu.VMEM((1, 2 * t), F32),
            pltpu.VMEM((n_k, LANES + DIFF_ONES, 2 * t), F32),
            pltpu.VMEM((n_k, LANES + DIFF_ONES, t), BF16),
        ],
        compiler_params=pltpu.CompilerParams(
            dimension_semantics=("arbitrary", "arbitrary"), vmem_limit_bytes=VMEM_LIMIT),
        name="diff",
    )(lam_full, qt2, k, vt, bias, subln_g)


_FFN_CHUNKS = ((0, 768), (768, 1536), (1536, 2304), (2304, FFN_HIDDEN))


def _post_kernel(x_ref, ona_ref, ocv_ref, odf_ref, gate_ref, wb_ref, wo_ref,
                 g_mix_ref, g_pre_ref, g_post_ref, wfi_ref, wfo_ref, o_ref, act_ref):
    merged = None
    for b, br_ref in enumerate((ona_ref, ocv_ref, odf_ref)):
        proj = jnp.dot(br_ref[...], wb_ref[b], preferred_element_type=F32)
        term = gate_ref[:, b * D_MODEL:(b + 1) * D_MODEL].astype(F32) * proj
        merged = term if merged is None else merged + term
    y = jnp.dot(merged.astype(BF16), wo_ref[...], preferred_element_type=F32)
    x1 = x_ref[...] + _rms(y, g_mix_ref[...])
    hf = _rms(x1, g_pre_ref[...]).astype(BF16)
    for c0, c1 in _FFN_CHUNKS:
        gate = jnp.dot(hf, wfi_ref[:, c0:c1], preferred_element_type=F32)
        up = jnp.dot(hf, wfi_ref[:, FFN_HIDDEN + c0:FFN_HIDDEN + c1], preferred_element_type=F32)
        act_ref[:, c0:c1] = (gate * jax.nn.sigmoid(gate) * up).astype(BF16)
    z = jnp.dot(act_ref[...], wfo_ref[...], preferred_element_type=F32)
    o_ref[...] = x1 + _rms(z, g_post_ref[...])


def _post(x, o_na, o_cv, o_df, gates, w_branch, w_out, g_mix, g_pre, g_post, w_ffn_in, w_ffn_out,
          layer):
    n_tok = x.shape[0]
    tm = POST_TM
    assert n_tok % tm == 0
    vec = lambda: _const_spec((None, 1, D_MODEL), lambda i: (layer, 0, 0))
    br = lambda: pl.BlockSpec((tm, BRANCH_W), lambda i: (i, 0))
    return pl.pallas_call(
        _post_kernel,
        grid=(n_tok // tm,),
        in_specs=[
            pl.BlockSpec((tm, D_MODEL), lambda i: (i, 0)),
            br(), br(), br(),
            pl.BlockSpec((tm, N_BRANCH * D_MODEL), lambda i: (i, 0)),
            _const_spec((None, N_BRANCH, BRANCH_W, D_MODEL), lambda i: (layer, 0, 0, 0)),
            _const_spec((None, D_MODEL, D_MODEL), lambda i: (layer, 0, 0)),
            vec(), vec(), vec(),
            _const_spec((None, D_MODEL, 2 * FFN_HIDDEN), lambda i: (layer, 0, 0)),
            _const_spec((None, FFN_HIDDEN, D_MODEL), lambda i: (layer, 0, 0)),
        ],
        out_specs=pl.BlockSpec((tm, D_MODEL), lambda i: (i, 0)),
        out_shape=jax.ShapeDtypeStruct((n_tok, D_MODEL), F32),
        scratch_shapes=[pltpu.VMEM((tm, FFN_HIDDEN), BF16)],
        compiler_params=pltpu.CompilerParams(
            dimension_semantics=("arbitrary",), vmem_limit_bytes=VMEM_LIMIT),
        name="post",
    )(x, o_na, o_cv, o_df, gates, w_branch, w_out, g_mix, g_pre, g_post, w_ffn_in, w_ffn_out)


def _trunk(x, p):
    bsz, seq, _ = x.shape
    n_tok = bsz * seq
    depth = p["w_in"].shape[0]
    x = x.reshape(n_tok, D_MODEL)
    for l in range(depth):
        lam_init = 0.8 - 0.6 * math.exp(-0.3 * l)
        naq, nak, nav, cvu, dqt, dk, dvt, gates = _in_proj(x, p["ln_mix_pre"], p["w_in"], p["b_gate"], l)
        seq3 = lambda a: a.reshape(bsz, seq, a.shape[-1])
        o_na = _na(seq3(naq), seq3(nak), seq3(nav), p["na_rpb"][l], p["na_tiles"][l])
        o_cv = _conv(seq3(cvu), p["conv_dw_w"], p["conv_dw_b"], p["conv_ln_g"], p["conv_ln_b"], l)
        o_df = _diff(dqt, seq3(dk), dvt, p["diff_bias"], p["lam_full"][l], p["diff_subln_g"],
                     lam_init, l)
        flat = lambda a: a.reshape(n_tok, a.shape[-1])
        x = _post(x, flat(o_na), flat(o_cv), flat(o_df), gates, p["w_branch"], p["w_out"],
                  p["ln_mix_post"], p["ln_ffn_pre"], p["ln_ffn_post"], p["w_ffn_in"], p["w_ffn_out"], l)
    return x.reshape(bsz, seq, D_MODEL)


def _prepare(w_in, b_gate, na_rpb, conv_dw_w, conv_dw_b, conv_ln_g, conv_ln_b,
             diff_lambda, diff_subln_g, t5_bias, w_branch, w_out,
             ln_mix_pre, ln_mix_post, ln_ffn_pre, ln_ffn_post, w_ffn_in, w_ffn_out):
    depth = w_in.shape[0]
    row = lambda a: a.reshape(depth, 1, a.shape[-1])
    lf = diff_lambda.astype(F32)
    lam_init = jnp.asarray([0.8 - 0.6 * math.exp(-0.3 * l) for l in range(depth)], F32)
    lam_full = (jnp.exp(jnp.sum(lf[:, 0] * lf[:, 1], axis=-1))
                - jnp.exp(jnp.sum(lf[:, 2] * lf[:, 3], axis=-1)) + lam_init)
    return dict(
        w_in=w_in.astype(BF16), b_gate=row(b_gate), na_rpb=na_rpb, na_tiles=[{} for _ in range(depth)],
        conv_dw_w=conv_dw_w, conv_dw_b=row(conv_dw_b), conv_ln_g=row(conv_ln_g), conv_ln_b=row(conv_ln_b),
        lam_full=lam_full.reshape(depth, 1), diff_subln_g=row(diff_subln_g),
        diff_bias=_diff_bias(t5_bias),
        w_branch=w_branch.astype(BF16), w_out=w_out.astype(BF16),
        ln_mix_pre=row(ln_mix_pre), ln_mix_post=row(ln_mix_post),
        ln_ffn_pre=row(ln_ffn_pre), ln_ffn_post=row(ln_ffn_post),
        w_ffn_in=w_ffn_in.astype(BF16), w_ffn_out=w_ffn_out.astype(BF16),
    )


def kernel(x_prompt, x_sample, w_in, b_gate, na_rpb, conv_dw_w, conv_dw_b, conv_ln_g, conv_ln_b,
           diff_lambda, diff_subln_g, t5_bias, w_branch, w_out,
           ln_mix_pre, ln_mix_post, ln_ffn_pre, ln_ffn_post, w_ffn_in, w_ffn_out):
    p = _prepare(w_in, b_gate, na_rpb, conv_dw_w, conv_dw_b, conv_ln_g, conv_ln_b,
                 diff_lambda, diff_subln_g, t5_bias, w_branch, w_out,
                 ln_mix_pre, ln_mix_post, ln_ffn_pre, ln_ffn_post, w_ffn_in, w_ffn_out)
    return (_trunk(x_prompt, p), _trunk(x_sample, p))
```

```python
import functools
import math

import numpy as np
import jax
import jax.numpy as jnp
from jax import lax
from jax.experimental import pallas as pl
from jax.experimental.pallas import tpu as pltpu

F32 = jnp.float32
BF16 = jnp.bfloat16

D_MODEL = 1024
GRID_W = 64
NA_HEADS = 8
NA_HEAD_DIM = 64
NA_WIDTH = NA_HEADS * NA_HEAD_DIM
NA_WIN_ROWS_MAX = 8
NA_WIN_COLS = 16
CONV_CH = 512
CONV_WIDTH = 31
DIFF_HEADS = 4
DIFF_HEAD_DIM = 64
DIFF_WIDTH = DIFF_HEADS * 2 * DIFF_HEAD_DIM
N_BRANCH = 3
BRANCH_W = 512
T5_BUCKETS = 32
T5_MAX_DIST = 128
FFN_HIDDEN = 2816
EPS = 1e-6

OFF_CONV = 3 * NA_WIDTH
OFF_DIFF = OFF_CONV + 2 * CONV_CH
OFF_GATE = OFF_DIFF + 3 * DIFF_WIDTH
IN_COLS = OFF_GATE + N_BRANCH * D_MODEL

LANES = 128
SUBLANES = 8
VMEM_LIMIT = 56 * 1024 * 1024

IN_TM = 512
POST_TM = 512
NA_R = 4
NA_KR = 12
CONV_TS = 512
CONV_HALO = 16
CONV_RC = 128
DIFF_T = 512
DIFF_STRIP = 256
DIFF_ONES = 16
DIFF_UNROLL = 2
NEG = -1e30
LOG2E = math.log2(math.e)

_EXACT = lax.Precision.HIGHEST


def _rms(x, g):
    return x * lax.rsqrt(jnp.mean(x * x, axis=-1, keepdims=True) + EPS) * g


def _const_spec(shape, index_map):
    return pl.BlockSpec(shape, index_map, pipeline_mode=pl.Buffered(1))


def _in_proj_kernel(x_ref, g_ref, w_ref, bg_ref,
                    naq_ref, nak_ref, nav_ref, cv_ref, dqt_ref, dk_ref, dvt_ref, gate_ref):
    h = _rms(x_ref[...], g_ref[...]).astype(BF16)

    def proj(c0, c1):
        return jnp.dot(h, w_ref[:, c0:c1], preferred_element_type=F32)

    naq_ref[...] = (proj(0, NA_WIDTH) * (NA_HEAD_DIM ** -0.5 * LOG2E)).astype(BF16)
    nak_ref[...] = proj(NA_WIDTH, 2 * NA_WIDTH).astype(BF16)
    nav_ref[...] = proj(2 * NA_WIDTH, 3 * NA_WIDTH).astype(BF16)
    cv_ref[...] = proj(OFF_CONV, OFF_DIFF).astype(BF16)
    dqt = (proj(OFF_DIFF, OFF_DIFF + DIFF_WIDTH) * (DIFF_HEAD_DIM ** -0.5 * LOG2E)).T
    comp0 = lax.broadcasted_iota(jnp.int32, dqt.shape, 0) % (2 * DIFF_HEAD_DIM) < DIFF_HEAD_DIM
    dqt_ref[:, :IN_TM] = jnp.where(comp0, dqt, 0.0).astype(BF16)
    dqt_ref[:, IN_TM:] = jnp.where(comp0, 0.0, dqt).astype(BF16)
    dk_ref[...] = proj(OFF_DIFF + DIFF_WIDTH, OFF_DIFF + 2 * DIFF_WIDTH).astype(BF16)
    dvt_ref[...] = proj(OFF_DIFF + 2 * DIFF_WIDTH, OFF_GATE).T.astype(BF16)
    for b in range(N_BRANCH):
        c0 = OFF_GATE + b * D_MODEL
        gate = jax.nn.sigmoid(proj(c0, c0 + D_MODEL) + bg_ref[:, b * D_MODEL:(b + 1) * D_MODEL])
        gate_ref[:, b * D_MODEL:(b + 1) * D_MODEL] = gate.astype(BF16)


def _in_proj(x, g_pre, w_in, b_gate, layer):
    n_tok = x.shape[0]
    tm = IN_TM
    assert n_tok % tm == 0
    n_t = n_tok // tm
    tok = lambda w: (pl.BlockSpec((tm, w), lambda i: (i, 0)), jax.ShapeDtypeStruct((n_tok, w), BF16))
    tok_t = lambda w, c: (pl.BlockSpec((None, w, c), lambda i: (i, 0, 0)),
                          jax.ShapeDtypeStruct((n_t, w, c), BF16))
    outs = [tok(NA_WIDTH), tok(NA_WIDTH), tok(NA_WIDTH), tok(2 * CONV_CH),
            tok_t(DIFF_WIDTH, 2 * tm), tok(DIFF_WIDTH), tok_t(DIFF_WIDTH, tm), tok(N_BRANCH * D_MODEL)]
    return pl.pallas_call(
        _in_proj_kernel,
        grid=(n_t,),
        in_specs=[
            pl.BlockSpec((tm, D_MODEL), lambda i: (i, 0)),
            _const_spec((None, 1, D_MODEL), lambda i: (layer, 0, 0)),
            _const_spec((None, D_MODEL, IN_COLS), lambda i: (layer, 0, 0)),
            _const_spec((None, 1, N_BRANCH * D_MODEL), lambda i: (layer, 0, 0)),
        ],
        out_specs=[o[0] for o in outs],
        out_shape=[o[1] for o in outs],
        compiler_params=pltpu.CompilerParams(
            dimension_semantics=("arbitrary",), vmem_limit_bytes=VMEM_LIMIT),
        name="in_proj",
    )(x, g_pre, w_in, b_gate)


def _na_plan(rows):
    assert rows % NA_R == 0 and rows >= NA_KR
    kr = min(NA_WIN_ROWS_MAX, rows)
    n_groups = rows // NA_R
    variants, var_idx, win_start = [], [], []
    for g in range(n_groups):
        r0 = g * NA_R
        ws = int(np.clip(r0 - kr // 2, 0, rows - NA_KR))
        ro = -np.ones((NA_R, NA_KR), np.int32)
        for i in range(NA_R):
            r = r0 + i
            rs = int(np.clip(r - kr // 2, 0, rows - kr))
            assert ws <= rs and rs + kr <= ws + NA_KR
            for a in range(NA_KR):
                if rs <= ws + a < rs + kr:
                    ro[i, a] = ws + a - r + (NA_WIN_ROWS_MAX - 1)
        for v, known in enumerate(variants):
            if np.array_equal(known, ro):
                var_idx.append(v)
                break
        else:
            var_idx.append(len(variants))
            variants.append(ro)
        win_start.append(ws)
    return np.stack(variants), np.asarray(var_idx, np.int32), np.asarray(win_start, np.int32)


def _na_bias_tiles(rpb, row_off):
    n_var = row_off.shape[0]
    n_rows = 2 * NA_WIN_ROWS_MAX - 1
    n_cols = 2 * NA_WIN_COLS - 1
    col = np.arange(GRID_W)
    col_start = np.clip(col - NA_WIN_COLS // 2, 0, GRID_W - NA_WIN_COLS)
    kc = col[None, :]
    col_ok = (kc >= col_start[:, None]) & (kc < col_start[:, None] + NA_WIN_COLS)
    col_off = kc - col[:, None] + (NA_WIN_COLS - 1)
    col_hot = (col_ok[None] & (col_off[None] == np.arange(n_cols)[:, None, None])).astype(np.float32)
    row_hot = (row_off.reshape(-1)[:, None] == np.arange(n_rows)[None, :]).astype(np.float32)
    t = jnp.einsum("xr,hrc->hxc", row_hot, rpb, precision=_EXACT)
    t = jnp.einsum("hxc,cqk->hxqk", t, col_hot, precision=_EXACT)
    ok = (row_off >= 0).reshape(-1)[:, None, None] & col_ok[None]
    t = jnp.where(ok[None], t * LOG2E, NEG).reshape(NA_HEADS, n_var, NA_R, NA_KR, GRID_W, GRID_W)
    return t.transpose(1, 0, 2, 4, 3, 5).reshape(n_var, NA_HEADS, NA_R * GRID_W, NA_KR * GRID_W)


def _na_kernel(var_ref, ws_ref, q_ref, k_ref, v_ref, bias_ref, o_ref):
    del var_ref
    n_q = NA_R * GRID_W
    n_k = NA_KR * GRID_W
    k0 = pl.multiple_of(ws_ref[pl.program_id(1)] * GRID_W, GRID_W)
    low_half = lax.broadcasted_iota(jnp.int32, (n_q, LANES), 1) < NA_HEAD_DIM
    for hp in range(NA_HEADS // 2):
        cols = slice(hp * LANES, (hp + 1) * LANES)
        q2 = q_ref[:, cols]
        k2 = k_ref[pl.ds(k0, n_k), cols]
        v2 = jnp.concatenate([v_ref[pl.ds(k0, n_k), cols], jnp.ones((n_k, LANES), BF16)], axis=1)
        halves = []
        for half in range(2):
            keep = low_half if half == 0 else jnp.logical_not(low_half)
            qm = jnp.where(keep, q2, jnp.zeros_like(q2))
            s = lax.dot_general(qm, k2, (((1,), (1,)), ((), ())), preferred_element_type=F32)
            s = s + bias_ref[2 * hp + half]
            p = jnp.exp2(s - jnp.max(s, axis=-1, keepdims=True))
            o = jnp.dot(p.astype(BF16), v2, preferred_element_type=F32)
            halves.append(o[:, :LANES] / o[:, LANES:])
        o_ref[:, cols] = jnp.where(low_half, halves[0], halves[1]).astype(BF16)


def _na(q, k, v, rpb, tile_cache):
    bsz, seq, _ = q.shape
    rows = seq // GRID_W
    row_off, var_idx, win_start = _na_plan(rows)
    key = row_off.tobytes()
    if key not in tile_cache:
        tile_cache[key] = _na_bias_tiles(rpb, row_off)
    bias = tile_cache[key]
    n_q = NA_R * GRID_W
    n_k = NA_KR * GRID_W
    grid_spec = pltpu.PrefetchScalarGridSpec(
        num_scalar_prefetch=2,
        grid=(bsz, rows // NA_R),
        in_specs=[
            pl.BlockSpec((None, n_q, NA_WIDTH), lambda b, g, var, ws: (b, g, 0)),
            pl.BlockSpec((None, seq, NA_WIDTH), lambda b, g, var, ws: (b, 0, 0)),
            pl.BlockSpec((None, seq, NA_WIDTH), lambda b, g, var, ws: (b, 0, 0)),
            pl.BlockSpec((None, NA_HEADS, n_q, n_k), lambda b, g, var, ws: (var[g], 0, 0, 0)),
        ],
        out_specs=pl.BlockSpec((None, n_q, NA_WIDTH), lambda b, g, var, ws: (b, g, 0)),
    )
    return pl.pallas_call(
        _na_kernel,
        grid_spec=grid_spec,
        out_shape=jax.ShapeDtypeStruct((bsz, seq, NA_WIDTH), BF16),
        compiler_params=pltpu.CompilerParams(
            dimension_semantics=("arbitrary", "arbitrary"), vmem_limit_bytes=VMEM_LIMIT),
        name="na",
    )(jnp.asarray(var_idx), jnp.asarray(win_start), q, k, v, bias)


def _glu(u):
    a = u[:, :CONV_CH].astype(F32)
    g = u[:, CONV_CH:].astype(F32)
    return a * jax.nn.sigmoid(g)


def _conv_kernel(u_ref, ul_ref, ur_ref, w_ref, b_ref, lg_ref, lb_ref, o_ref, xs_ref):
    t = pl.program_id(1)
    ts = CONV_TS
    xs_ref[0:CONV_HALO, :] = jnp.where(t > 0, _glu(ul_ref[...]), 0.0)
    xs_ref[CONV_HALO:CONV_HALO + ts, :] = _glu(u_ref[...])
    xs_ref[CONV_HALO + ts:, :] = jnp.where(t < pl.num_programs(1) - 1, _glu(ur_ref[...]), 0.0)
    first = CONV_HALO - CONV_WIDTH // 2
    ext = CONV_RC + SUBLANES

    def chunk(rc, carry):
        r0 = pl.multiple_of(rc * CONV_RC, CONV_RC)
        groups = []
        for c0 in range(0, CONV_CH, LANES):
            slab = xs_ref[pl.ds(r0, CONV_RC + 2 * CONV_HALO), c0:c0 + LANES]
            acc = None
            for s in range(SUBLANES):
                part = None
                for j in range(s, CONV_WIDTH, SUBLANES):
                    term = w_ref[j:j + 1, c0:c0 + LANES] * slab[j - s:j - s + ext]
                    part = term if part is None else part + term
                shift = first + s
                part = part[shift:shift + CONV_RC]
                acc = part if acc is None else acc + part
            groups.append(acc)
        acc = jnp.concatenate(groups, axis=1) + b_ref[...]
        mu = jnp.mean(acc, axis=-1, keepdims=True)
        xc = acc - mu
        y = xc * lax.rsqrt(jnp.mean(xc * xc, axis=-1, keepdims=True) + EPS)
        y = y * lg_ref[...] + lb_ref[...]
        o_ref[pl.ds(r0, CONV_RC), :] = (y * jax.nn.sigmoid(y)).astype(BF16)
        return carry

    lax.fori_loop(0, ts // CONV_RC, chunk, 0)


def _conv(u, dw_w, dw_b, ln_g, ln_b, layer):
    bsz, seq, _ = u.shape
    ts = CONV_TS
    assert seq % ts == 0 and ts % CONV_HALO == 0 and ts % CONV_RC == 0
    assert CONV_HALO >= CONV_WIDTH // 2 and 2 * CONV_HALO >= SUBLANES + (CONV_WIDTH - 1) // SUBLANES * SUBLANES
    assert CONV_HALO - CONV_WIDTH // 2 + SUBLANES - 1 + CONV_RC <= CONV_RC + SUBLANES
    n_t = seq // ts
    per_tile = ts // CONV_HALO
    n_halo = seq // CONV_HALO
    vec = lambda: _const_spec((None, 1, CONV_CH), lambda b, t: (layer, 0, 0))
    return pl.pallas_call(
        _conv_kernel,
        grid=(bsz, n_t),
        in_specs=[
            pl.BlockSpec((None, ts, 2 * CONV_CH), lambda b, t: (b, t, 0)),
            pl.BlockSpec((None, CONV_HALO, 2 * CONV_CH),
                         lambda b, t: (b, jnp.maximum(t * per_tile - 1, 0), 0)),
            pl.BlockSpec((None, CONV_HALO, 2 * CONV_CH),
                         lambda b, t: (b, jnp.minimum((t + 1) * per_tile, n_halo - 1), 0)),
            _const_spec((None, CONV_WIDTH, CONV_CH), lambda b, t: (layer, 0, 0)),
            vec(), vec(), vec(),
        ],
        out_specs=pl.BlockSpec((None, ts, CONV_CH), lambda b, t: (b, t, 0)),
        out_shape=jax.ShapeDtypeStruct((bsz, seq, CONV_CH), BF16),
        scratch_shapes=[pltpu.VMEM((ts + 2 * CONV_HALO, CONV_CH), F32)],
        compiler_params=pltpu.CompilerParams(
            dimension_semantics=("arbitrary", "arbitrary"), vmem_limit_bytes=VMEM_LIMIT),
        name="conv",
    )(u, u, u, dw_w, dw_b, ln_g, ln_b)


def _t5_bucket(rel):
    nb = T5_BUCKETS // 2
    max_exact = nb // 2
    ret = jnp.where(rel > 0, nb, 0)
    n = jnp.abs(rel)
    nf = jnp.maximum(n, 1).astype(jnp.float32)
    large = max_exact + (jnp.log(nf / max_exact) / math.log(T5_MAX_DIST / max_exact)
                         * (nb - max_exact)).astype(jnp.int32)
    large = jnp.minimum(large, nb - 1)
    return ret + jnp.where(n < max_exact, n, large)


def _toeplitz(g, n):
    lead = g.shape[:-1]
    flat = jnp.tile(g, (1,) * len(lead) + (n,))
    return flat[..., :n * (2 * n - 1)].reshape(lead + (n, 2 * n - 1))[..., :n]


def _diff_bias(t5_bias):
    t = DIFF_T
    assert t >= T5_MAX_DIST
    m = np.arange(2 * t)
    q_minus_k = np.where(m < t, m, m - 2 * t)
    rel = np.stack([d * t - q_minus_k for d in (-1, 0, 1)]).astype(np.int32)
    g = t5_bias[_t5_bucket(jnp.asarray(rel))]
    near = _toeplitz(g.transpose(2, 0, 1), t)
    far = t5_bias[_t5_bucket(jnp.asarray([-T5_MAX_DIST, T5_MAX_DIST], jnp.int32))]
    far = jnp.broadcast_to(far.T[:, :, None, None], (DIFF_HEADS, 2, t, t))
    return jnp.concatenate([far[:, :1], near, far[:, 1:]], axis=1) * LOG2E


def _diff_kernel(lam_ref, qt_ref, k_ref, vt_ref, bias_ref, g_ref, o_ref,
                 s_ref, smax_ref, p_ref, alpha_ref, m_ref, acc_ref, vt1_ref, *, lam_init, n_k, interleave):
    t = DIFF_T
    n_pairs = n_k * n_k

    strips = [slice(c0, c0 + DIFF_STRIP) for c0 in range(0, 2 * t, DIFF_STRIP)]

    def scores(e, par, cols):
        qi, ki = e // n_k, e % n_k
        kk = k_ref[pl.ds(pl.multiple_of(ki * t, t), t), :]
        s = jnp.dot(kk, qt_ref[qi, :, cols], preferred_element_type=F32)
        q0 = cols.start % t
        s = s + bias_ref[jnp.clip(ki - qi, -2, 2) + 2, :, q0:q0 + DIFF_STRIP]
        s_ref[par, :, cols] = s
        smax_ref[par, :, cols] = jnp.max(s, axis=0, keepdims=True)

    def softmax(e, par, cols):
        m_old = jnp.where(e % n_k == 0, NEG, m_ref[:, cols])
        m_new = jnp.maximum(m_old, smax_ref[par, :, cols])
        m_ref[:, cols] = m_new
        alpha_ref[par, :, cols] = jnp.exp2(m_old - m_new)
        p_ref[par, :, cols] = jnp.exp2(s_ref[par, :, cols] - m_new).astype(BF16)

    def accumulate(e, par, cols):
        pv = jnp.dot(vt1_ref[e % n_k], p_ref[par, :, cols], preferred_element_type=F32)
        qi = e // n_k
        acc_ref[qi, :, cols] = alpha_ref[par, :, cols] * acc_ref[qi, :, cols] + pv

    def finalize(qi, carry):
        inv_l = 1.0 / acc_ref[qi, LANES:LANES + 1, :]
        ot = (acc_ref[qi, :LANES, :t] * inv_l[:, :t]
              - lam_ref[0] * (acc_ref[qi, :LANES, t:] * inv_l[:, t:]))
        o = _rms(ot.T, g_ref[...]) * (1.0 - lam_init)
        o_ref[pl.ds(pl.multiple_of(qi * t, t), t), :] = o.astype(BF16)
        return carry

    vt1_ref[:, :LANES, :] = vt_ref[...]
    vt1_ref[:, LANES:, :] = jnp.ones((n_k, DIFF_ONES, t), BF16)
    acc_ref[...] = jnp.zeros(acc_ref.shape, F32)
    alpha_ref[1] = jnp.zeros(alpha_ref.shape[1:], F32)
    p_ref[1] = jnp.zeros(p_ref.shape[1:], BF16)
    for cols in strips:
        scores(0, 0, cols)

    def step(e, par):
        e_next = jnp.minimum(e + 1, n_pairs - 1)
        if interleave:
            for cols in strips:
                accumulate(jnp.maximum(e - 1, 0), 1 - par, cols)
                softmax(e, par, cols)
                scores(e_next, 1 - par, cols)
            return
        for cols in strips:
            accumulate(jnp.maximum(e - 1, 0), 1 - par, cols)
        for cols in strips:
            softmax(e, par, cols)
        for cols in strips:
            scores(e_next, 1 - par, cols)

    def steps(j, carry):
        for u in range(DIFF_UNROLL):
            step(DIFF_UNROLL * j + u, u % 2)
        return carry

    lax.fori_loop(0, n_pairs // DIFF_UNROLL, steps, 0)
    for cols in strips:
        accumulate(n_pairs - 1, 1, cols)
    lax.fori_loop(0, n_k, finalize, 0)


def _diff(qt2, k, vt, bias, lam_full, subln_g, lam_init, layer):
    bsz, seq, _ = k.shape
    t = DIFF_T
    assert seq % t == 0 and IN_TM == t
    n_k = seq // t
    assert DIFF_UNROLL % 2 == 0 and (n_k * n_k) % DIFF_UNROLL == 0
    smem = pl.BlockSpec(memory_space=pltpu.SMEM)
    return pl.pallas_call(
        functools.partial(_diff_kernel, lam_init=lam_init, n_k=n_k, interleave=layer % 2 == 1),
        grid=(DIFF_HEADS, bsz),
        in_specs=[
            smem,
            pl.BlockSpec((n_k, LANES, 2 * t), lambda h, b: (b, h, 0)),
            pl.BlockSpec((None, seq, LANES), lambda h, b: (b, 0, h)),
            pl.BlockSpec((n_k, LANES, t), lambda h, b: (b, h, 0)),
            pl.BlockSpec((None, 5, t, t), lambda h, b: (h, 0, 0, 0)),
            _const_spec((None, 1, LANES), lambda h, b: (layer, 0, 0)),
        ],
        out_specs=pl.BlockSpec((None, seq, LANES), lambda h, b: (b, 0, h)),
        out_shape=jax.ShapeDtypeStruct((bsz, seq, DIFF_WIDTH), BF16),
        scratch_shapes=[
            pltpu.VMEM((2, t, 2 * t), F32),
            pltpu.VMEM((2, 1, 2 * t), F32),
            pltpu.VMEM((2, t, 2 * t), BF16),
            pltpu.VMEM((2, 1, 2 * t), F32),
            pltpu.VMEM((1, 2 * t), F32),
            pltpu.VMEM((n_k, LANES + DIFF_ONES, 2 * t), F32),
            pltpu.VMEM((n_k, LANES + DIFF_ONES, t), BF16),
        ],
        compiler_params=pltpu.CompilerParams(
            dimension_semantics=("arbitrary", "arbitrary"), vmem_limit_bytes=VMEM_LIMIT),
        name="diff",
    )(lam_full, qt2, k, vt, bias, subln_g)


_FFN_CHUNKS = ((0, 768), (768, 1536), (1536, 2304), (2304, FFN_HIDDEN))


def _post_kernel(x_ref, ona_ref, ocv_ref, odf_ref, gate_ref, wb_ref, wo_ref,
                 g_mix_ref, g_pre_ref, g_post_ref, wfi_ref, wfo_ref, o_ref, act_ref):
    merged = None
    for b, br_ref in enumerate((ona_ref, ocv_ref, odf_ref)):
        proj = jnp.dot(br_ref[...], wb_ref[b], preferred_element_type=F32)
        term = gate_ref[:, b * D_MODEL:(b + 1) * D_MODEL].astype(F32) * proj
        merged = term if merged is None else merged + term
    y = jnp.dot(merged.astype(BF16), wo_ref[...], preferred_element_type=F32)
    x1 = x_ref[...] + _rms(y, g_mix_ref[...])
    hf = _rms(x1, g_pre_ref[...]).astype(BF16)
    for c0, c1 in _FFN_CHUNKS:
        gate = jnp.dot(hf, wfi_ref[:, c0:c1], preferred_element_type=F32)
        up = jnp.dot(hf, wfi_ref[:, FFN_HIDDEN + c0:FFN_HIDDEN + c1], preferred_element_type=F32)
        act_ref[:, c0:c1] = (gate * jax.nn.sigmoid(gate) * up).astype(BF16)
    z = jnp.dot(act_ref[...], wfo_ref[...], preferred_element_type=F32)
    o_ref[...] = x1 + _rms(z, g_post_ref[...])


def _post(x, o_na, o_cv, o_df, gates, w_branch, w_out, g_mix, g_pre, g_post, w_ffn_in, w_ffn_out,
          layer):
    n_tok = x.shape[0]
    tm = POST_TM
    assert n_tok % tm == 0
    vec = lambda: _const_spec((None, 1, D_MODEL), lambda i: (layer, 0, 0))
    br = lambda: pl.BlockSpec((tm, BRANCH_W), lambda i: (i, 0))
    return pl.pallas_call(
        _post_kernel,
        grid=(n_tok // tm,),
        in_specs=[
            pl.BlockSpec((tm, D_MODEL), lambda i: (i, 0)),
            br(), br(), br(),
            pl.BlockSpec((tm, N_BRANCH * D_MODEL), lambda i: (i, 0)),
            _const_spec((None, N_BRANCH, BRANCH_W, D_MODEL), lambda i: (layer, 0, 0, 0)),
            _const_spec((None, D_MODEL, D_MODEL), lambda i: (layer, 0, 0)),
            vec(), vec(), vec(),
            _const_spec((None, D_MODEL, 2 * FFN_HIDDEN), lambda i: (layer, 0, 0)),
            _const_spec((None, FFN_HIDDEN, D_MODEL), lambda i: (layer, 0, 0)),
        ],
        out_specs=pl.BlockSpec((tm, D_MODEL), lambda i: (i, 0)),
        out_shape=jax.ShapeDtypeStruct((n_tok, D_MODEL), F32),
        scratch_shapes=[pltpu.VMEM((tm, FFN_HIDDEN), BF16)],
        compiler_params=pltpu.CompilerParams(
            dimension_semantics=("arbitrary",), vmem_limit_bytes=VMEM_LIMIT),
        name="post",
    )(x, o_na, o_cv, o_df, gates, w_branch, w_out, g_mix, g_pre, g_post, w_ffn_in, w_ffn_out)


def _trunk(x, p):
    bsz, seq, _ = x.shape
    n_tok = bsz * seq
    depth = p["w_in"].shape[0]
    x = x.reshape(n_tok, D_MODEL)
    for l in range(depth):
        lam_init = 0.8 - 0.6 * math.exp(-0.3 * l)
        naq, nak, nav, cvu, dqt, dk, dvt, gates = _in_proj(x, p["ln_mix_pre"], p["w_in"], p["b_gate"], l)
        seq3 = lambda a: a.reshape(bsz, seq, a.shape[-1])
        o_na = _na(seq3(naq), seq3(nak), seq3(nav), p["na_rpb"][l], p["na_tiles"][l])
        o_cv = _conv(seq3(cvu), p["conv_dw_w"], p["conv_dw_b"], p["conv_ln_g"], p["conv_ln_b"], l)
        o_df = _diff(dqt, seq3(dk), dvt, p["diff_bias"], p["lam_full"][l], p["diff_subln_g"],
                     lam_init, l)
        flat = lambda a: a.reshape(n_tok, a.shape[-1])
        x = _post(x, flat(o_na), flat(o_cv), flat(o_df), gates, p["w_branch"], p["w_out"],
                  p["ln_mix_post"], p["ln_ffn_pre"], p["ln_ffn_post"], p["w_ffn_in"], p["w_ffn_out"], l)
    return x.reshape(bsz, seq, D_MODEL)


def _prepare(w_in, b_gate, na_rpb, conv_dw_w, conv_dw_b, conv_ln_g, conv_ln_b,
             diff_lambda, diff_subln_g, t5_bias, w_branch, w_out,
             ln_mix_pre, ln_mix_post, ln_ffn_pre, ln_ffn_post, w_ffn_in, w_ffn_out):
    depth = w_in.shape[0]
    row = lambda a: a.reshape(depth, 1, a.shape[-1])
    lf = diff_lambda.astype(F32)
    lam_init = jnp.asarray([0.8 - 0.6 * math.exp(-0.3 * l) for l in range(depth)], F32)
    lam_full = (jnp.exp(jnp.sum(lf[:, 0] * lf[:, 1], axis=-1))
                - jnp.exp(jnp.sum(lf[:, 2] * lf[:, 3], axis=-1)) + lam_init)
    return dict(
        w_in=w_in.astype(BF16), b_gate=row(b_gate), na_rpb=na_rpb, na_tiles=[{} for _ in range(depth)],
        conv_dw_w=conv_dw_w, conv_dw_b=row(conv_dw_b), conv_ln_g=row(conv_ln_g), conv_ln_b=row(conv_ln_b),
        lam_full=lam_full.reshape(depth, 1), diff_subln_g=row(diff_subln_g),
        diff_bias=_diff_bias(t5_bias),
        w_branch=w_branch.astype(BF16), w_out=w_out.astype(BF16),
        ln_mix_pre=row(ln_mix_pre), ln_mix_post=row(ln_mix_post),
        ln_ffn_pre=row(ln_ffn_pre), ln_ffn_post=row(ln_ffn_post),
        w_ffn_in=w_ffn_in.astype(BF16), w_ffn_out=w_ffn_out.astype(BF16),
    )


def kernel(x_prompt, x_sample, w_in, b_gate, na_rpb, conv_dw_w, conv_dw_b, conv_ln_g, conv_ln_b,
           diff_lambda, diff_subln_g, t5_bias, w_branch, w_out,
           ln_mix_pre, ln_mix_post, ln_ffn_pre, ln_ffn_post, w_ffn_in, w_ffn_out):
    p = _prepare(w_in, b_gate, na_rpb, conv_dw_w, conv_dw_b, conv_ln_g, conv_ln_b,
                 diff_lambda, diff_subln_g, t5_bias, w_branch, w_out,
                 ln_mix_pre, ln_mix_post, ln_ffn_pre, ln_ffn_post, w_ffn_in, w_ffn_out)
    return (_trunk(x_prompt, p), _trunk(x_sample, p))
```

```python
import functools
import math

import numpy as np
import jax
import jax.numpy as jnp
from jax import lax
from jax.experimental import pallas as pl
from jax.experimental.pallas import tpu as pltpu

F32 = jnp.float32
BF16 = jnp.bfloat16

D_MODEL = 1024
GRID_W = 64
NA_HEADS = 8
NA_HEAD_DIM = 64
NA_WIDTH = NA_HEADS * NA_HEAD_DIM
NA_WIN_ROWS_MAX = 8
NA_WIN_COLS = 16
CONV_CH = 512
CONV_WIDTH = 31
DIFF_HEADS = 4
DIFF_HEAD_DIM = 64
DIFF_WIDTH = DIFF_HEADS * 2 * DIFF_HEAD_DIM
N_BRANCH = 3
BRANCH_W = 512
T5_BUCKETS = 32
T5_MAX_DIST = 128
FFN_HIDDEN = 2816
EPS = 1e-6

OFF_CONV = 3 * NA_WIDTH
OFF_DIFF = OFF_CONV + 2 * CONV_CH
OFF_GATE = OFF_DIFF + 3 * DIFF_WIDTH
IN_COLS = OFF_GATE + N_BRANCH * D_MODEL

LANES = 128
SUBLANES = 8
VMEM_LIMIT = 56 * 1024 * 1024

IN_TM = 512
POST_TM = 512
NA_R = 4
NA_KR = 12
CONV_TS = 512
CONV_HALO = 16
CONV_RC = 128
DIFF_T = 512
DIFF_STRIP = 256
DIFF_ONES = 16
DIFF_UNROLL = 4
NEG = -1e30
LOG2E = math.log2(math.e)

_EXACT = lax.Precision.HIGHEST


def _rms(x, g):
    return x * lax.rsqrt(jnp.mean(x * x, axis=-1, keepdims=True) + EPS) * g


def _const_spec(shape, index_map):
    return pl.BlockSpec(shape, index_map, pipeline_mode=pl.Buffered(1))


def _in_proj_kernel(x_ref, g_ref, w_ref, bg_ref,
                    naq_ref, nak_ref, nav_ref, cv_ref, dqt_ref, dk_ref, dvt_ref, gate_ref):
    h = _rms(x_ref[...], g_ref[...]).astype(BF16)

    def proj(c0, c1):
        return jnp.dot(h, w_ref[:, c0:c1], preferred_element_type=F32)

    naq_ref[...] = (proj(0, NA_WIDTH) * (NA_HEAD_DIM ** -0.5 * LOG2E)).astype(BF16)
    nak_ref[...] = proj(NA_WIDTH, 2 * NA_WIDTH).astype(BF16)
    nav_ref[...] = proj(2 * NA_WIDTH, 3 * NA_WIDTH).astype(BF16)
    cv_ref[...] = proj(OFF_CONV, OFF_DIFF).astype(BF16)
    dqt = (proj(OFF_DIFF, OFF_DIFF + DIFF_WIDTH) * (DIFF_HEAD_DIM ** -0.5 * LOG2E)).T
    comp0 = lax.broadcasted_iota(jnp.int32, dqt.shape, 0) % (2 * DIFF_HEAD_DIM) < DIFF_HEAD_DIM
    dqt_ref[:, :IN_TM] = jnp.where(comp0, dqt, 0.0).astype(BF16)
    dqt_ref[:, IN_TM:] = jnp.where(comp0, 0.0, dqt).astype(BF16)
    dk_ref[...] = proj(OFF_DIFF + DIFF_WIDTH, OFF_DIFF + 2 * DIFF_WIDTH).astype(BF16)
    dvt_ref[...] = proj(OFF_DIFF + 2 * DIFF_WIDTH, OFF_GATE).T.astype(BF16)
    for b in range(N_BRANCH):
        c0 = OFF_GATE + b * D_MODEL
        gate = jax.nn.sigmoid(proj(c0, c0 + D_MODEL) + bg_ref[:, b * D_MODEL:(b + 1) * D_MODEL])
        gate_ref[:, b * D_MODEL:(b + 1) * D_MODEL] = gate.astype(BF16)


def _in_proj(x, g_pre, w_in, b_gate, layer):
    n_tok = x.shape[0]
    tm = IN_TM
    assert n_tok % tm == 0
    n_t = n_tok // tm
    tok = lambda w: (pl.BlockSpec((tm, w), lambda i: (i, 0)), jax.ShapeDtypeStruct((n_tok, w), BF16))
    tok_t = lambda w, c: (pl.BlockSpec((None, w, c), lambda i: (i, 0, 0)),
                          jax.ShapeDtypeStruct((n_t, w, c), BF16))
    outs = [tok(NA_WIDTH), tok(NA_WIDTH), tok(NA_WIDTH), tok(2 * CONV_CH),
            tok_t(DIFF_WIDTH, 2 * tm), tok(DIFF_WIDTH), tok_t(DIFF_WIDTH, tm), tok(N_BRANCH * D_MODEL)]
    return pl.pallas_call(
        _in_proj_kernel,
        grid=(n_t,),
        in_specs=[
            pl.BlockSpec((tm, D_MODEL), lambda i: (i, 0)),
            _const_spec((None, 1, D_MODEL), lambda i: (layer, 0, 0)),
            _const_spec((None, D_MODEL, IN_COLS), lambda i: (layer, 0, 0)),
            _const_spec((None, 1, N_BRANCH * D_MODEL), lambda i: (layer, 0, 0)),
        ],
        out_specs=[o[0] for o in outs],
        out_shape=[o[1] for o in outs],
        compiler_params=pltpu.CompilerParams(
            dimension_semantics=("arbitrary",), vmem_limit_bytes=VMEM_LIMIT),
        name="in_proj",
    )(x, g_pre, w_in, b_gate)


def _na_plan(rows):
    assert rows % NA_R == 0 and rows >= NA_KR
    kr = min(NA_WIN_ROWS_MAX, rows)
    n_groups = rows // NA_R
    variants, var_idx, win_start = [], [], []
    for g in range(n_groups):
        r0 = g * NA_R
        ws = int(np.clip(r0 - kr // 2, 0, rows - NA_KR))
        ro = -np.ones((NA_R, NA_KR), np.int32)
        for i in range(NA_R):
            r = r0 + i
            rs = int(np.clip(r - kr // 2, 0, rows - kr))
            assert ws <= rs and rs + kr <= ws + NA_KR
            for a in range(NA_KR):
                if rs <= ws + a < rs + kr:
                    ro[i, a] = ws + a - r + (NA_WIN_ROWS_MAX - 1)
        for v, known in enumerate(variants):
            if np.array_equal(known, ro):
                var_idx.append(v)
                break
        else:
            var_idx.append(len(variants))
            variants.append(ro)
        win_start.append(ws)
    return np.stack(variants), np.asarray(var_idx, np.int32), np.asarray(win_start, np.int32)


def _na_bias_tiles(rpb, row_off):
    n_var = row_off.shape[0]
    n_rows = 2 * NA_WIN_ROWS_MAX - 1
    n_cols = 2 * NA_WIN_COLS - 1
    col = np.arange(GRID_W)
    col_start = np.clip(col - NA_WIN_COLS // 2, 0, GRID_W - NA_WIN_COLS)
    kc = col[None, :]
    col_ok = (kc >= col_start[:, None]) & (kc < col_start[:, None] + NA_WIN_COLS)
    col_off = kc - col[:, None] + (NA_WIN_COLS - 1)
    col_hot = (col_ok[None] & (col_off[None] == np.arange(n_cols)[:, None, None])).astype(np.float32)
    row_hot = (row_off.reshape(-1)[:, None] == np.arange(n_rows)[None, :]).astype(np.float32)
    t = jnp.einsum("xr,hrc->hxc", row_hot, rpb, precision=_EXACT)
    t = jnp.einsum("hxc,cqk->hxqk", t, col_hot, precision=_EXACT)
    ok = (row_off >= 0).reshape(-1)[:, None, None] & col_ok[None]
    t = jnp.where(ok[None], t * LOG2E, NEG).reshape(NA_HEADS, n_var, NA_R, NA_KR, GRID_W, GRID_W)
    return t.transpose(1, 0, 2, 4, 3, 5).reshape(n_var, NA_HEADS, NA_R * GRID_W, NA_KR * GRID_W)


def _na_kernel(var_ref, ws_ref, q_ref, k_ref, v_ref, bias_ref, o_ref):
    del var_ref
    n_q = NA_R * GRID_W
    n_k = NA_KR * GRID_W
    k0 = pl.multiple_of(ws_ref[pl.program_id(1)] * GRID_W, GRID_W)
    low_half = lax.broadcasted_iota(jnp.int32, (n_q, LANES), 1) < NA_HEAD_DIM
    for hp in range(NA_HEADS // 2):
        cols = slice(hp * LANES, (hp + 1) * LANES)
        q2 = q_ref[:, cols]
        k2 = k_ref[pl.ds(k0, n_k), cols]
        v2 = jnp.concatenate([v_ref[pl.ds(k0, n_k), cols], jnp.ones((n_k, LANES), BF16)], axis=1)
        halves = []
        for half in range(2):
            keep = low_half if half == 0 else jnp.logical_not(low_half)
            qm = jnp.where(keep, q2, jnp.zeros_like(q2))
            s = lax.dot_general(qm, k2, (((1,), (1,)), ((), ())), preferred_element_type=F32)
            s = s + bias_ref[2 * hp + half]
            p = jnp.exp2(s - jnp.max(s, axis=-1, keepdims=True))
            o = jnp.dot(p.astype(BF16), v2, preferred_element_type=F32)
            halves.append(o[:, :LANES] / o[:, LANES:])
        o_ref[:, cols] = jnp.where(low_half, halves[0], halves[1]).astype(BF16)


def _na(q, k, v, rpb, tile_cache):
    bsz, seq, _ = q.shape
    rows = seq // GRID_W
    row_off, var_idx, win_start = _na_plan(rows)
    key = row_off.tobytes()
    if key not in tile_cache:
        tile_cache[key] = _na_bias_tiles(rpb, row_off)
    bias = tile_cache[key]
    n_q = NA_R * GRID_W
    n_k = NA_KR * GRID_W
    grid_spec = pltpu.PrefetchScalarGridSpec(
        num_scalar_prefetch=2,
        grid=(bsz, rows // NA_R),
        in_specs=[
            pl.BlockSpec((None, n_q, NA_WIDTH), lambda b, g, var, ws: (b, g, 0)),
            pl.BlockSpec((None, seq, NA_WIDTH), lambda b, g, var, ws: (b, 0, 0)),
            pl.BlockSpec((None, seq, NA_WIDTH), lambda b, g, var, ws: (b, 0, 0)),
            pl.BlockSpec((None, NA_HEADS, n_q, n_k), lambda b, g, var, ws: (var[g], 0, 0, 0)),
        ],
        out_specs=pl.BlockSpec((None, n_q, NA_WIDTH), lambda b, g, var, ws: (b, g, 0)),
    )
    return pl.pallas_call(
        _na_kernel,
        grid_spec=grid_spec,
        out_shape=jax.ShapeDtypeStruct((bsz, seq, NA_WIDTH), BF16),
        compiler_params=pltpu.CompilerParams(
            dimension_semantics=("arbitrary", "arbitrary"), vmem_limit_bytes=VMEM_LIMIT),
        name="na",
    )(jnp.asarray(var_idx), jnp.asarray(win_start), q, k, v, bias)


def _glu(u):
    a = u[:, :CONV_CH].astype(F32)
    g = u[:, CONV_CH:].astype(F32)
    return a * jax.nn.sigmoid(g)


def _conv_kernel(u_ref, ul_ref, ur_ref, w_ref, b_ref, lg_ref, lb_ref, o_ref, xs_ref):
    t = pl.program_id(1)
    ts = CONV_TS
    xs_ref[0:CONV_HALO, :] = jnp.where(t > 0, _glu(ul_ref[...]), 0.0)
    xs_ref[CONV_HALO:CONV_HALO + ts, :] = _glu(u_ref[...])
    xs_ref[CONV_HALO + ts:, :] = jnp.where(t < pl.num_programs(1) - 1, _glu(ur_ref[...]), 0.0)
    first = CONV_HALO - CONV_WIDTH // 2
    ext = CONV_RC + SUBLANES

    def chunk(rc, carry):
        r0 = pl.multiple_of(rc * CONV_RC, CONV_RC)
        groups = []
        for c0 in range(0, CONV_CH, LANES):
            slab = xs_ref[pl.ds(r0, CONV_RC + 2 * CONV_HALO), c0:c0 + LANES]
            acc = None
            for s in range(SUBLANES):
                part = None
                for j in range(s, CONV_WIDTH, SUBLANES):
                    term = w_ref[j:j + 1, c0:c0 + LANES] * slab[j - s:j - s + ext]
                    part = term if part is None else part + term
                shift = first + s
                part = part[shift:shift + CONV_RC]
                acc = part if acc is None else acc + part
            groups.append(acc)
        acc = jnp.concatenate(groups, axis=1) + b_ref[...]
        mu = jnp.mean(acc, axis=-1, keepdims=True)
        xc = acc - mu
        y = xc * lax.rsqrt(jnp.mean(xc * xc, axis=-1, keepdims=True) + EPS)
        y = y * lg_ref[...] + lb_ref[...]
        o_ref[pl.ds(r0, CONV_RC), :] = (y * jax.nn.sigmoid(y)).astype(BF16)
        return carry

    lax.fori_loop(0, ts // CONV_RC, chunk, 0)


def _conv(u, dw_w, dw_b, ln_g, ln_b, layer):
    bsz, seq, _ = u.shape
    ts = CONV_TS
    assert seq % ts == 0 and ts % CONV_HALO == 0 and ts % CONV_RC == 0
    assert CONV_HALO >= CONV_WIDTH // 2 and 2 * CONV_HALO >= SUBLANES + (CONV_WIDTH - 1) // SUBLANES * SUBLANES
    assert CONV_HALO - CONV_WIDTH // 2 + SUBLANES - 1 + CONV_RC <= CONV_RC + SUBLANES
    n_t = seq // ts
    per_tile = ts // CONV_HALO
    n_halo = seq // CONV_HALO
    vec = lambda: _const_spec((None, 1, CONV_CH), lambda b, t: (layer, 0, 0))
    return pl.pallas_call(
        _conv_kernel,
        grid=(bsz, n_t),
        in_specs=[
            pl.BlockSpec((None, ts, 2 * CONV_CH), lambda b, t: (b, t, 0)),
            pl.BlockSpec((None, CONV_HALO, 2 * CONV_CH),
                         lambda b, t: (b, jnp.maximum(t * per_tile - 1, 0), 0)),
            pl.BlockSpec((None, CONV_HALO, 2 * CONV_CH),
                         lambda b, t: (b, jnp.minimum((t + 1) * per_tile, n_halo - 1), 0)),
            _const_spec((None, CONV_WIDTH, CONV_CH), lambda b, t: (layer, 0, 0)),
            vec(), vec(), vec(),
        ],
        out_specs=pl.BlockSpec((None, ts, CONV_CH), lambda b, t: (b, t, 0)),
        out_shape=jax.ShapeDtypeStruct((bsz, seq, CONV_CH), BF16),
        scratch_shapes=[pltpu.VMEM((ts + 2 * CONV_HALO, CONV_CH), F32)],
        compiler_params=pltpu.CompilerParams(
            dimension_semantics=("arbitrary", "arbitrary"), vmem_limit_bytes=VMEM_LIMIT),
        name="conv",
    )(u, u, u, dw_w, dw_b, ln_g, ln_b)


def _t5_bucket(rel):
    nb = T5_BUCKETS // 2
    max_exact = nb // 2
    ret = jnp.where(rel > 0, nb, 0)
    n = jnp.abs(rel)
    nf = jnp.maximum(n, 1).astype(jnp.float32)
    large = max_exact + (jnp.log(nf / max_exact) / math.log(T5_MAX_DIST / max_exact)
                         * (nb - max_exact)).astype(jnp.int32)
    large = jnp.minimum(large, nb - 1)
    return ret + jnp.where(n < max_exact, n, large)


def _toeplitz(g, n):
    lead = g.shape[:-1]
    flat = jnp.tile(g, (1,) * len(lead) + (n,))
    return flat[..., :n * (2 * n - 1)].reshape(lead + (n, 2 * n - 1))[..., :n]


def _diff_bias(t5_bias):
    t = DIFF_T
    assert t >= T5_MAX_DIST
    m = np.arange(2 * t)
    q_minus_k = np.where(m < t, m, m - 2 * t)
    rel = np.stack([d * t - q_minus_k for d in (-1, 0, 1)]).astype(np.int32)
    g = t5_bias[_t5_bucket(jnp.asarray(rel))]
    near = _toeplitz(g.transpose(2, 0, 1), t)
    far = t5_bias[_t5_bucket(jnp.asarray([-T5_MAX_DIST, T5_MAX_DIST], jnp.int32))]
    far = jnp.broadcast_to(far.T[:, :, None, None], (DIFF_HEADS, 2, t, t))
    return jnp.concatenate([far[:, :1], near, far[:, 1:]], axis=1) * LOG2E


def _diff_kernel(lam_ref, qt_ref, k_ref, vt_ref, bias_ref, g_ref, o_ref,
                 s_ref, smax_ref, p_ref, alpha_ref, m_ref, acc_ref, vt1_ref, *, lam_init, n_k):
    t = DIFF_T
    n_pairs = n_k * n_k

    strips = [slice(c0, c0 + DIFF_STRIP) for c0 in range(0, 2 * t, DIFF_STRIP)]

    def scores(e, par, cols):
        qi, ki = e // n_k, e % n_k
        kk = k_ref[pl.ds(pl.multiple_of(ki * t, t), t), :]
        s = jnp.dot(kk, qt_ref[qi, :, cols], preferred_element_type=F32)
        q0 = cols.start % t
        s = s + bias_ref[jnp.clip(ki - qi, -2, 2) + 2, :, q0:q0 + DIFF_STRIP]
        s_ref[par, :, cols] = s
        smax_ref[par, :, cols] = jnp.max(s, axis=0, keepdims=True)

    def softmax(e, par, cols):
        m_old = jnp.where(e % n_k == 0, NEG, m_ref[:, cols])
        m_new = jnp.maximum(m_old, smax_ref[par, :, cols])
        m_ref[:, cols] = m_new
        alpha_ref[par, :, cols] = jnp.exp2(m_old - m_new)
        p_ref[par, :, cols] = jnp.exp2(s_ref[par, :, cols] - m_new).astype(BF16)

    def accumulate(e, par, cols):
        pv = jnp.dot(vt1_ref[e % n_k], p_ref[par, :, cols], preferred_element_type=F32)
        qi = e // n_k
        acc_ref[qi, :, cols] = alpha_ref[par, :, cols] * acc_ref[qi, :, cols] + pv

    def finalize(qi, carry):
        inv_l = 1.0 / acc_ref[qi, LANES:LANES + 1, :]
        ot = (acc_ref[qi, :LANES, :t] * inv_l[:, :t]
              - lam_ref[0] * (acc_ref[qi, :LANES, t:] * inv_l[:, t:]))
        o = _rms(ot.T, g_ref[...]) * (1.0 - lam_init)
        o_ref[pl.ds(pl.multiple_of(qi * t, t), t), :] = o.astype(BF16)
        return carry

    vt1_ref[:, :LANES, :] = vt_ref[...]
    vt1_ref[:, LANES:, :] = jnp.ones((n_k, DIFF_ONES, t), BF16)
    acc_ref[...] = jnp.zeros(acc_ref.shape, F32)
    alpha_ref[1] = jnp.zeros(alpha_ref.shape[1:], F32)
    p_ref[1] = jnp.zeros(p_ref.shape[1:], BF16)
    for cols in strips:
        scores(0, 0, cols)

    def step(e, par):
        e_next = jnp.minimum(e + 1, n_pairs - 1)
        for cols in strips:
            accumulate(jnp.maximum(e - 1, 0), 1 - par, cols)
            softmax(e, par, cols)
            scores(e_next, 1 - par, cols)

    def steps(j, carry):
        for u in range(DIFF_UNROLL):
            step(DIFF_UNROLL * j + u, u % 2)
        return carry

    lax.fori_loop(0, n_pairs // DIFF_UNROLL, steps, 0)
    for cols in strips:
        accumulate(n_pairs - 1, 1, cols)
    lax.fori_loop(0, n_k, finalize, 0)


def _diff(qt2, k, vt, bias, lam_full, subln_g, lam_init, layer):
    bsz, seq, _ = k.shape
    t = DIFF_T
    assert seq % t == 0 and IN_TM == t
    n_k = seq // t
    assert DIFF_UNROLL % 2 == 0 and (n_k * n_k) % DIFF_UNROLL == 0
    smem = pl.BlockSpec(memory_space=pltpu.SMEM)
    return pl.pallas_call(
        functools.partial(_diff_kernel, lam_init=lam_init, n_k=n_k),
        grid=(DIFF_HEADS, bsz),
        in_specs=[
            smem,
            pl.BlockSpec((n_k, LANES, 2 * t), lambda h, b: (b, h, 0)),
            pl.BlockSpec((None, seq, LANES), lambda h, b: (b, 0, h)),
            pl.BlockSpec((n_k, LANES, t), lambda h, b: (b, h, 0)),
            pl.BlockSpec((None, 5, t, t), lambda h, b: (h, 0, 0, 0)),
            _const_spec((None, 1, LANES), lambda h, b: (layer, 0, 0)),
        ],
        out_specs=pl.BlockSpec((None, seq, LANES), lambda h, b: (b, 0, h)),
        out_shape=jax.ShapeDtypeStruct((bsz, seq, DIFF_WIDTH), BF16),
        scratch_shapes=[
            pltpu.VMEM((2, t, 2 * t), F32),
            pltpu.VMEM((2, 1, 2 * t), F32),
            pltpu.VMEM((2, t, 2 * t), BF16),
            pltpu.VMEM((2, 1, 2 * t), F32),
            pltpu.VMEM((1, 2 * t), F32),
            pltpu.VMEM((n_k, LANES + DIFF_ONES, 2 * t), F32),
            pltpu.VMEM((n_k, LANES + DIFF_ONES, t), BF16),
        ],
        compiler_params=pltpu.CompilerParams(
            dimension_semantics=("arbitrary", "arbitrary"), vmem_limit_bytes=VMEM_LIMIT),
        name="diff",
    )(lam_full, qt2, k, vt, bias, subln_g)


_FFN_CHUNKS = ((0, 768), (768, 1536), (1536, 2304), (2304, FFN_HIDDEN))


def _post_kernel(x_ref, ona_ref, ocv_ref, odf_ref, gate_ref, wb_ref, wo_ref,
                 g_mix_ref, g_pre_ref, g_post_ref, wfi_ref, wfo_ref, o_ref, act_ref):
    merged = None
    for b, br_ref in enumerate((ona_ref, ocv_ref, odf_ref)):
        proj = jnp.dot(br_ref[...], wb_ref[b], preferred_element_type=F32)
        term = gate_ref[:, b * D_MODEL:(b + 1) * D_MODEL].astype(F32) * proj
        merged = term if merged is None else merged + term
    y = jnp.dot(merged.astype(BF16), wo_ref[...], preferred_element_type=F32)
    x1 = x_ref[...] + _rms(y, g_mix_ref[...])
    hf = _rms(x1, g_pre_ref[...]).astype(BF16)
    for c0, c1 in _FFN_CHUNKS:
        gate = jnp.dot(hf, wfi_ref[:, c0:c1], preferred_element_type=F32)
        up = jnp.dot(hf, wfi_ref[:, FFN_HIDDEN + c0:FFN_HIDDEN + c1], preferred_element_type=F32)
        act_ref[:, c0:c1] = (gate * jax.nn.sigmoid(gate) * up).astype(BF16)
    z = jnp.dot(act_ref[...], wfo_ref[...], preferred_element_type=F32)
    o_ref[...] = x1 + _rms(z, g_post_ref[...])


def _post(x, o_na, o_cv, o_df, gates, w_branch, w_out, g_mix, g_pre, g_post, w_ffn_in, w_ffn_out,
          layer):
    n_tok = x.shape[0]
    tm = POST_TM
    assert n_tok % tm == 0
    vec = lambda: _const_spec((None, 1, D_MODEL), lambda i: (layer, 0, 0))
    br = lambda: pl.BlockSpec((tm, BRANCH_W), lambda i: (i, 0))
    return pl.pallas_call(
        _post_kernel,
        grid=(n_tok // tm,),
        in_specs=[
            pl.BlockSpec((tm, D_MODEL), lambda i: (i, 0)),
            br(), br(), br(),
            pl.BlockSpec((tm, N_BRANCH * D_MODEL), lambda i: (i, 0)),
            _const_spec((None, N_BRANCH, BRANCH_W, D_MODEL), lambda i: (layer, 0, 0, 0)),
            _const_spec((None, D_MODEL, D_MODEL), lambda i: (layer, 0, 0)),
            vec(), vec(), vec(),
            _const_spec((None, D_MODEL, 2 * FFN_HIDDEN), lambda i: (layer, 0, 0)),
            _const_spec((None, FFN_HIDDEN, D_MODEL), lambda i: (layer, 0, 0)),
        ],
        out_specs=pl.BlockSpec((tm, D_MODEL), lambda i: (i, 0)),
        out_shape=jax.ShapeDtypeStruct((n_tok, D_MODEL), F32),
        scratch_shapes=[pltpu.VMEM((tm, FFN_HIDDEN), BF16)],
        compiler_params=pltpu.CompilerParams(
            dimension_semantics=("arbitrary",), vmem_limit_bytes=VMEM_LIMIT),
        name="post",
    )(x, o_na, o_cv, o_df, gates, w_branch, w_out, g_mix, g_pre, g_post, w_ffn_in, w_ffn_out)


def _trunk(x, p):
    bsz, seq, _ = x.shape
    n_tok = bsz * seq
    depth = p["w_in"].shape[0]
    x = x.reshape(n_tok, D_MODEL)
    for l in range(depth):
        lam_init = 0.8 - 0.6 * math.exp(-0.3 * l)
        naq, nak, nav, cvu, dqt, dk, dvt, gates = _in_proj(x, p["ln_mix_pre"], p["w_in"], p["b_gate"], l)
        seq3 = lambda a: a.reshape(bsz, seq, a.shape[-1])
        o_na = _na(seq3(naq), seq3(nak), seq3(nav), p["na_rpb"][l], p["na_tiles"][l])
        o_cv = _conv(seq3(cvu), p["conv_dw_w"], p["conv_dw_b"], p["conv_ln_g"], p["conv_ln_b"], l)
        o_df = _diff(dqt, seq3(dk), dvt, p["diff_bias"], p["lam_full"][l], p["diff_subln_g"],
                     lam_init, l)
        flat = lambda a: a.reshape(n_tok, a.shape[-1])
        x = _post(x, flat(o_na), flat(o_cv), flat(o_df), gates, p["w_branch"], p["w_out"],
                  p["ln_mix_post"], p["ln_ffn_pre"], p["ln_ffn_post"], p["w_ffn_in"], p["w_ffn_out"], l)
    return x.reshape(bsz, seq, D_MODEL)


def _prepare(w_in, b_gate, na_rpb, conv_dw_w, conv_dw_b, conv_ln_g, conv_ln_b,
             diff_lambda, diff_subln_g, t5_bias, w_branch, w_out,
             ln_mix_pre, ln_mix_post, ln_ffn_pre, ln_ffn_post, w_ffn_in, w_ffn_out):
    depth = w_in.shape[0]
    row = lambda a: a.reshape(depth, 1, a.shape[-1])
    lf = diff_lambda.astype(F32)
    lam_init = jnp.asarray([0.8 - 0.6 * math.exp(-0.3 * l) for l in range(depth)], F32)
    lam_full = (jnp.exp(jnp.sum(lf[:, 0] * lf[:, 1], axis=-1))
                - jnp.exp(jnp.sum(lf[:, 2] * lf[:, 3], axis=-1)) + lam_init)
    return dict(
        w_in=w_in.astype(BF16), b_gate=row(b_gate), na_rpb=na_rpb, na_tiles=[{} for _ in range(depth)],
        conv_dw_w=conv_dw_w, conv_dw_b=row(conv_dw_b), conv_ln_g=row(conv_ln_g), conv_ln_b=row(conv_ln_b),
        lam_full=lam_full.reshape(depth, 1), diff_subln_g=row(diff_subln_g),
        diff_bias=_diff_bias(t5_bias),
        w_branch=w_branch.astype(BF16), w_out=w_out.astype(BF16),
        ln_mix_pre=row(ln_mix_pre), ln_mix_post=row(ln_mix_post),
        ln_ffn_pre=row(ln_ffn_pre), ln_ffn_post=row(ln_ffn_post),
        w_ffn_in=w_ffn_in.astype(BF16), w_ffn_out=w_ffn_out.astype(BF16),
    )


def kernel(x_prompt, x_sample, w_in, b_gate, na_rpb, conv_dw_w, conv_dw_b, conv_ln_g, conv_ln_b,
           diff_lambda, diff_subln_g, t5_bias, w_branch, w_out,
           ln_mix_pre, ln_mix_post, ln_ffn_pre, ln_ffn_post, w_ffn_in, w_ffn_out):
    p = _prepare(w_in, b_gate, na_rpb, conv_dw_w, conv_dw_b, conv_ln_g, conv_ln_b,
                 diff_lambda, diff_subln_g, t5_bias, w_branch, w_out,
                 ln_mix_pre, ln_mix_post, ln_ffn_pre, ln_ffn_post, w_ffn_in, w_ffn_out)
    return (_trunk(x_prompt, p), _trunk(x_sample, p))
```

```python
import functools
import math

import numpy as np
import jax
import jax.numpy as jnp
from jax import lax
from jax.experimental import pallas as pl
from jax.experimental.pallas import tpu as pltpu

F32 = jnp.float32
BF16 = jnp.bfloat16

D_MODEL = 1024
GRID_W = 64
NA_HEADS = 8
NA_HEAD_DIM = 64
NA_WIDTH = NA_HEADS * NA_HEAD_DIM
NA_WIN_ROWS_MAX = 8
NA_WIN_COLS = 16
CONV_CH = 512
CONV_WIDTH = 31
DIFF_HEADS = 4
DIFF_HEAD_DIM = 64
DIFF_WIDTH = DIFF_HEADS * 2 * DIFF_HEAD_DIM
N_BRANCH = 3
BRANCH_W = 512
T5_BUCKETS = 32
T5_MAX_DIST = 128
FFN_HIDDEN = 2816
EPS = 1e-6

OFF_CONV = 3 * NA_WIDTH
OFF_DIFF = OFF_CONV + 2 * CONV_CH
OFF_GATE = OFF_DIFF + 3 * DIFF_WIDTH
IN_COLS = OFF_GATE + N_BRANCH * D_MODEL

LANES = 128
SUBLANES = 8
VMEM_LIMIT = 56 * 1024 * 1024

IN_TM = 512
POST_TM = 512
NA_R = 4
NA_KR = 12
CONV_TS = 512
CONV_HALO = 16
CONV_RC = 128
DIFF_T = 512
DIFF_STRIP = 256
DIFF_ONES = 16
DIFF_UNROLL = 2
DIFF_MAX_UNROLL = 4
NEG = -1e30
LOG2E = math.log2(math.e)

_EXACT = lax.Precision.HIGHEST


def _rms(x, g):
    return x * lax.rsqrt(jnp.mean(x * x, axis=-1, keepdims=True) + EPS) * g


def _const_spec(shape, index_map):
    return pl.BlockSpec(shape, index_map, pipeline_mode=pl.Buffered(1))


def _in_proj_kernel(x_ref, g_ref, w_ref, bg_ref,
                    naq_ref, nak_ref, nav_ref, cv_ref, dqt_ref, dk_ref, dvt_ref, gate_ref):
    h = _rms(x_ref[...], g_ref[...]).astype(BF16)

    def proj(c0, c1):
        return jnp.dot(h, w_ref[:, c0:c1], preferred_element_type=F32)

    naq_ref[...] = (proj(0, NA_WIDTH) * (NA_HEAD_DIM ** -0.5 * LOG2E)).astype(BF16)
    nak_ref[...] = proj(NA_WIDTH, 2 * NA_WIDTH).astype(BF16)
    nav_ref[...] = proj(2 * NA_WIDTH, 3 * NA_WIDTH).astype(BF16)
    cv_ref[...] = proj(OFF_CONV, OFF_DIFF).astype(BF16)
    dqt = (proj(OFF_DIFF, OFF_DIFF + DIFF_WIDTH) * (DIFF_HEAD_DIM ** -0.5 * LOG2E)).T
    comp0 = lax.broadcasted_iota(jnp.int32, dqt.shape, 0) % (2 * DIFF_HEAD_DIM) < DIFF_HEAD_DIM
    dqt_ref[:, :IN_TM] = jnp.where(comp0, dqt, 0.0).astype(BF16)
    dqt_ref[:, IN_TM:] = jnp.where(comp0, 0.0, dqt).astype(BF16)
    dk_ref[...] = proj(OFF_DIFF + DIFF_WIDTH, OFF_DIFF + 2 * DIFF_WIDTH).astype(BF16)
    dvt_ref[...] = proj(OFF_DIFF + 2 * DIFF_WIDTH, OFF_GATE).T.astype(BF16)
    for b in range(N_BRANCH):
        c0 = OFF_GATE + b * D_MODEL
        gate = jax.nn.sigmoid(proj(c0, c0 + D_MODEL) + bg_ref[:, b * D_MODEL:(b + 1) * D_MODEL])
        gate_ref[:, b * D_MODEL:(b + 1) * D_MODEL] = gate.astype(BF16)


def _in_proj(x, g_pre, w_in, b_gate, layer):
    n_tok = x.shape[0]
    tm = IN_TM
    assert n_tok % tm == 0
    n_t = n_tok // tm
    tok = lambda w: (pl.BlockSpec((tm, w), lambda i: (i, 0)), jax.ShapeDtypeStruct((n_tok, w), BF16))
    tok_t = lambda w, c: (pl.BlockSpec((None, w, c), lambda i: (i, 0, 0)),
                          jax.ShapeDtypeStruct((n_t, w, c), BF16))
    outs = [tok(NA_WIDTH), tok(NA_WIDTH), tok(NA_WIDTH), tok(2 * CONV_CH),
            tok_t(DIFF_WIDTH, 2 * tm), tok(DIFF_WIDTH), tok_t(DIFF_WIDTH, tm), tok(N_BRANCH * D_MODEL)]
    return pl.pallas_call(
        _in_proj_kernel,
        grid=(n_t,),
        in_specs=[
            pl.BlockSpec((tm, D_MODEL), lambda i: (i, 0)),
            _const_spec((None, 1, D_MODEL), lambda i: (layer, 0, 0)),
            _const_spec((None, D_MODEL, IN_COLS), lambda i: (layer, 0, 0)),
            _const_spec((None, 1, N_BRANCH * D_MODEL), lambda i: (layer, 0, 0)),
        ],
        out_specs=[o[0] for o in outs],
        out_shape=[o[1] for o in outs],
        compiler_params=pltpu.CompilerParams(
            dimension_semantics=("arbitrary",), vmem_limit_bytes=VMEM_LIMIT),
        name="in_proj",
    )(x, g_pre, w_in, b_gate)


def _na_plan(rows):
    assert rows % NA_R == 0 and rows >= NA_KR
    kr = min(NA_WIN_ROWS_MAX, rows)
    n_groups = rows // NA_R
    variants, var_idx, win_start = [], [], []
    for g in range(n_groups):
        r0 = g * NA_R
        ws = int(np.clip(r0 - kr // 2, 0, rows - NA_KR))
        ro = -np.ones((NA_R, NA_KR), np.int32)
        for i in range(NA_R):
            r = r0 + i
            rs = int(np.clip(r - kr // 2, 0, rows - kr))
            assert ws <= rs and rs + kr <= ws + NA_KR
            for a in range(NA_KR):
                if rs <= ws + a < rs + kr:
                    ro[i, a] = ws + a - r + (NA_WIN_ROWS_MAX - 1)
        for v, known in enumerate(variants):
            if np.array_equal(known, ro):
                var_idx.append(v)
                break
        else:
            var_idx.append(len(variants))
            variants.append(ro)
        win_start.append(ws)
    return np.stack(variants), np.asarray(var_idx, np.int32), np.asarray(win_start, np.int32)


def _na_bias_tiles(rpb, row_off):
    n_var = row_off.shape[0]
    n_rows = 2 * NA_WIN_ROWS_MAX - 1
    n_cols = 2 * NA_WIN_COLS - 1
    col = np.arange(GRID_W)
    col_start = np.clip(col - NA_WIN_COLS // 2, 0, GRID_W - NA_WIN_COLS)
    kc = col[None, :]
    col_ok = (kc >= col_start[:, None]) & (kc < col_start[:, None] + NA_WIN_COLS)
    col_off = kc - col[:, None] + (NA_WIN_COLS - 1)
    col_hot = (col_ok[None] & (col_off[None] == np.arange(n_cols)[:, None, None])).astype(np.float32)
    row_hot = (row_off.reshape(-1)[:, None] == np.arange(n_rows)[None, :]).astype(np.float32)
    t = jnp.einsum("xr,hrc->hxc", row_hot, rpb, precision=_EXACT)
    t = jnp.einsum("hxc,cqk->hxqk", t, col_hot, precision=_EXACT)
    ok = (row_off >= 0).reshape(-1)[:, None, None] & col_ok[None]
    t = jnp.where(ok[None], t * LOG2E, NEG).reshape(NA_HEADS, n_var, NA_R, NA_KR, GRID_W, GRID_W)
    return t.transpose(1, 0, 2, 4, 3, 5).reshape(n_var, NA_HEADS, NA_R * GRID_W, NA_KR * GRID_W)


def _na_kernel(var_ref, ws_ref, q_ref, k_ref, v_ref, bias_ref, o_ref):
    del var_ref
    n_q = NA_R * GRID_W
    n_k = NA_KR * GRID_W
    k0 = pl.multiple_of(ws_ref[pl.program_id(1)] * GRID_W, GRID_W)
    low_half = lax.broadcasted_iota(jnp.int32, (n_q, LANES), 1) < NA_HEAD_DIM
    for hp in range(NA_HEADS // 2):
        cols = slice(hp * LANES, (hp + 1) * LANES)
        q2 = q_ref[:, cols]
        k2 = k_ref[pl.ds(k0, n_k), cols]
        v2 = jnp.concatenate([v_ref[pl.ds(k0, n_k), cols], jnp.ones((n_k, LANES), BF16)], axis=1)
        halves = []
        for half in range(2):
            keep = low_half if half == 0 else jnp.logical_not(low_half)
            qm = jnp.where(keep, q2, jnp.zeros_like(q2))
            s = lax.dot_general(qm, k2, (((1,), (1,)), ((), ())), preferred_element_type=F32)
            s = s + bias_ref[2 * hp + half]
            p = jnp.exp2(s - jnp.max(s, axis=-1, keepdims=True))
            o = jnp.dot(p.astype(BF16), v2, preferred_element_type=F32)
            halves.append(o[:, :LANES] / o[:, LANES:])
        o_ref[:, cols] = jnp.where(low_half, halves[0], halves[1]).astype(BF16)


def _na(q, k, v, rpb, tile_cache):
    bsz, seq, _ = q.shape
    rows = seq // GRID_W
    row_off, var_idx, win_start = _na_plan(rows)
    key = row_off.tobytes()
    if key not in tile_cache:
        tile_cache[key] = _na_bias_tiles(rpb, row_off)
    bias = tile_cache[key]
    n_q = NA_R * GRID_W
    n_k = NA_KR * GRID_W
    grid_spec = pltpu.PrefetchScalarGridSpec(
        num_scalar_prefetch=2,
        grid=(bsz, rows // NA_R),
        in_specs=[
            pl.BlockSpec((None, n_q, NA_WIDTH), lambda b, g, var, ws: (b, g, 0)),
            pl.BlockSpec((None, seq, NA_WIDTH), lambda b, g, var, ws: (b, 0, 0)),
            pl.BlockSpec((None, seq, NA_WIDTH), lambda b, g, var, ws: (b, 0, 0)),
            pl.BlockSpec((None, NA_HEADS, n_q, n_k), lambda b, g, var, ws: (var[g], 0, 0, 0)),
        ],
        out_specs=pl.BlockSpec((None, n_q, NA_WIDTH), lambda b, g, var, ws: (b, g, 0)),
    )
    return pl.pallas_call(
        _na_kernel,
        grid_spec=grid_spec,
        out_shape=jax.ShapeDtypeStruct((bsz, seq, NA_WIDTH), BF16),
        compiler_params=pltpu.CompilerParams(
            dimension_semantics=("arbitrary", "arbitrary"), vmem_limit_bytes=VMEM_LIMIT),
        name="na",
    )(jnp.asarray(var_idx), jnp.asarray(win_start), q, k, v, bias)


def _glu(u):
    a = u[:, :CONV_CH].astype(F32)
    g = u[:, CONV_CH:].astype(F32)
    return a * jax.nn.sigmoid(g)


def _conv_kernel(u_ref, ul_ref, ur_ref, w_ref, b_ref, lg_ref, lb_ref, o_ref, xs_ref):
    t = pl.program_id(1)
    ts = CONV_TS
    xs_ref[0:CONV_HALO, :] = jnp.where(t > 0, _glu(ul_ref[...]), 0.0)
    xs_ref[CONV_HALO:CONV_HALO + ts, :] = _glu(u_ref[...])
    xs_ref[CONV_HALO + ts:, :] = jnp.where(t < pl.num_programs(1) - 1, _glu(ur_ref[...]), 0.0)
    first = CONV_HALO - CONV_WIDTH // 2
    ext = CONV_RC + SUBLANES

    def chunk(rc, carry):
        r0 = pl.multiple_of(rc * CONV_RC, CONV_RC)
        groups = []
        for c0 in range(0, CONV_CH, LANES):
            slab = xs_ref[pl.ds(r0, CONV_RC + 2 * CONV_HALO), c0:c0 + LANES]
            acc = None
            for s in range(SUBLANES):
                part = None
                for j in range(s, CONV_WIDTH, SUBLANES):
                    term = w_ref[j:j + 1, c0:c0 + LANES] * slab[j - s:j - s + ext]
                    part = term if part is None else part + term
                shift = first + s
                part = part[shift:shift + CONV_RC]
                acc = part if acc is None else acc + part
            groups.append(acc)
        acc = jnp.concatenate(groups, axis=1) + b_ref[...]
        mu = jnp.mean(acc, axis=-1, keepdims=True)
        xc = acc - mu
        y = xc * lax.rsqrt(jnp.mean(xc * xc, axis=-1, keepdims=True) + EPS)
        y = y * lg_ref[...] + lb_ref[...]
        o_ref[pl.ds(r0, CONV_RC), :] = (y * jax.nn.sigmoid(y)).astype(BF16)
        return carry

    lax.fori_loop(0, ts // CONV_RC, chunk, 0)


def _conv(u, dw_w, dw_b, ln_g, ln_b, layer):
    bsz, seq, _ = u.shape
    ts = CONV_TS
    assert seq % ts == 0 and ts % CONV_HALO == 0 and ts % CONV_RC == 0
    assert CONV_HALO >= CONV_WIDTH // 2 and 2 * CONV_HALO >= SUBLANES + (CONV_WIDTH - 1) // SUBLANES * SUBLANES
    assert CONV_HALO - CONV_WIDTH // 2 + SUBLANES - 1 + CONV_RC <= CONV_RC + SUBLANES
    n_t = seq // ts
    per_tile = ts // CONV_HALO
    n_halo = seq // CONV_HALO
    vec = lambda: _const_spec((None, 1, CONV_CH), lambda b, t: (layer, 0, 0))
    return pl.pallas_call(
        _conv_kernel,
        grid=(bsz, n_t),
        in_specs=[
            pl.BlockSpec((None, ts, 2 * CONV_CH), lambda b, t: (b, t, 0)),
            pl.BlockSpec((None, CONV_HALO, 2 * CONV_CH),
                         lambda b, t: (b, jnp.maximum(t * per_tile - 1, 0), 0)),
            pl.BlockSpec((None, CONV_HALO, 2 * CONV_CH),
                         lambda b, t: (b, jnp.minimum((t + 1) * per_tile, n_halo - 1), 0)),
            _const_spec((None, CONV_WIDTH, CONV_CH), lambda b, t: (layer, 0, 0)),
            vec(), vec(), vec(),
        ],
        out_specs=pl.BlockSpec((None, ts, CONV_CH), lambda b, t: (b, t, 0)),
        out_shape=jax.ShapeDtypeStruct((bsz, seq, CONV_CH), BF16),
        scratch_shapes=[pltpu.VMEM((ts + 2 * CONV_HALO, CONV_CH), F32)],
        compiler_params=pltpu.CompilerParams(
            dimension_semantics=("arbitrary", "arbitrary"), vmem_limit_bytes=VMEM_LIMIT),
        name="conv",
    )(u, u, u, dw_w, dw_b, ln_g, ln_b)


def _t5_bucket(rel):
    nb = T5_BUCKETS // 2
    max_exact = nb // 2
    ret = jnp.where(rel > 0, nb, 0)
    n = jnp.abs(rel)
    nf = jnp.maximum(n, 1).astype(jnp.float32)
    large = max_exact + (jnp.log(nf / max_exact) / math.log(T5_MAX_DIST / max_exact)
                         * (nb - max_exact)).astype(jnp.int32)
    large = jnp.minimum(large, nb - 1)
    return ret + jnp.where(n < max_exact, n, large)


def _toeplitz(g, n):
    lead = g.shape[:-1]
    flat = jnp.tile(g, (1,) * len(lead) + (n,))
    return flat[..., :n * (2 * n - 1)].reshape(lead + (n, 2 * n - 1))[..., :n]


def _diff_bias(t5_bias):
    t = DIFF_T
    assert t >= T5_MAX_DIST
    m = np.arange(2 * t)
    q_minus_k = np.where(m < t, m, m - 2 * t)
    rel = np.stack([d * t - q_minus_k for d in (-1, 0, 1)]).astype(np.int32)
    g = t5_bias[_t5_bucket(jnp.asarray(rel))]
    near = _toeplitz(g.transpose(2, 0, 1), t)
    far = t5_bias[_t5_bucket(jnp.asarray([-T5_MAX_DIST, T5_MAX_DIST], jnp.int32))]
    far = jnp.broadcast_to(far.T[:, :, None, None], (DIFF_HEADS, 2, t, t))
    return jnp.concatenate([far[:, :1], near, far[:, 1:]], axis=1) * LOG2E


def _diff_kernel(lam_ref, qt_ref, k_ref, vt_ref, bias_ref, g_ref, o_ref,
                 s_ref, smax_ref, p_ref, alpha_ref, m_ref, acc_ref, vt1_ref, *, lam_init, n_k):
    t = DIFF_T
    n_pairs = n_k * n_k

    strips = [slice(c0, c0 + DIFF_STRIP) for c0 in range(0, 2 * t, DIFF_STRIP)]

    def scores(e, par, cols):
        qi, ki = e // n_k, e % n_k
        kk = k_ref[pl.ds(pl.multiple_of(ki * t, t), t), :]
        s = jnp.dot(kk, qt_ref[qi, :, cols], preferred_element_type=F32)
        q0 = cols.start % t
        s = s + bias_ref[jnp.clip(ki - qi, -2, 2) + 2, :, q0:q0 + DIFF_STRIP]
        s_ref[par, :, cols] = s
        smax_ref[par, :, cols] = jnp.max(s, axis=0, keepdims=True)

    def softmax(e, par, cols):
        m_old = jnp.where(e % n_k == 0, NEG, m_ref[:, cols])
        m_new = jnp.maximum(m_old, smax_ref[par, :, cols])
        m_ref[:, cols] = m_new
        alpha_ref[par, :, cols] = jnp.exp2(m_old - m_new)
        p_ref[par, :, cols] = jnp.exp2(s_ref[par, :, cols] - m_new).astype(BF16)

    def accumulate(e, par, cols):
        pv = jnp.dot(vt1_ref[e % n_k], p_ref[par, :, cols], preferred_element_type=F32)
        qi = e // n_k
        acc_ref[qi, :, cols] = alpha_ref[par, :, cols] * acc_ref[qi, :, cols] + pv

    def finalize(qi, carry):
        inv_l = 1.0 / acc_ref[qi, LANES:LANES + 1, :]
        ot = (acc_ref[qi, :LANES, :t] * inv_l[:, :t]
              - lam_ref[0] * (acc_ref[qi, :LANES, t:] * inv_l[:, t:]))
        o = _rms(ot.T, g_ref[...]) * (1.0 - lam_init)
        o_ref[pl.ds(pl.multiple_of(qi * t, t), t), :] = o.astype(BF16)
        return carry

    vt1_ref[:, :LANES, :] = vt_ref[...]
    vt1_ref[:, LANES:, :] = jnp.ones((n_k, DIFF_ONES, t), BF16)
    acc_ref[...] = jnp.zeros(acc_ref.shape, F32)
    alpha_ref[1] = jnp.zeros(alpha_ref.shape[1:], F32)
    p_ref[1] = jnp.zeros(p_ref.shape[1:], BF16)
    for cols in strips:
        scores(0, 0, cols)

    def step(e, par):
        e_next = jnp.minimum(e + 1, n_pairs - 1)
        for cols in strips:
            accumulate(jnp.maximum(e - 1, 0), 1 - par, cols)
            softmax(e, par, cols)
            scores(e_next, 1 - par, cols)

    unroll = n_k if n_k <= DIFF_MAX_UNROLL else DIFF_UNROLL
    inline_finalize = unroll == n_k

    def steps(j, carry):
        for u in range(unroll):
            step(unroll * j + u, u % 2)
            if inline_finalize and u == 0:
                finalize(jnp.maximum(j - 1, 0), carry)
        return carry

    lax.fori_loop(0, n_pairs // unroll, steps, 0)
    for cols in strips:
        accumulate(n_pairs - 1, 1, cols)
    if inline_finalize:
        finalize(n_k - 1, 0)
    else:
        lax.fori_loop(0, n_k, finalize, 0)


def _diff(qt2, k, vt, bias, lam_full, subln_g, lam_init, layer):
    bsz, seq, _ = k.shape
    t = DIFF_T
    assert seq % t == 0 and IN_TM == t
    n_k = seq // t
    assert DIFF_UNROLL % 2 == 0 and n_k % 2 == 0
    smem = pl.BlockSpec(memory_space=pltpu.SMEM)
    return pl.pallas_call(
        functools.partial(_diff_kernel, lam_init=lam_init, n_k=n_k),
        grid=(DIFF_HEADS, bsz),
        in_specs=[
            smem,
            pl.BlockSpec((n_k, LANES, 2 * t), lambda h, b: (b, h, 0)),
            pl.BlockSpec((None, seq, LANES), lambda h, b: (b, 0, h)),
            pl.BlockSpec((n_k, LANES, t), lambda h, b: (b, h, 0)),
            pl.BlockSpec((None, 5, t, t), lambda h, b: (h, 0, 0, 0)),
            _const_spec((None, 1, LANES), lambda h, b: (layer, 0, 0)),
        ],
        out_specs=pl.BlockSpec((None, seq, LANES), lambda h, b: (b, 0, h)),
        out_shape=jax.ShapeDtypeStruct((bsz, seq, DIFF_WIDTH), BF16),
        scratch_shapes=[
            pltpu.VMEM((2, t, 2 * t), F32),
            pltpu.VMEM((2, 1, 2 * t), F32),
            pltpu.VMEM((2, t, 2 * t), BF16),
            pltpu.VMEM((2, 1, 2 * t), F32),
            pltpu.VMEM((1, 2 * t), F32),
            pltpu.VMEM((n_k, LANES + DIFF_ONES, 2 * t), F32),
            pltpu.VMEM((n_k, LANES + DIFF_ONES, t), BF16),
        ],
        compiler_params=pltpu.CompilerParams(
            dimension_semantics=("arbitrary", "arbitrary"), vmem_limit_bytes=VMEM_LIMIT),
        name="diff",
    )(lam_full, qt2, k, vt, bias, subln_g)


_FFN_CHUNKS = ((0, 768), (768, 1536), (1536, 2304), (2304, FFN_HIDDEN))


def _post_kernel(x_ref, ona_ref, ocv_ref, odf_ref, gate_ref, wb_ref, wo_ref,
                 g_mix_ref, g_pre_ref, g_post_ref, wfi_ref, wfo_ref, o_ref, act_ref):
    merged = None
    for b, br_ref in enumerate((ona_ref, ocv_ref, odf_ref)):
        proj = jnp.dot(br_ref[...], wb_ref[b], preferred_element_type=F32)
        term = gate_ref[:, b * D_MODEL:(b + 1) * D_MODEL].astype(F32) * proj
        merged = term if merged is None else merged + term
    y = jnp.dot(merged.astype(BF16), wo_ref[...], preferred_element_type=F32)
    x1 = x_ref[...] + _rms(y, g_mix_ref[...])
    hf = _rms(x1, g_pre_ref[...]).astype(BF16)
    for c0, c1 in _FFN_CHUNKS:
        gate = jnp.dot(hf, wfi_ref[:, c0:c1], preferred_element_type=F32)
        up = jnp.dot(hf, wfi_ref[:, FFN_HIDDEN + c0:FFN_HIDDEN + c1], preferred_element_type=F32)
        act_ref[:, c0:c1] = (gate * jax.nn.sigmoid(gate) * up).astype(BF16)
    z = jnp.dot(act_ref[...], wfo_ref[...], preferred_element_type=F32)
    o_ref[...] = x1 + _rms(z, g_post_ref[...])


def _post(x, o_na, o_cv, o_df, gates, w_branch, w_out, g_mix, g_pre, g_post, w_ffn_in, w_ffn_out,
          layer):
    n_tok = x.shape[0]
    tm = POST_TM
    assert n_tok % tm == 0
    vec = lambda: _const_spec((None, 1, D_MODEL), lambda i: (layer, 0, 0))
    br = lambda: pl.BlockSpec((tm, BRANCH_W), lambda i: (i, 0))
    return pl.pallas_call(
        _post_kernel,
        grid=(n_tok // tm,),
        in_specs=[
            pl.BlockSpec((tm, D_MODEL), lambda i: (i, 0)),
            br(), br(), br(),
            pl.BlockSpec((tm, N_BRANCH * D_MODEL), lambda i: (i, 0)),
            _const_spec((None, N_BRANCH, BRANCH_W, D_MODEL), lambda i: (layer, 0, 0, 0)),
            _const_spec((None, D_MODEL, D_MODEL), lambda i: (layer, 0, 0)),
            vec(), vec(), vec(),
            _const_spec((None, D_MODEL, 2 * FFN_HIDDEN), lambda i: (layer, 0, 0)),
            _const_spec((None, FFN_HIDDEN, D_MODEL), lambda i: (layer, 0, 0)),
        ],
        out_specs=pl.BlockSpec((tm, D_MODEL), lambda i: (i, 0)),
        out_shape=jax.ShapeDtypeStruct((n_tok, D_MODEL), F32),
        scratch_shapes=[pltpu.VMEM((tm, FFN_HIDDEN), BF16)],
        compiler_params=pltpu.CompilerParams(
            dimension_semantics=("arbitrary",), vmem_limit_bytes=VMEM_LIMIT),
        name="post",
    )(x, o_na, o_cv, o_df, gates, w_branch, w_out, g_mix, g_pre, g_post, w_ffn_in, w_ffn_out)


def _trunk(x, p):
    bsz, seq, _ = x.shape
    n_tok = bsz * seq
    depth = p["w_in"].shape[0]
    x = x.reshape(n_tok, D_MODEL)
    for l in range(depth):
        lam_init = 0.8 - 0.6 * math.exp(-0.3 * l)
        naq, nak, nav, cvu, dqt, dk, dvt, gates = _in_proj(x, p["ln_mix_pre"], p["w_in"], p["b_gate"], l)
        seq3 = lambda a: a.reshape(bsz, seq, a.shape[-1])
        o_na = _na(seq3(naq), seq3(nak), seq3(nav), p["na_rpb"][l], p["na_tiles"][l])
        o_cv = _conv(seq3(cvu), p["conv_dw_w"], p["conv_dw_b"], p["conv_ln_g"], p["conv_ln_b"], l)
        o_df = _diff(dqt, seq3(dk), dvt, p["diff_bias"], p["lam_full"][l], p["diff_subln_g"],
                     lam_init, l)
        flat = lambda a: a.reshape(n_tok, a.shape[-1])
        x = _post(x, flat(o_na), flat(o_cv), flat(o_df), gates, p["w_branch"], p["w_out"],
                  p["ln_mix_post"], p["ln_ffn_pre"], p["ln_ffn_post"], p["w_ffn_in"], p["w_ffn_out"], l)
    return x.reshape(bsz, seq, D_MODEL)


def _prepare(w_in, b_gate, na_rpb, conv_dw_w, conv_dw_b, conv_ln_g, conv_ln_b,
             diff_lambda, diff_subln_g, t5_bias, w_branch, w_out,
             ln_mix_pre, ln_mix_post, ln_ffn_pre, ln_ffn_post, w_ffn_in, w_ffn_out):
    depth = w_in.shape[0]
    row = lambda a: a.reshape(depth, 1, a.shape[-1])
    lf = diff_lambda.astype(F32)
    lam_init = jnp.asarray([0.8 - 0.6 * math.exp(-0.3 * l) for l in range(depth)], F32)
    lam_full = (jnp.exp(jnp.sum(lf[:, 0] * lf[:, 1], axis=-1))
                - jnp.exp(jnp.sum(lf[:, 2] * lf[:, 3], axis=-1)) + lam_init)
    return dict(
        w_in=w_in.astype(BF16), b_gate=row(b_gate), na_rpb=na_rpb, na_tiles=[{} for _ in range(depth)],
        conv_dw_w=conv_dw_w, conv_dw_b=row(conv_dw_b), conv_ln_g=row(conv_ln_g), conv_ln_b=row(conv_ln_b),
        lam_full=lam_full.reshape(depth, 1), diff_subln_g=row(diff_subln_g),
        diff_bias=_diff_bias(t5_bias),
        w_branch=w_branch.astype(BF16), w_out=w_out.astype(BF16),
        ln_mix_pre=row(ln_mix_pre), ln_mix_post=row(ln_mix_post),
        ln_ffn_pre=row(ln_ffn_pre), ln_ffn_post=row(ln_ffn_post),
        w_ffn_in=w_ffn_in.astype(BF16), w_ffn_out=w_ffn_out.astype(BF16),
    )


def kernel(x_prompt, x_sample, w_in, b_gate, na_rpb, conv_dw_w, conv_dw_b, conv_ln_g, conv_ln_b,
           diff_lambda, diff_subln_g, t5_bias, w_branch, w_out,
           ln_mix_pre, ln_mix_post, ln_ffn_pre, ln_ffn_post, w_ffn_in, w_ffn_out):
    p = _prepare(w_in, b_gate, na_rpb, conv_dw_w, conv_dw_b, conv_ln_g, conv_ln_b,
                 diff_lambda, diff_subln_g, t5_bias, w_branch, w_out,
                 ln_mix_pre, ln_mix_post, ln_ffn_pre, ln_ffn_post, w_ffn_in, w_ffn_out)
    return (_trunk(x_prompt, p), _trunk(x_sample, p))
```

```python
import functools
import math

import numpy as np
import jax
import jax.numpy as jnp
from jax import lax
from jax.experimental import pallas as pl
from jax.experimental.pallas import tpu as pltpu

F32 = jnp.float32
BF16 = jnp.bfloat16

D_MODEL = 1024
GRID_W = 64
NA_HEADS = 8
NA_HEAD_DIM = 64
NA_WIDTH = NA_HEADS * NA_HEAD_DIM
NA_WIN_ROWS_MAX = 8
NA_WIN_COLS = 16
CONV_CH = 512
CONV_WIDTH = 31
DIFF_HEADS = 4
DIFF_HEAD_DIM = 64
DIFF_WIDTH = DIFF_HEADS * 2 * DIFF_HEAD_DIM
N_BRANCH = 3
BRANCH_W = 512
T5_BUCKETS = 32
T5_MAX_DIST = 128
FFN_HIDDEN = 2816
EPS = 1e-6

OFF_CONV = 3 * NA_WIDTH
OFF_DIFF = OFF_CONV + 2 * CONV_CH
OFF_GATE = OFF_DIFF + 3 * DIFF_WIDTH
IN_COLS = OFF_GATE + N_BRANCH * D_MODEL

LANES = 128
SUBLANES = 8
VMEM_LIMIT = 56 * 1024 * 1024

IN_TM = 512
POST_TM = 512
NA_R = 4
NA_KR = 12
CONV_TS = 512
CONV_HALO = 16
CONV_RC = 128
DIFF_T = 512
DIFF_STRIP = 256
DIFF_ONES = 16
DIFF_UNROLL = 2
DIFF_MAX_UNROLL = 4
DIFF_PAIRS_PER_STEP = 64
NEG = -1e30
LOG2E = math.log2(math.e)

_EXACT = lax.Precision.HIGHEST


def _rms(x, g):
    return x * lax.rsqrt(jnp.mean(x * x, axis=-1, keepdims=True) + EPS) * g


def _const_spec(shape, index_map):
    return pl.BlockSpec(shape, index_map, pipeline_mode=pl.Buffered(1))


def _in_proj_kernel(x_ref, g_ref, w_ref, bg_ref,
                    naq_ref, nak_ref, nav_ref, cv_ref, dqt_ref, dk_ref, dvt_ref, gate_ref):
    h = _rms(x_ref[...], g_ref[...]).astype(BF16)

    def proj(c0, c1):
        return jnp.dot(h, w_ref[:, c0:c1], preferred_element_type=F32)

    naq_ref[...] = (proj(0, NA_WIDTH) * (NA_HEAD_DIM ** -0.5 * LOG2E)).astype(BF16)
    nak_ref[...] = proj(NA_WIDTH, 2 * NA_WIDTH).astype(BF16)
    nav_ref[...] = proj(2 * NA_WIDTH, 3 * NA_WIDTH).astype(BF16)
    cv_ref[...] = proj(OFF_CONV, OFF_DIFF).astype(BF16)
    dqt = (proj(OFF_DIFF, OFF_DIFF + DIFF_WIDTH) * (DIFF_HEAD_DIM ** -0.5 * LOG2E)).T
    comp0 = lax.broadcasted_iota(jnp.int32, dqt.shape, 0) % (2 * DIFF_HEAD_DIM) < DIFF_HEAD_DIM
    dqt_ref[:, :IN_TM] = jnp.where(comp0, dqt, 0.0).astype(BF16)
    dqt_ref[:, IN_TM:] = jnp.where(comp0, 0.0, dqt).astype(BF16)
    dk_ref[...] = proj(OFF_DIFF + DIFF_WIDTH, OFF_DIFF + 2 * DIFF_WIDTH).astype(BF16)
    dvt_ref[...] = proj(OFF_DIFF + 2 * DIFF_WIDTH, OFF_GATE).T.astype(BF16)
    for b in range(N_BRANCH):
        c0 = OFF_GATE + b * D_MODEL
        gate = jax.nn.sigmoid(proj(c0, c0 + D_MODEL) + bg_ref[:, b * D_MODEL:(b + 1) * D_MODEL])
        gate_ref[:, b * D_MODEL:(b + 1) * D_MODEL] = gate.astype(BF16)


def _in_proj(x, g_pre, w_in, b_gate, layer):
    n_tok = x.shape[0]
    tm = IN_TM
    assert n_tok % tm == 0
    n_t = n_tok // tm
    tok = lambda w: (pl.BlockSpec((tm, w), lambda i: (i, 0)), jax.ShapeDtypeStruct((n_tok, w), BF16))
    tok_t = lambda w, c: (pl.BlockSpec((None, w, c), lambda i: (i, 0, 0)),
                          jax.ShapeDtypeStruct((n_t, w, c), BF16))
    outs = [tok(NA_WIDTH), tok(NA_WIDTH), tok(NA_WIDTH), tok(2 * CONV_CH),
            tok_t(DIFF_WIDTH, 2 * tm), tok(DIFF_WIDTH), tok_t(DIFF_WIDTH, tm), tok(N_BRANCH * D_MODEL)]
    return pl.pallas_call(
        _in_proj_kernel,
        grid=(n_t,),
        in_specs=[
            pl.BlockSpec((tm, D_MODEL), lambda i: (i, 0)),
            _const_spec((None, 1, D_MODEL), lambda i: (layer, 0, 0)),
            _const_spec((None, D_MODEL, IN_COLS), lambda i: (layer, 0, 0)),
            _const_spec((None, 1, N_BRANCH * D_MODEL), lambda i: (layer, 0, 0)),
        ],
        out_specs=[o[0] for o in outs],
        out_shape=[o[1] for o in outs],
        compiler_params=pltpu.CompilerParams(
            dimension_semantics=("arbitrary",), vmem_limit_bytes=VMEM_LIMIT),
        name="in_proj",
    )(x, g_pre, w_in, b_gate)


def _na_plan(rows):
    assert rows % NA_R == 0 and rows >= NA_KR
    kr = min(NA_WIN_ROWS_MAX, rows)
    n_groups = rows // NA_R
    variants, var_idx, win_start = [], [], []
    for g in range(n_groups):
        r0 = g * NA_R
        ws = int(np.clip(r0 - kr // 2, 0, rows - NA_KR))
        ro = -np.ones((NA_R, NA_KR), np.int32)
        for i in range(NA_R):
            r = r0 + i
            rs = int(np.clip(r - kr // 2, 0, rows - kr))
            assert ws <= rs and rs + kr <= ws + NA_KR
            for a in range(NA_KR):
                if rs <= ws + a < rs + kr:
                    ro[i, a] = ws + a - r + (NA_WIN_ROWS_MAX - 1)
        for v, known in enumerate(variants):
            if np.array_equal(known, ro):
                var_idx.append(v)
                break
        else:
            var_idx.append(len(variants))
            variants.append(ro)
        win_start.append(ws)
    return np.stack(variants), np.asarray(var_idx, np.int32), np.asarray(win_start, np.int32)


def _na_bias_tiles(rpb, row_off):
    n_var = row_off.shape[0]
    n_rows = 2 * NA_WIN_ROWS_MAX - 1
    n_cols = 2 * NA_WIN_COLS - 1
    col = np.arange(GRID_W)
    col_start = np.clip(col - NA_WIN_COLS // 2, 0, GRID_W - NA_WIN_COLS)
    kc = col[None, :]
    col_ok = (kc >= col_start[:, None]) & (kc < col_start[:, None] + NA_WIN_COLS)
    col_off = kc - col[:, None] + (NA_WIN_COLS - 1)
    col_hot = (col_ok[None] & (col_off[None] == np.arange(n_cols)[:, None, None])).astype(np.float32)
    row_hot = (row_off.reshape(-1)[:, None] == np.arange(n_rows)[None, :]).astype(np.float32)
    t = jnp.einsum("xr,hrc->hxc", row_hot, rpb, precision=_EXACT)
    t = jnp.einsum("hxc,cqk->hxqk", t, col_hot, precision=_EXACT)
    ok = (row_off >= 0).reshape(-1)[:, None, None] & col_ok[None]
    t = jnp.where(ok[None], t * LOG2E, NEG).reshape(NA_HEADS, n_var, NA_R, NA_KR, GRID_W, GRID_W)
    return t.transpose(1, 0, 2, 4, 3, 5).reshape(n_var, NA_HEADS, NA_R * GRID_W, NA_KR * GRID_W)


def _na_kernel(var_ref, ws_ref, q_ref, k_ref, v_ref, bias_ref, o_ref):
    del var_ref
    n_q = NA_R * GRID_W
    n_k = NA_KR * GRID_W
    k0 = pl.multiple_of(ws_ref[pl.program_id(1)] * GRID_W, GRID_W)
    low_half = lax.broadcasted_iota(jnp.int32, (n_q, LANES), 1) < NA_HEAD_DIM
    for hp in range(NA_HEADS // 2):
        cols = slice(hp * LANES, (hp + 1) * LANES)
        q2 = q_ref[:, cols]
        k2 = k_ref[pl.ds(k0, n_k), cols]
        v2 = jnp.concatenate([v_ref[pl.ds(k0, n_k), cols], jnp.ones((n_k, LANES), BF16)], axis=1)
        halves = []
        for half in range(2):
            keep = low_half if half == 0 else jnp.logical_not(low_half)
            qm = jnp.where(keep, q2, jnp.zeros_like(q2))
            s = lax.dot_general(qm, k2, (((1,), (1,)), ((), ())), preferred_element_type=F32)
            s = s + bias_ref[2 * hp + half]
            p = jnp.exp2(s - jnp.max(s, axis=-1, keepdims=True))
            o = jnp.dot(p.astype(BF16), v2, preferred_element_type=F32)
            halves.append(o[:, :LANES] / o[:, LANES:])
        o_ref[:, cols] = jnp.where(low_half, halves[0], halves[1]).astype(BF16)


def _na(q, k, v, rpb, tile_cache):
    bsz, seq, _ = q.shape
    rows = seq // GRID_W
    row_off, var_idx, win_start = _na_plan(rows)
    key = row_off.tobytes()
    if key not in tile_cache:
        tile_cache[key] = _na_bias_tiles(rpb, row_off)
    bias = tile_cache[key]
    n_q = NA_R * GRID_W
    n_k = NA_KR * GRID_W
    grid_spec = pltpu.PrefetchScalarGridSpec(
        num_scalar_prefetch=2,
        grid=(bsz, rows // NA_R),
        in_specs=[
            pl.BlockSpec((None, n_q, NA_WIDTH), lambda b, g, var, ws: (b, g, 0)),
            pl.BlockSpec((None, seq, NA_WIDTH), lambda b, g, var, ws: (b, 0, 0)),
            pl.BlockSpec((None, seq, NA_WIDTH), lambda b, g, var, ws: (b, 0, 0)),
            pl.BlockSpec((None, NA_HEADS, n_q, n_k), lambda b, g, var, ws: (var[g], 0, 0, 0)),
        ],
        out_specs=pl.BlockSpec((None, n_q, NA_WIDTH), lambda b, g, var, ws: (b, g, 0)),
    )
    return pl.pallas_call(
        _na_kernel,
        grid_spec=grid_spec,
        out_shape=jax.ShapeDtypeStruct((bsz, seq, NA_WIDTH), BF16),
        compiler_params=pltpu.CompilerParams(
            dimension_semantics=("arbitrary", "arbitrary"), vmem_limit_bytes=VMEM_LIMIT),
        name="na",
    )(jnp.asarray(var_idx), jnp.asarray(win_start), q, k, v, bias)


def _glu(u):
    a = u[:, :CONV_CH].astype(F32)
    g = u[:, CONV_CH:].astype(F32)
    return a * jax.nn.sigmoid(g)


def _conv_kernel(u_ref, ul_ref, ur_ref, w_ref, b_ref, lg_ref, lb_ref, o_ref, xs_ref):
    t = pl.program_id(1)
    ts = CONV_TS
    xs_ref[0:CONV_HALO, :] = jnp.where(t > 0, _glu(ul_ref[...]), 0.0)
    xs_ref[CONV_HALO:CONV_HALO + ts, :] = _glu(u_ref[...])
    xs_ref[CONV_HALO + ts:, :] = jnp.where(t < pl.num_programs(1) - 1, _glu(ur_ref[...]), 0.0)
    first = CONV_HALO - CONV_WIDTH // 2
    ext = CONV_RC + SUBLANES

    def chunk(rc, carry):
        r0 = pl.multiple_of(rc * CONV_RC, CONV_RC)
        groups = []
        for c0 in range(0, CONV_CH, LANES):
            slab = xs_ref[pl.ds(r0, CONV_RC + 2 * CONV_HALO), c0:c0 + LANES]
            acc = None
            for s in range(SUBLANES):
                part = None
                for j in range(s, CONV_WIDTH, SUBLANES):
                    term = w_ref[j:j + 1, c0:c0 + LANES] * slab[j - s:j - s + ext]
                    part = term if part is None else part + term
                shift = first + s
                part = part[shift:shift + CONV_RC]
                acc = part if acc is None else acc + part
            groups.append(acc)
        acc = jnp.concatenate(groups, axis=1) + b_ref[...]
        mu = jnp.mean(acc, axis=-1, keepdims=True)
        xc = acc - mu
        y = xc * lax.rsqrt(jnp.mean(xc * xc, axis=-1, keepdims=True) + EPS)
        y = y * lg_ref[...] + lb_ref[...]
        o_ref[pl.ds(r0, CONV_RC), :] = (y * jax.nn.sigmoid(y)).astype(BF16)
        return carry

    lax.fori_loop(0, ts // CONV_RC, chunk, 0)


def _conv(u, dw_w, dw_b, ln_g, ln_b, layer):
    bsz, seq, _ = u.shape
    ts = CONV_TS
    assert seq % ts == 0 and ts % CONV_HALO == 0 and ts % CONV_RC == 0
    assert CONV_HALO >= CONV_WIDTH // 2 and 2 * CONV_HALO >= SUBLANES + (CONV_WIDTH - 1) // SUBLANES * SUBLANES
    assert CONV_HALO - CONV_WIDTH // 2 + SUBLANES - 1 + CONV_RC <= CONV_RC + SUBLANES
    n_t = seq // ts
    per_tile = ts // CONV_HALO
    n_halo = seq // CONV_HALO
    vec = lambda: _const_spec((None, 1, CONV_CH), lambda b, t: (layer, 0, 0))
    return pl.pallas_call(
        _conv_kernel,
        grid=(bsz, n_t),
        in_specs=[
            pl.BlockSpec((None, ts, 2 * CONV_CH), lambda b, t: (b, t, 0)),
            pl.BlockSpec((None, CONV_HALO, 2 * CONV_CH),
                         lambda b, t: (b, jnp.maximum(t * per_tile - 1, 0), 0)),
            pl.BlockSpec((None, CONV_HALO, 2 * CONV_CH),
                         lambda b, t: (b, jnp.minimum((t + 1) * per_tile, n_halo - 1), 0)),
            _const_spec((None, CONV_WIDTH, CONV_CH), lambda b, t: (layer, 0, 0)),
            vec(), vec(), vec(),
        ],
        out_specs=pl.BlockSpec((None, ts, CONV_CH), lambda b, t: (b, t, 0)),
        out_shape=jax.ShapeDtypeStruct((bsz, seq, CONV_CH), BF16),
        scratch_shapes=[pltpu.VMEM((ts + 2 * CONV_HALO, CONV_CH), F32)],
        compiler_params=pltpu.CompilerParams(
            dimension_semantics=("arbitrary", "arbitrary"), vmem_limit_bytes=VMEM_LIMIT),
        name="conv",
    )(u, u, u, dw_w, dw_b, ln_g, ln_b)


def _t5_bucket(rel):
    nb = T5_BUCKETS // 2
    max_exact = nb // 2
    ret = jnp.where(rel > 0, nb, 0)
    n = jnp.abs(rel)
    nf = jnp.maximum(n, 1).astype(jnp.float32)
    large = max_exact + (jnp.log(nf / max_exact) / math.log(T5_MAX_DIST / max_exact)
                         * (nb - max_exact)).astype(jnp.int32)
    large = jnp.minimum(large, nb - 1)
    return ret + jnp.where(n < max_exact, n, large)


def _toeplitz(g, n):
    lead = g.shape[:-1]
    flat = jnp.tile(g, (1,) * len(lead) + (n,))
    return flat[..., :n * (2 * n - 1)].reshape(lead + (n, 2 * n - 1))[..., :n]


def _diff_bias(t5_bias):
    t = DIFF_T
    assert t >= T5_MAX_DIST
    m = np.arange(2 * t)
    q_minus_k = np.where(m < t, m, m - 2 * t)
    rel = np.stack([d * t - q_minus_k for d in (-1, 0, 1)]).astype(np.int32)
    g = t5_bias[_t5_bucket(jnp.asarray(rel))]
    near = _toeplitz(g.transpose(2, 0, 1), t)
    far = t5_bias[_t5_bucket(jnp.asarray([-T5_MAX_DIST, T5_MAX_DIST], jnp.int32))]
    far = jnp.broadcast_to(far.T[:, :, None, None], (DIFF_HEADS, 2, t, t))
    return jnp.concatenate([far[:, :1], near, far[:, 1:]], axis=1) * LOG2E


def _diff_kernel(lam_ref, qt_ref, k_ref, vt_ref, bias_ref, g_ref, o_ref,
                 s_ref, smax_ref, p_ref, alpha_ref, m_ref, acc_ref, vt1_ref, *, lam_init, n_k, n_seq):
    t = DIFF_T
    n_blocks = n_seq * n_k
    n_pairs = n_blocks * n_k

    strips =[slice(c0, c0 + DIFF_STRIP) for c0 in range(0, 2 * t, DIFF_STRIP)]

    def scores(e, par, cols):
        qb, ki = e // n_k, e % n_k
        sq, qi = qb // n_k, qb % n_k
        kk = k_ref[sq, pl.ds(pl.multiple_of(ki * t, t), t), :]
        s = jnp.dot(kk, qt_ref[qb, :, cols], preferred_element_type=F32)
        q0 = cols.start % t
        s = s + bias_ref[jnp.clip(ki - qi, -2, 2) + 2, :, q0:q0 + DIFF_STRIP]
        s_ref[par, :, cols] = s
        smax_ref[par, :, cols] = jnp.max(s, axis=0, keepdims=True)

    def softmax(e, par, cols):
        m_old = jnp.where(e % n_k == 0, NEG, m_ref[:, cols])
        m_new = jnp.maximum(m_old, smax_ref[par, :, cols])
        m_ref[:, cols] = m_new
        alpha_ref[par, :, cols] = jnp.exp2(m_old - m_new)
        p_ref[par, :, cols] = jnp.exp2(s_ref[par, :, cols] - m_new).astype(BF16)

    def accumulate(e, par, cols):
        qb = e // n_k
        kb = qb // n_k * n_k + e % n_k
        pv = jnp.dot(vt1_ref[kb], p_ref[par, :, cols], preferred_element_type=F32)
        acc_ref[qb, :, cols] = alpha_ref[par, :, cols] * acc_ref[qb, :, cols] + pv

    def finalize(qb, carry):
        inv_l = 1.0 / acc_ref[qb, LANES:LANES + 1, :]
        ot = (acc_ref[qb, :LANES, :t] * inv_l[:, :t]
              - lam_ref[0] * (acc_ref[qb, :LANES, t:] * inv_l[:, t:]))
        o = _rms(ot.T, g_ref[...]) * (1.0 - lam_init)
        o_ref[qb // n_k, pl.ds(pl.multiple_of(qb % n_k * t, t), t), :] = o.astype(BF16)
        return carry

    vt1_ref[:, :LANES, :] = vt_ref[...]
    vt1_ref[:, LANES:, :] = jnp.ones((n_blocks, DIFF_ONES, t), BF16)
    acc_ref[...] = jnp.zeros(acc_ref.shape, F32)
    alpha_ref[1] = jnp.zeros(alpha_ref.shape[1:], F32)
    p_ref[1] = jnp.zeros(p_ref.shape[1:], BF16)
    for cols in strips:
        scores(0, 0, cols)

    def step(e, par):
        e_next = jnp.minimum(e + 1, n_pairs - 1)
        for cols in strips:
            accumulate(jnp.maximum(e - 1, 0), 1 - par, cols)
            softmax(e, par, cols)
            scores(e_next, 1 - par, cols)

    unroll = n_k if n_k <= DIFF_MAX_UNROLL else DIFF_UNROLL
    inline_finalize = unroll == n_k

    def steps(j, carry):
        for u in range(unroll):
            step(unroll * j + u, u % 2)
            if inline_finalize and u == 0:
                finalize(jnp.maximum(j - 1, 0), carry)
        return carry

    lax.fori_loop(0, n_pairs // unroll, steps, 0)
    for cols in strips:
        accumulate(n_pairs - 1, 1, cols)
    if inline_finalize:
        finalize(n_blocks - 1, 0)
    else:
        lax.fori_loop(0, n_blocks, finalize, 0)


def _diff(qt2, k, vt, bias, lam_full, subln_g, lam_init, layer):
    bsz, seq, _ = k.shape
    t = DIFF_T
    assert seq % t == 0 and IN_TM == t
    n_k = seq // t
    assert DIFF_UNROLL % 2 == 0 and n_k % 2 == 0
    n_seq = max(d for d in range(1, bsz + 1) if bsz % d == 0 and (d == 1 or d * n_k * n_k <= DIFF_PAIRS_PER_STEP))
    n_blocks = n_seq * n_k
    smem = pl.BlockSpec(memory_space=pltpu.SMEM)
    return pl.pallas_call(
        functools.partial(_diff_kernel, lam_init=lam_init, n_k=n_k, n_seq=n_seq),
        grid=(DIFF_HEADS, bsz // n_seq),
        in_specs=[
            smem,
            pl.BlockSpec((n_blocks, LANES, 2 * t), lambda h, b: (b, h, 0)),
            pl.BlockSpec((n_seq, seq, LANES), lambda h, b: (b, 0, h)),
            pl.BlockSpec((n_blocks, LANES, t), lambda h, b: (b, h, 0)),
            pl.BlockSpec((None, 5, t, t), lambda h, b: (h, 0, 0, 0)),
            _const_spec((None, 1, LANES), lambda h, b: (layer, 0, 0)),
        ],
        out_specs=pl.BlockSpec((n_seq, seq, LANES), lambda h, b: (b, 0, h)),
        out_shape=jax.ShapeDtypeStruct((bsz, seq, DIFF_WIDTH), BF16),
        scratch_shapes=[
            pltpu.VMEM((2, t, 2 * t), F32),
            pltpu.VMEM((2, 1, 2 * t), F32),
            pltpu.VMEM((2, t, 2 * t), BF16),
            pltpu.VMEM((2, 1, 2 * t), F32),
            pltpu.VMEM((1, 2 * t), F32),
            pltpu.VMEM((n_blocks, LANES + DIFF_ONES, 2 * t), F32),
            pltpu.VMEM((n_blocks, LANES + DIFF_ONES, t), BF16),
        ],
        compiler_params=pltpu.CompilerParams(
            dimension_semantics=("arbitrary", "arbitrary"), vmem_limit_bytes=VMEM_LIMIT),
        name="diff",
    )(lam_full, qt2, k, vt, bias, subln_g)


_FFN_CHUNKS = ((0, 768), (768, 1536), (1536, 2304), (2304, FFN_HIDDEN))


def _post_kernel(x_ref, ona_ref, ocv_ref, odf_ref, gate_ref, wb_ref, wo_ref,
                 g_mix_ref, g_pre_ref, g_post_ref, wfi_ref, wfo_ref, o_ref, act_ref):
    merged = None
    for b, br_ref in enumerate((ona_ref, ocv_ref, odf_ref)):
        proj = jnp.dot(br_ref[...], wb_ref[b], preferred_element_type=F32)
        term = gate_ref[:, b * D_MODEL:(b + 1) * D_MODEL].astype(F32) * proj
        merged = term if merged is None else merged + term
    y = jnp.dot(merged.astype(BF16), wo_ref[...], preferred_element_type=F32)
    x1 = x_ref[...] + _rms(y, g_mix_ref[...])
    hf = _rms(x1, g_pre_ref[...]).astype(BF16)
    for c0, c1 in _FFN_CHUNKS:
        gate = jnp.dot(hf, wfi_ref[:, c0:c1], preferred_element_type=F32)
        up = jnp.dot(hf, wfi_ref[:, FFN_HIDDEN + c0:FFN_HIDDEN + c1], preferred_element_type=F32)
        act_ref[:, c0:c1] = (gate * jax.nn.sigmoid(gate) * up).astype(BF16)
    z = jnp.dot(act_ref[...], wfo_ref[...], preferred_element_type=F32)
    o_ref[...] = x1 + _rms(z, g_post_ref[...])


def _post(x, o_na, o_cv, o_df, gates, w_branch, w_out, g_mix, g_pre, g_post, w_ffn_in, w_ffn_out,
          layer):
    n_tok = x.shape[0]
    tm = POST_TM
    assert n_tok % tm == 0
    vec = lambda: _const_spec((None, 1, D_MODEL), lambda i: (layer, 0, 0))
    br = lambda: pl.BlockSpec((tm, BRANCH_W), lambda i: (i, 0))
    return pl.pallas_call(
        _post_kernel,
        grid=(n_tok // tm,),
        in_specs=[
            pl.BlockSpec((tm, D_MODEL), lambda i: (i, 0)),
            br(), br(), br(),
            pl.BlockSpec((tm, N_BRANCH * D_MODEL), lambda i: (i, 0)),
            _const_spec((None, N_BRANCH, BRANCH_W, D_MODEL), lambda i: (layer, 0, 0, 0)),
            _const_spec((None, D_MODEL, D_MODEL), lambda i: (layer, 0, 0)),
            vec(), vec(), vec(),
            _const_spec((None, D_MODEL, 2 * FFN_HIDDEN), lambda i: (layer, 0, 0)),
            _const_spec((None, FFN_HIDDEN, D_MODEL), lambda i: (layer, 0, 0)),
        ],
        out_specs=pl.BlockSpec((tm, D_MODEL), lambda i: (i, 0)),
        out_shape=jax.ShapeDtypeStruct((n_tok, D_MODEL), F32),
        scratch_shapes=[pltpu.VMEM((tm, FFN_HIDDEN), BF16)],
        compiler_params=pltpu.CompilerParams(
            dimension_semantics=("arbitrary",), vmem_limit_bytes=VMEM_LIMIT),
        name="post",
    )(x, o_na, o_cv, o_df, gates, w_branch, w_out, g_mix, g_pre, g_post, w_ffn_in, w_ffn_out)


def _trunk(x, p):
    bsz, seq, _ = x.shape
    n_tok = bsz * seq
    depth = p["w_in"].shape[0]
    x = x.reshape(n_tok, D_MODEL)
    for l in range(depth):
        lam_init = 0.8 - 0.6 * math.exp(-0.3 * l)
        naq, nak, nav, cvu, dqt, dk, dvt, gates = _in_proj(x, p["ln_mix_pre"], p["w_in"], p["b_gate"], l)
        seq3 = lambda a: a.reshape(bsz, seq, a.shape[-1])
        o_na = _na(seq3(naq), seq3(nak), seq3(nav), p["na_rpb"][l], p["na_tiles"][l])
        o_cv = _conv(seq3(cvu), p["conv_dw_w"], p["conv_dw_b"], p["conv_ln_g"], p["conv_ln_b"], l)
        o_df = _diff(dqt, seq3(dk), dvt, p["diff_bias"], p["lam_full"][l], p["diff_subln_g"],
                     lam_init, l)
        flat = lambda a: a.reshape(n_tok, a.shape[-1])
        x = _post(x, flat(o_na), flat(o_cv), flat(o_df), gates, p["w_branch"], p["w_out"],
                  p["ln_mix_post"], p["ln_ffn_pre"], p["ln_ffn_post"], p["w_ffn_in"], p["w_ffn_out"], l)
    return x.reshape(bsz, seq, D_MODEL)


def _prepare(w_in, b_gate, na_rpb, conv_dw_w, conv_dw_b, conv_ln_g, conv_ln_b,
             diff_lambda, diff_subln_g, t5_bias, w_branch, w_out,
             ln_mix_pre, ln_mix_post, ln_ffn_pre, ln_ffn_post, w_ffn_in, w_ffn_out):
    depth = w_in.shape[0]
    row = lambda a: a.reshape(depth, 1, a.shape[-1])
    lf = diff_lambda.astype(F32)
    lam_init = jnp.asarray([0.8 - 0.6 * math.exp(-0.3 * l) for l in range(depth)], F32)
    lam_full = (jnp.exp(jnp.sum(lf[:, 0] * lf[:, 1], axis=-1))
                - jnp.exp(jnp.sum(lf[:, 2] * lf[:, 3], axis=-1)) + lam_init)
    return dict(
        w_in=w_in.astype(BF16), b_gate=row(b_gate), na_rpb=na_rpb, na_tiles=[{} for _ in range(depth)],
        conv_dw_w=conv_dw_w, conv_dw_b=row(conv_dw_b), conv_ln_g=row(conv_ln_g), conv_ln_b=row(conv_ln_b),
        lam_full=lam_full.reshape(depth, 1), diff_subln_g=row(diff_subln_g),
        diff_bias=_diff_bias(t5_bias),
        w_branch=w_branch.astype(BF16), w_out=w_out.astype(BF16),
        ln_mix_pre=row(ln_mix_pre), ln_mix_post=row(ln_mix_post),
        ln_ffn_pre=row(ln_ffn_pre), ln_ffn_post=row(ln_ffn_post),
        w_ffn_in=w_ffn_in.astype(BF16), w_ffn_out=w_ffn_out.astype(BF16),
    )


def kernel(x_prompt, x_sample, w_in, b_gate, na_rpb, conv_dw_w, conv_dw_b, conv_ln_g, conv_ln_b,
           diff_lambda, diff_subln_g, t5_bias, w_branch, w_out,
           ln_mix_pre, ln_mix_post, ln_ffn_pre, ln_ffn_post, w_ffn_in, w_ffn_out):
    p = _prepare(w_in, b_gate, na_rpb, conv_dw_w, conv_dw_b, conv_ln_g, conv_ln_b,
                 diff_lambda, diff_subln_g, t5_bias, w_branch, w_out,
                 ln_mix_pre, ln_mix_post, ln_ffn_pre, ln_ffn_post, w_ffn_in, w_ffn_out)
    return (_trunk(x_prompt, p), _trunk(x_sample, p))
```

```python
import functools
import math

import numpy as np
import jax
import jax.numpy as jnp
from jax import lax
from jax.experimental import pallas as pl
from jax.experimental.pallas import tpu as pltpu

F32 = jnp.float32
BF16 = jnp.bfloat16

D_MODEL = 1024
GRID_W = 64
NA_HEADS = 8
NA_HEAD_DIM = 64
NA_WIDTH = NA_HEADS * NA_HEAD_DIM
NA_WIN_ROWS_MAX = 8
NA_WIN_COLS = 16
CONV_CH = 512
CONV_WIDTH = 31
DIFF_HEADS = 4
DIFF_HEAD_DIM = 64
DIFF_WIDTH = DIFF_HEADS * 2 * DIFF_HEAD_DIM
N_BRANCH = 3
BRANCH_W = 512
T5_BUCKETS = 32
T5_MAX_DIST = 128
FFN_HIDDEN = 2816
EPS = 1e-6

OFF_CONV = 3 * NA_WIDTH
OFF_DIFF = OFF_CONV + 2 * CONV_CH
OFF_GATE = OFF_DIFF + 3 * DIFF_WIDTH
IN_COLS = OFF_GATE + N_BRANCH * D_MODEL

LANES = 128
SUBLANES = 8
VMEM_LIMIT = 56 * 1024 * 1024

IN_TM = 512
POST_TM = 512
NA_R = 4
NA_KR = 12
CONV_HALO = 16
CONV_RC = 128
DIFF_T = 512
DIFF_STRIP = 256
DIFF_ONES = 16
DIFF_UNROLL = 2
DIFF_MAX_UNROLL = 4
DIFF_PAIRS_PER_STEP = 64
NEG = -1e30
LOG2E = math.log2(math.e)

_EXACT = lax.Precision.HIGHEST


def _rms(x, g):
    return x * lax.rsqrt(jnp.mean(x * x, axis=-1, keepdims=True) + EPS) * g


def _const_spec(shape, index_map):
    return pl.BlockSpec(shape, index_map, pipeline_mode=pl.Buffered(1))


def _in_proj_kernel(x_ref, g_ref, w_ref, bg_ref,
                    naq_ref, nak_ref, nav_ref, cv_ref, dqt_ref, dk_ref, dvt_ref, gate_ref):
    h = _rms(x_ref[...], g_ref[...]).astype(BF16)

    def proj(c0, c1):
        return jnp.dot(h, w_ref[:, c0:c1], preferred_element_type=F32)

    for b in range(N_BRANCH):
        c0 = OFF_GATE + b * D_MODEL
        gate = jax.nn.sigmoid(proj(c0, c0 + D_MODEL) + bg_ref[:, b * D_MODEL:(b + 1) * D_MODEL])
        gate_ref[:, b * D_MODEL:(b + 1) * D_MODEL] = gate.astype(BF16)
    dqt = (proj(OFF_DIFF, OFF_DIFF + DIFF_WIDTH) * (DIFF_HEAD_DIM ** -0.5 * LOG2E)).T
    comp0 = lax.broadcasted_iota(jnp.int32, dqt.shape, 0) % (2 * DIFF_HEAD_DIM) < DIFF_HEAD_DIM
    dqt_ref[:, :IN_TM] = jnp.where(comp0, dqt, 0.0).astype(BF16)
    dqt_ref[:, IN_TM:] = jnp.where(comp0, 0.0, dqt).astype(BF16)
    dvt_ref[...] = proj(OFF_DIFF + 2 * DIFF_WIDTH, OFF_GATE).T.astype(BF16)
    naq_ref[...] = (proj(0, NA_WIDTH) * (NA_HEAD_DIM ** -0.5 * LOG2E)).astype(BF16)
    cv_ref[...] = proj(OFF_CONV, OFF_DIFF).astype(BF16)
    dk_ref[...] = proj(OFF_DIFF + DIFF_WIDTH, OFF_DIFF + 2 * DIFF_WIDTH).astype(BF16)
    nak_ref[...] = proj(NA_WIDTH, 2 * NA_WIDTH).astype(BF16)
    nav_ref[...] = proj(2 * NA_WIDTH, 3 * NA_WIDTH).astype(BF16)


def _in_proj(x, g_pre, w_in, b_gate, layer):
    n_tok = x.shape[0]
    tm = IN_TM
    assert n_tok % tm == 0
    n_t = n_tok // tm
    tok = lambda w: (pl.BlockSpec((tm, w), lambda i: (i, 0)), jax.ShapeDtypeStruct((n_tok, w), BF16))
    tok_t = lambda w, c: (pl.BlockSpec((None, w, c), lambda i: (i, 0, 0)),
                          jax.ShapeDtypeStruct((n_t, w, c), BF16))
    outs = [tok(NA_WIDTH), tok(NA_WIDTH), tok(NA_WIDTH), tok(2 * CONV_CH),
            tok_t(DIFF_WIDTH, 2 * tm), tok(DIFF_WIDTH), tok_t(DIFF_WIDTH, tm), tok(N_BRANCH * D_MODEL)]
    return pl.pallas_call(
        _in_proj_kernel,
        grid=(n_t,),
        in_specs=[
            pl.BlockSpec((tm, D_MODEL), lambda i: (i, 0)),
            _const_spec((None, 1, D_MODEL), lambda i: (layer, 0, 0)),
            _const_spec((None, D_MODEL, IN_COLS), lambda i: (layer, 0, 0)),
            _const_spec((None, 1, N_BRANCH * D_MODEL), lambda i: (layer, 0, 0)),
        ],
        out_specs=[o[0] for o in outs],
        out_shape=[o[1] for o in outs],
        compiler_params=pltpu.CompilerParams(
            dimension_semantics=("arbitrary",), vmem_limit_bytes=VMEM_LIMIT),
        name="in_proj",
    )(x, g_pre, w_in, b_gate)


def _na_plan(rows):
    assert rows % NA_R == 0 and rows >= NA_KR
    kr = min(NA_WIN_ROWS_MAX, rows)
    n_groups = rows // NA_R
    variants, var_idx, win_start = [], [], []
    for g in range(n_groups):
        r0 = g * NA_R
        ws = int(np.clip(r0 - kr // 2, 0, rows - NA_KR))
        ro = -np.ones((NA_R, NA_KR), np.int32)
        for i in range(NA_R):
            r = r0 + i
            rs = int(np.clip(r - kr // 2, 0, rows - kr))
            assert ws <= rs and rs + kr <= ws + NA_KR
            for a in range(NA_KR):
                if rs <= ws + a < rs + kr:
                    ro[i, a] = ws + a - r + (NA_WIN_ROWS_MAX - 1)
        for v, known in enumerate(variants):
            if np.array_equal(known, ro):
                var_idx.append(v)
                break
        else:
            var_idx.append(len(variants))
            variants.append(ro)
        win_start.append(ws)
    return np.stack(variants), np.asarray(var_idx, np.int32), np.asarray(win_start, np.int32)


def _na_bias_tiles(rpb, row_off):
    n_var = row_off.shape[0]
    n_rows = 2 * NA_WIN_ROWS_MAX - 1
    n_cols = 2 * NA_WIN_COLS - 1
    col = np.arange(GRID_W)
    col_start = np.clip(col - NA_WIN_COLS // 2, 0, GRID_W - NA_WIN_COLS)
    kc = col[None, :]
    col_ok = (kc >= col_start[:, None]) & (kc < col_start[:, None] + NA_WIN_COLS)
    col_off = kc - col[:, None] + (NA_WIN_COLS - 1)
    col_hot = (col_ok[None] & (col_off[None] == np.arange(n_cols)[:, None, None])).astype(np.float32)
    row_hot = (row_off.reshape(-1)[:, None] == np.arange(n_rows)[None, :]).astype(np.float32)
    t = jnp.einsum("xr,hrc->hxc", row_hot, rpb, precision=_EXACT)
    t = jnp.einsum("hxc,cqk->hxqk", t, col_hot, precision=_EXACT)
    ok = (row_off >= 0).reshape(-1)[:, None, None] & col_ok[None]
    t = jnp.where(ok[None], t * LOG2E, NEG).reshape(NA_HEADS, n_var, NA_R, NA_KR, GRID_W, GRID_W)
    return t.transpose(1, 0, 2, 4, 3, 5).reshape(n_var, NA_HEADS, NA_R * GRID_W, NA_KR * GRID_W)


def _glu(u):
    a = u[:, :CONV_CH].astype(F32)
    g = u[:, CONV_CH:].astype(F32)
    return a * jax.nn.sigmoid(g)


def _conv_rows(xs_ref, r0, w_ref, b_ref, lg_ref, lb_ref):
    first = CONV_HALO - CONV_WIDTH // 2
    ext = CONV_RC + SUBLANES
    groups = []
    for c0 in range(0, CONV_CH, LANES):
        slab = xs_ref[r0:r0 + CONV_RC + 2 * CONV_HALO, c0:c0 + LANES]
        acc = None
        for s in range(SUBLANES):
            part = None
            for j in range(s, CONV_WIDTH, SUBLANES):
                term = w_ref[j:j + 1, c0:c0 + LANES] * slab[j - s:j - s + ext]
                part = term if part is None else part + term
            shift = first + s
            part = part[shift:shift + CONV_RC]
            acc = part if acc is None else acc + part
        groups.append(acc)
    acc = jnp.concatenate(groups, axis=1) + b_ref[...]
    mu = jnp.mean(acc, axis=-1, keepdims=True)
    xc = acc - mu
    y = xc * lax.rsqrt(jnp.mean(xc * xc, axis=-1, keepdims=True) + EPS)
    y = y * lg_ref[...] + lb_ref[...]
    return (y * jax.nn.sigmoid(y)).astype(BF16)


def _na_kernel(var_ref, ws_ref, q_ref, k_ref, v_ref, bias_ref, u_ref, ul_ref, ur_ref, cw_ref, cb_ref, clg_ref, clb_ref,
               o_ref, ocv_ref, xs_ref):
    del var_ref
    n_q = NA_R * GRID_W
    g = pl.program_id(1)
    xs_ref[0:CONV_HALO, :] = jnp.where(g > 0, _glu(ul_ref[...]), 0.0)
    xs_ref[CONV_HALO:CONV_HALO + n_q, :] = _glu(u_ref[...])
    xs_ref[CONV_HALO + n_q:, :] = jnp.where(g < pl.num_programs(1) - 1, _glu(ur_ref[...]), 0.0)
    for rc in range(n_q // CONV_RC):
        ocv_ref[rc * CONV_RC:(rc + 1) * CONV_RC, :] = _conv_rows(xs_ref, rc * CONV_RC, cw_ref, cb_ref, clg_ref, clb_ref)
    n_k = NA_KR * GRID_W
    k0 = pl.multiple_of(ws_ref[pl.program_id(1)] * GRID_W, GRID_W)
    low_half = lax.broadcasted_iota(jnp.int32, (n_q, LANES), 1) < NA_HEAD_DIM
    for hp in range(NA_HEADS // 2):
        cols = slice(hp * LANES, (hp + 1) * LANES)
        q2 = q_ref[:, cols]
        k2 = k_ref[pl.ds(k0, n_k), cols]
        v2 = jnp.concatenate([v_ref[pl.ds(k0, n_k), cols], jnp.ones((n_k, LANES), BF16)], axis=1)
        halves = []
        for half in range(2):
            keep = low_half if half == 0 else jnp.logical_not(low_half)
            qm = jnp.where(keep, q2, jnp.zeros_like(q2))
            s = lax.dot_general(qm, k2, (((1,), (1,)), ((), ())), preferred_element_type=F32)
            s = s + bias_ref[2 * hp + half]
            p = jnp.exp2(s - jnp.max(s, axis=-1, keepdims=True))
            o = jnp.dot(p.astype(BF16), v2, preferred_element_type=F32)
            halves.append(o[:, :LANES] / o[:, LANES:])
        o_ref[:, cols] = jnp.where(low_half, halves[0], halves[1]).astype(BF16)


def _na(q, k, v, rpb, tile_cache, u, dw_w, dw_b, ln_g, ln_b, layer):
    bsz, seq, _ = q.shape
    per_tile = NA_R * GRID_W // CONV_HALO
    n_halo = seq // CONV_HALO
    cvec = lambda: _const_spec((None, 1, CONV_CH), lambda b, g, var, ws: (layer, 0, 0))
    rows = seq // GRID_W
    row_off, var_idx, win_start = _na_plan(rows)
    key = row_off.tobytes()
    if key not in tile_cache:
        tile_cache[key] = _na_bias_tiles(rpb, row_off)
    bias = tile_cache[key]
    n_q = NA_R * GRID_W
    n_k = NA_KR * GRID_W
    grid_spec = pltpu.PrefetchScalarGridSpec(
        num_scalar_prefetch=2,
        grid=(bsz, rows // NA_R),
        in_specs=[
            pl.BlockSpec((None, n_q, NA_WIDTH), lambda b, g, var, ws: (b, g, 0)),
            pl.BlockSpec((None, seq, NA_WIDTH), lambda b, g, var, ws: (b, 0, 0)),
            pl.BlockSpec((None, seq, NA_WIDTH), lambda b, g, var, ws: (b, 0, 0)),
            pl.BlockSpec((None, NA_HEADS, n_q, n_k), lambda b, g, var, ws: (var[g], 0, 0, 0)),
            pl.BlockSpec((None, n_q, 2 * CONV_CH), lambda b, g, var, ws: (b, g, 0)),
            pl.BlockSpec((None, CONV_HALO, 2 * CONV_CH), lambda b, g, var, ws: (b, jnp.maximum(g * per_tile - 1, 0), 0)),
            pl.BlockSpec((None, CONV_HALO, 2 * CONV_CH), lambda b, g, var, ws: (b, jnp.minimum((g + 1) * per_tile, n_halo - 1), 0)),
            _const_spec((None, CONV_WIDTH, CONV_CH), lambda b, g, var, ws: (layer, 0, 0)),
            cvec(), cvec(), cvec(),
        ],
        out_specs=[pl.BlockSpec((None, n_q, NA_WIDTH), lambda b, g, var, ws: (b, g, 0)),
                   pl.BlockSpec((None, n_q, CONV_CH), lambda b, g, var, ws: (b, g, 0))],
        scratch_shapes=[pltpu.VMEM((n_q + 2 * CONV_HALO, CONV_CH), F32)],
    )
    return pl.pallas_call(
        _na_kernel,
        grid_spec=grid_spec,
        out_shape=[jax.ShapeDtypeStruct((bsz, seq, NA_WIDTH), BF16), jax.ShapeDtypeStruct((bsz, seq, CONV_CH), BF16)],
        compiler_params=pltpu.CompilerParams(
            dimension_semantics=("arbitrary", "arbitrary"), vmem_limit_bytes=VMEM_LIMIT),
        name="na",
    )(jnp.asarray(var_idx), jnp.asarray(win_start), q, k, v, bias, u, u, u, dw_w, dw_b, ln_g, ln_b)


def _t5_bucket(rel):
    nb = T5_BUCKETS // 2
    max_exact = nb // 2
    ret = jnp.where(rel > 0, nb, 0)
    n = jnp.abs(rel)
    nf = jnp.maximum(n, 1).astype(jnp.float32)
    large = max_exact + (jnp.log(nf / max_exact) / math.log(T5_MAX_DIST / max_exact)
                         * (nb - max_exact)).astype(jnp.int32)
    large = jnp.minimum(large, nb - 1)
    return ret + jnp.where(n < max_exact, n, large)


def _toeplitz(g, n):
    lead = g.shape[:-1]
    flat = jnp.tile(g, (1,) * len(lead) + (n,))
    return flat[..., :n * (2 * n - 1)].reshape(lead + (n, 2 * n - 1))[..., :n]


def _diff_bias(t5_bias):
    t = DIFF_T
    assert t >= T5_MAX_DIST
    m = np.arange(2 * t)
    q_minus_k = np.where(m < t, m, m - 2 * t)
    rel = np.stack([d * t - q_minus_k for d in (-1, 0, 1)]).astype(np.int32)
    g = t5_bias[_t5_bucket(jnp.asarray(rel))]
    near = _toeplitz(g.transpose(2, 0, 1), t)
    far = t5_bias[_t5_bucket(jnp.asarray([-T5_MAX_DIST, T5_MAX_DIST], jnp.int32))]
    far = jnp.broadcast_to(far.T[:, :, None, None], (DIFF_HEADS, 2, t, t))
    return jnp.concatenate([far[:, :1], near, far[:, 1:]], axis=1) * LOG2E


def _diff_kernel(lam_ref, qt_ref, k_ref, vt_ref, bias_ref, g_ref, o_ref,
                 s_ref, smax_ref, p_ref, alpha_ref, m_ref, acc_ref, vt1_ref, *, lam_init, n_k, n_seq):
    t = DIFF_T
    n_blocks = n_seq * n_k
    n_pairs = n_blocks * n_k

    strips =[slice(c0, c0 + DIFF_STRIP) for c0 in range(0, 2 * t, DIFF_STRIP)]

    def scores(e, par, cols):
        qb, ki = e // n_k, e % n_k
        sq, qi = qb // n_k, qb % n_k
        kk = k_ref[sq, pl.ds(pl.multiple_of(ki * t, t), t), :]
        s = jnp.dot(kk, qt_ref[qb, :, cols], preferred_element_type=F32)
        q0 = cols.start % t
        s = s + bias_ref[jnp.clip(ki - qi, -2, 2) + 2, :, q0:q0 + DIFF_STRIP]
        s_ref[par, :, cols] = s
        smax_ref[par, :, cols] = jnp.max(s, axis=0, keepdims=True)

    def softmax(e, par, cols):
        m_old = jnp.where(e % n_k == 0, NEG, m_ref[:, cols])
        m_new = jnp.maximum(m_old, smax_ref[par, :, cols])
        m_ref[:, cols] = m_new
        alpha_ref[par, :, cols] = jnp.exp2(m_old - m_new)
        p_ref[par, :, cols] = jnp.exp2(s_ref[par, :, cols] - m_new).astype(BF16)

    def accumulate(e, par, cols):
        qb = e // n_k
        kb = qb // n_k * n_k + e % n_k
        pv = jnp.dot(vt1_ref[kb], p_ref[par, :, cols], preferred_element_type=F32)
        acc_ref[qb, :, cols] = alpha_ref[par, :, cols] * acc_ref[qb, :, cols] + pv

    def finalize(qb, carry):
        inv_l = 1.0 / acc_ref[qb, LANES:LANES + 1, :]
        ot = (acc_ref[qb, :LANES, :t] * inv_l[:, :t]
              - lam_ref[0] * (acc_ref[qb, :LANES, t:] * inv_l[:, t:]))
        o = _rms(ot.T, g_ref[...]) * (1.0 - lam_init)
        o_ref[qb // n_k, pl.ds(pl.multiple_of(qb % n_k * t, t), t), :] = o.astype(BF16)
        return carry

    vt1_ref[:, :LANES, :] = vt_ref[...]
    vt1_ref[:, LANES:, :] = jnp.ones((n_blocks, DIFF_ONES, t), BF16)
    acc_ref[...] = jnp.zeros(acc_ref.shape, F32)
    alpha_ref[1] = jnp.zeros(alpha_ref.shape[1:], F32)
    p_ref[1] = jnp.zeros(p_ref.shape[1:], BF16)
    for cols in strips:
        scores(0, 0, cols)

    def step(e, par):
        e_next = jnp.minimum(e + 1, n_pairs - 1)
        for cols in strips:
            accumulate(jnp.maximum(e - 1, 0), 1 - par, cols)
            softmax(e, par, cols)
            scores(e_next, 1 - par, cols)

    unroll = n_k if n_k <= DIFF_MAX_UNROLL else DIFF_UNROLL
    inline_finalize = unroll == n_k

    def steps(j, carry):
        for u in range(unroll):
            step(unroll * j + u, u % 2)
            if inline_finalize and u == 0:
                finalize(jnp.maximum(j - 1, 0), carry)
        return carry

    lax.fori_loop(0, n_pairs // unroll, steps, 0)
    for cols in strips:
        accumulate(n_pairs - 1, 1, cols)
    if inline_finalize:
        finalize(n_blocks - 1, 0)
    else:
        lax.fori_loop(0, n_blocks, finalize, 0)


def _diff(qt2, k, vt, bias, lam_full, subln_g, lam_init, layer):
    bsz, seq, _ = k.shape
    t = DIFF_T
    assert seq % t == 0 and IN_TM == t
    n_k = seq // t
    assert DIFF_UNROLL % 2 == 0 and n_k % 2 == 0
    n_seq = max(d for d in range(1, bsz + 1) if bsz % d == 0 and (d == 1 or d * n_k * n_k <= DIFF_PAIRS_PER_STEP))
    n_blocks = n_seq * n_k
    smem = pl.BlockSpec(memory_space=pltpu.SMEM)
    return pl.pallas_call(
        functools.partial(_diff_kernel, lam_init=lam_init, n_k=n_k, n_seq=n_seq),
        grid=(DIFF_HEADS, bsz // n_seq),
        in_specs=[
            smem,
            pl.BlockSpec((n_blocks, LANES, 2 * t), lambda h, b: (b, h, 0)),
            pl.BlockSpec((n_seq, seq, LANES), lambda h, b: (b, 0, h)),
            pl.BlockSpec((n_blocks, LANES, t), lambda h, b: (b, h, 0)),
            pl.BlockSpec((None, 5, t, t), lambda h, b: (h, 0, 0, 0)),
            _const_spec((None, 1, LANES), lambda h, b: (layer, 0, 0)),
        ],
        out_specs=pl.BlockSpec((n_seq, seq, LANES), lambda h, b: (b, 0, h)),
        out_shape=jax.ShapeDtypeStruct((bsz, seq, DIFF_WIDTH), BF16),
        scratch_shapes=[
            pltpu.VMEM((2, t, 2 * t), F32),
            pltpu.VMEM((2, 1, 2 * t), F32),
            pltpu.VMEM((2, t, 2 * t), BF16),
            pltpu.VMEM((2, 1, 2 * t), F32),
            pltpu.VMEM((1, 2 * t), F32),
            pltpu.VMEM((n_blocks, LANES + DIFF_ONES, 2 * t), F32),
            pltpu.VMEM((n_blocks, LANES + DIFF_ONES, t), BF16),
        ],
        compiler_params=pltpu.CompilerParams(
            dimension_semantics=("arbitrary", "arbitrary"), vmem_limit_bytes=VMEM_LIMIT),
        name="diff",
    )(lam_full, qt2, k, vt, bias, subln_g)


_FFN_CHUNKS = ((0, 768), (768, 1536), (1536, 2304), (2304, FFN_HIDDEN))


def _post_kernel(x_ref, ona_ref, ocv_ref, odf_ref, gate_ref, wb_ref, wo_ref,
                 g_mix_ref, g_pre_ref, g_post_ref, wfi_ref, wfo_ref, o_ref, act_ref):
    merged = None
    for b, br_ref in enumerate((ona_ref, ocv_ref, odf_ref)):
        proj = jnp.dot(br_ref[...], wb_ref[b], preferred_element_type=F32)
        term = gate_ref[:, b * D_MODEL:(b + 1) * D_MODEL].astype(F32) * proj
        merged = term if merged is None else merged + term
    y = jnp.dot(merged.astype(BF16), wo_ref[...], preferred_element_type=F32)
    x1 = x_ref[...] + _rms(y, g_mix_ref[...])
    hf = _rms(x1, g_pre_ref[...]).astype(BF16)
    for c0, c1 in _FFN_CHUNKS:
        gate = jnp.dot(hf, wfi_ref[:, c0:c1], preferred_element_type=F32)
        up = jnp.dot(hf, wfi_ref[:, FFN_HIDDEN + c0:FFN_HIDDEN + c1], preferred_element_type=F32)
        act_ref[:, c0:c1] = (gate * jax.nn.sigmoid(gate) * up).astype(BF16)
    z = jnp.dot(act_ref[...], wfo_ref[...], preferred_element_type=F32)
    o_ref[...] = x1 + _rms(z, g_post_ref[...])


def _post(x, o_na, o_cv, o_df, gates, w_branch, w_out, g_mix, g_pre, g_post, w_ffn_in, w_ffn_out,
          layer):
    n_tok = x.shape[0]
    tm = POST_TM
    assert n_tok % tm == 0
    vec = lambda: _const_spec((None, 1, D_MODEL), lambda i: (layer, 0, 0))
    br = lambda: pl.BlockSpec((tm, BRANCH_W), lambda i: (i, 0))
    return pl.pallas_call(
        _post_kernel,
        grid=(n_tok // tm,),
        in_specs=[
            pl.BlockSpec((tm, D_MODEL), lambda i: (i, 0)),
            br(), br(), br(),
            pl.BlockSpec((tm, N_BRANCH * D_MODEL), lambda i: (i, 0)),
            _const_spec((None, N_BRANCH, BRANCH_W, D_MODEL), lambda i: (layer, 0, 0, 0)),
            _const_spec((None, D_MODEL, D_MODEL), lambda i: (layer, 0, 0)),
            vec(), vec(), vec(),
            _const_spec((None, D_MODEL, 2 * FFN_HIDDEN), lambda i: (layer, 0, 0)),
            _const_spec((None, FFN_HIDDEN, D_MODEL), lambda i: (layer, 0, 0)),
        ],
        out_specs=pl.BlockSpec((tm, D_MODEL), lambda i: (i, 0)),
        out_shape=jax.ShapeDtypeStruct((n_tok, D_MODEL), F32),
        scratch_shapes=[pltpu.VMEM((tm, FFN_HIDDEN), BF16)],
        compiler_params=pltpu.CompilerParams(
            dimension_semantics=("arbitrary",), vmem_limit_bytes=VMEM_LIMIT),
        name="post",
    )(x, o_na, o_cv, o_df, gates, w_branch, w_out, g_mix, g_pre, g_post, w_ffn_in, w_ffn_out)


def _trunk(x, p):
    bsz, seq, _ = x.shape
    n_tok = bsz * seq
    depth = p["w_in"].shape[0]
    x = x.reshape(n_tok, D_MODEL)
    for l in range(depth):
        lam_init = 0.8 - 0.6 * math.exp(-0.3 * l)
        naq, nak, nav, cvu, dqt, dk, dvt, gates = _in_proj(x, p["ln_mix_pre"], p["w_in"], p["b_gate"], l)
        seq3 = lambda a: a.reshape(bsz, seq, a.shape[-1])
        o_na, o_cv = _na(seq3(naq), seq3(nak), seq3(nav), p["na_rpb"][l], p["na_tiles"][l],
                         seq3(cvu), p["conv_dw_w"], p["conv_dw_b"], p["conv_ln_g"], p["conv_ln_b"], l)
        o_df = _diff(dqt, seq3(dk), dvt, p["diff_bias"], p["lam_full"][l], p["diff_subln_g"],
                     lam_init, l)
        flat = lambda a: a.reshape(n_tok, a.shape[-1])
        x = _post(x, flat(o_na), flat(o_cv), flat(o_df), gates, p["w_branch"], p["w_out"],
                  p["ln_mix_post"], p["ln_ffn_pre"], p["ln_ffn_post"], p["w_ffn_in"], p["w_ffn_out"], l)
    return x.reshape(bsz, seq, D_MODEL)


def _prepare(w_in, b_gate, na_rpb, conv_dw_w, conv_dw_b, conv_ln_g, conv_ln_b,
             diff_lambda, diff_subln_g, t5_bias, w_branch, w_out,
             ln_mix_pre, ln_mix_post, ln_ffn_pre, ln_ffn_post, w_ffn_in, w_ffn_out):
    depth = w_in.shape[0]
    row = lambda a: a.reshape(depth, 1, a.shape[-1])
    lf = diff_lambda.astype(F32)
    lam_init = jnp.asarray([0.8 - 0.6 * math.exp(-0.3 * l) for l in range(depth)], F32)
    lam_full = (jnp.exp(jnp.sum(lf[:, 0] * lf[:, 1], axis=-1))
                - jnp.exp(jnp.sum(lf[:, 2] * lf[:, 3], axis=-1)) + lam_init)
    return dict(
        w_in=w_in.astype(BF16), b_gate=row(b_gate), na_rpb=na_rpb, na_tiles=[{} for _ in range(depth)],
        conv_dw_w=conv_dw_w, conv_dw_b=row(conv_dw_b), conv_ln_g=row(conv_ln_g), conv_ln_b=row(conv_ln_b),
        lam_full=lam_full.reshape(depth, 1), diff_subln_g=row(diff_subln_g),
        diff_bias=_diff_bias(t5_bias),
        w_branch=w_branch.astype(BF16), w_out=w_out.astype(BF16),
        ln_mix_pre=row(ln_mix_pre), ln_mix_post=row(ln_mix_post),
        ln_ffn_pre=row(ln_ffn_pre), ln_ffn_post=row(ln_ffn_post),
        w_ffn_in=w_ffn_in.astype(BF16), w_ffn_out=w_ffn_out.astype(BF16),
    )


def kernel(x_prompt, x_sample, w_in, b_gate, na_rpb, conv_dw_w, conv_dw_b, conv_ln_g, conv_ln_b,
           diff_lambda, diff_subln_g, t5_bias, w_branch, w_out,
           ln_mix_pre, ln_mix_post, ln_ffn_pre, ln_ffn_post, w_ffn_in, w_ffn_out):
    p = _prepare(w_in, b_gate, na_rpb, conv_dw_w, conv_dw_b, conv_ln_g, conv_ln_b,
                 diff_lambda, diff_subln_g, t5_bias, w_branch, w_out,
                 ln_mix_pre, ln_mix_post, ln_ffn_pre, ln_ffn_post, w_ffn_in, w_ffn_out)
    return (_trunk(x_prompt, p), _trunk(x_sample, p))
```

```python
import functools
import math

import numpy as np
import jax
import jax.numpy as jnp
from jax import lax
from jax.experimental import pallas as pl
from jax.experimental.pallas import tpu as pltpu

F32 = jnp.float32
BF16 = jnp.bfloat16

D_MODEL = 1024
GRID_W = 64
NA_HEADS = 8
NA_HEAD_DIM = 64
NA_WIDTH = NA_HEADS * NA_HEAD_DIM
NA_WIN_ROWS_MAX = 8
NA_WIN_COLS = 16
CONV_CH = 512
CONV_WIDTH = 31
DIFF_HEADS = 4
DIFF_HEAD_DIM = 64
DIFF_WIDTH = DIFF_HEADS * 2 * DIFF_HEAD_DIM
N_BRANCH = 3
BRANCH_W = 512
T5_BUCKETS = 32
T5_MAX_DIST = 128
FFN_HIDDEN = 2816
EPS = 1e-6

OFF_CONV = 3 * NA_WIDTH
OFF_DIFF = OFF_CONV + 2 * CONV_CH
OFF_GATE = OFF_DIFF + 3 * DIFF_WIDTH
IN_COLS = OFF_GATE + N_BRANCH * D_MODEL

LANES = 128
SUBLANES = 8
VMEM_LIMIT = 56 * 1024 * 1024

IN_TM = 512
POST_TM = 512
NA_R = 4
NA_KR = 12
CONV_HALO = 16
CONV_RC = 128
DIFF_T = 512
DIFF_STRIP = 256
DIFF_ONES = 16
DIFF_UNROLL = 2
DIFF_MAX_UNROLL = 4
DIFF_PAIRS_PER_STEP = 64
NEG = -1e30
LOG2E = math.log2(math.e)

_EXACT = lax.Precision.HIGHEST


def _rms(x, g):
    return x * lax.rsqrt(jnp.mean(x * x, axis=-1, keepdims=True) + EPS) * g


def _const_spec(shape, index_map):
    return pl.BlockSpec(shape, index_map, pipeline_mode=pl.Buffered(1))


def _in_proj_kernel(x_ref, g_ref, w_ref, bg_ref,
                    naq_ref, nak_ref, nav_ref, cv_ref, dqt_ref, dk_ref, dvt_ref, gate_ref):
    h = _rms(x_ref[...], g_ref[...]).astype(BF16)

    def proj(c0, c1):
        return jnp.dot(h, w_ref[:, c0:c1], preferred_element_type=F32)

    for b in range(N_BRANCH):
        c0 = OFF_GATE + b * D_MODEL
        gate = jax.nn.sigmoid(proj(c0, c0 + D_MODEL) + bg_ref[:, b * D_MODEL:(b + 1) * D_MODEL])
        gate_ref[:, b * D_MODEL:(b + 1) * D_MODEL] = gate.astype(BF16)
    dqt = (proj(OFF_DIFF, OFF_DIFF + DIFF_WIDTH) * (DIFF_HEAD_DIM ** -0.5 * LOG2E)).T
    comp0 = lax.broadcasted_iota(jnp.int32, dqt.shape, 0) % (2 * DIFF_HEAD_DIM) < DIFF_HEAD_DIM
    dqt_ref[:, :IN_TM] = jnp.where(comp0, dqt, 0.0).astype(BF16)
    dqt_ref[:, IN_TM:] = jnp.where(comp0, 0.0, dqt).astype(BF16)
    dvt_ref[...] = proj(OFF_DIFF + 2 * DIFF_WIDTH, OFF_GATE).T.astype(BF16)
    naq_ref[...] = (proj(0, NA_WIDTH) * (NA_HEAD_DIM ** -0.5 * LOG2E)).astype(BF16)
    u = proj(OFF_CONV, OFF_DIFF)
    cv_ref[...] = (u[:, :CONV_CH] * jax.nn.sigmoid(u[:, CONV_CH:])).astype(BF16)
    dk_ref[...] = proj(OFF_DIFF + DIFF_WIDTH, OFF_DIFF + 2 * DIFF_WIDTH).astype(BF16)
    nak_ref[...] = proj(NA_WIDTH, 2 * NA_WIDTH).astype(BF16)
    nav_ref[...] = proj(2 * NA_WIDTH, 3 * NA_WIDTH).astype(BF16)


def _in_proj(x, g_pre, w_in, b_gate, layer):
    n_tok = x.shape[0]
    tm = IN_TM
    assert n_tok % tm == 0
    n_t = n_tok // tm
    tok = lambda w: (pl.BlockSpec((tm, w), lambda i: (i, 0)), jax.ShapeDtypeStruct((n_tok, w), BF16))
    tok_t = lambda w, c: (pl.BlockSpec((None, w, c), lambda i: (i, 0, 0)),
                          jax.ShapeDtypeStruct((n_t, w, c), BF16))
    outs = [tok(NA_WIDTH), tok(NA_WIDTH), tok(NA_WIDTH), tok(CONV_CH),
            tok_t(DIFF_WIDTH, 2 * tm), tok(DIFF_WIDTH), tok_t(DIFF_WIDTH, tm), tok(N_BRANCH * D_MODEL)]
    return pl.pallas_call(
        _in_proj_kernel,
        grid=(n_t,),
        in_specs=[
            pl.BlockSpec((tm, D_MODEL), lambda i: (i, 0)),
            _const_spec((None, 1, D_MODEL), lambda i: (layer, 0, 0)),
            _const_spec((None, D_MODEL, IN_COLS), lambda i: (layer, 0, 0)),
            _const_spec((None, 1, N_BRANCH * D_MODEL), lambda i: (layer, 0, 0)),
        ],
        out_specs=[o[0] for o in outs],
        out_shape=[o[1] for o in outs],
        compiler_params=pltpu.CompilerParams(
            dimension_semantics=("arbitrary",), vmem_limit_bytes=VMEM_LIMIT),
        name="in_proj",
    )(x, g_pre, w_in, b_gate)


def _na_plan(rows):
    assert rows % NA_R == 0 and rows >= NA_KR
    kr = min(NA_WIN_ROWS_MAX, rows)
    n_groups = rows // NA_R
    variants, var_idx, win_start = [], [], []
    for g in range(n_groups):
        r0 = g * NA_R
        ws = int(np.clip(r0 - kr // 2, 0, rows - NA_KR))
        ro = -np.ones((NA_R, NA_KR), np.int32)
        for i in range(NA_R):
            r = r0 + i
            rs = int(np.clip(r - kr // 2, 0, rows - kr))
            assert ws <= rs and rs + kr <= ws + NA_KR
            for a in range(NA_KR):
                if rs <= ws + a < rs + kr:
                    ro[i, a] = ws + a - r + (NA_WIN_ROWS_MAX - 1)
        for v, known in enumerate(variants):
            if np.array_equal(known, ro):
                var_idx.append(v)
                break
        else:
            var_idx.append(len(variants))
            variants.append(ro)
        win_start.append(ws)
    return np.stack(variants), np.asarray(var_idx, np.int32), np.asarray(win_start, np.int32)


def _na_bias_tiles(rpb, row_off):
    n_var = row_off.shape[0]
    n_rows = 2 * NA_WIN_ROWS_MAX - 1
    n_cols = 2 * NA_WIN_COLS - 1
    col = np.arange(GRID_W)
    col_start = np.clip(col - NA_WIN_COLS // 2, 0, GRID_W - NA_WIN_COLS)
    kc = col[None, :]
    col_ok = (kc >= col_start[:, None]) & (kc < col_start[:, None] + NA_WIN_COLS)
    col_off = kc - col[:, None] + (NA_WIN_COLS - 1)
    col_hot = (col_ok[None] & (col_off[None] == np.arange(n_cols)[:, None, None])).astype(np.float32)
    row_hot = (row_off.reshape(-1)[:, None] == np.arange(n_rows)[None, :]).astype(np.float32)
    t = jnp.einsum("xr,hrc->hxc", row_hot, rpb, precision=_EXACT)
    t = jnp.einsum("hxc,cqk->hxqk", t, col_hot, precision=_EXACT)
    ok = (row_off >= 0).reshape(-1)[:, None, None] & col_ok[None]
    t = jnp.where(ok[None], t * LOG2E, NEG).reshape(NA_HEADS, n_var, NA_R, NA_KR, GRID_W, GRID_W)
    return t.transpose(1, 0, 2, 4, 3, 5).reshape(n_var, NA_HEADS, NA_R * GRID_W, NA_KR * GRID_W)


def _conv_rows(xs_ref, r0, w_ref, b_ref, lg_ref, lb_ref):
    first = CONV_HALO - CONV_WIDTH // 2
    ext = CONV_RC + SUBLANES
    groups = []
    for c0 in range(0, CONV_CH, LANES):
        slab = xs_ref[r0:r0 + CONV_RC + 2 * CONV_HALO, c0:c0 + LANES]
        acc = None
        for s in range(SUBLANES):
            part = None
            for j in range(s, CONV_WIDTH, SUBLANES):
                term = w_ref[j:j + 1, c0:c0 + LANES] * slab[j - s:j - s + ext]
                part = term if part is None else part + term
            shift = first + s
            part = part[shift:shift + CONV_RC]
            acc = part if acc is None else acc + part
        groups.append(acc)
    acc = jnp.concatenate(groups, axis=1) + b_ref[...]
    mu = jnp.mean(acc, axis=-1, keepdims=True)
    xc = acc - mu
    y = xc * lax.rsqrt(jnp.mean(xc * xc, axis=-1, keepdims=True) + EPS)
    y = y * lg_ref[...] + lb_ref[...]
    return (y * jax.nn.sigmoid(y)).astype(BF16)


def _na_kernel(var_ref, ws_ref, q_ref, k_ref, v_ref, bias_ref, u_ref, ul_ref, ur_ref, cw_ref, cb_ref, clg_ref, clb_ref,
               o_ref, ocv_ref, xs_ref):
    del var_ref
    n_q = NA_R * GRID_W
    g = pl.program_id(1)
    xs_ref[0:CONV_HALO, :] = jnp.where(g > 0, ul_ref[...].astype(F32), 0.0)
    xs_ref[CONV_HALO:CONV_HALO + n_q, :] = u_ref[...].astype(F32)
    xs_ref[CONV_HALO + n_q:, :] = jnp.where(g < pl.num_programs(1) - 1, ur_ref[...].astype(F32), 0.0)
    for rc in range(n_q // CONV_RC):
        ocv_ref[rc * CONV_RC:(rc + 1) * CONV_RC, :] = _conv_rows(xs_ref, rc * CONV_RC, cw_ref, cb_ref, clg_ref, clb_ref)
    n_k = NA_KR * GRID_W
    k0 = pl.multiple_of(ws_ref[pl.program_id(1)] * GRID_W, GRID_W)
    low_half = lax.broadcasted_iota(jnp.int32, (n_q, LANES), 1) < NA_HEAD_DIM
    for hp in range(NA_HEADS // 2):
        cols = slice(hp * LANES, (hp + 1) * LANES)
        q2 = q_ref[:, cols]
        k2 = k_ref[pl.ds(k0, n_k), cols]
        v2 = jnp.concatenate([v_ref[pl.ds(k0, n_k), cols], jnp.ones((n_k, LANES), BF16)], axis=1)
        halves = []
        for half in range(2):
            keep = low_half if half == 0 else jnp.logical_not(low_half)
            qm = jnp.where(keep, q2, jnp.zeros_like(q2))
            s = lax.dot_general(qm, k2, (((1,), (1,)), ((), ())), preferred_element_type=F32)
            s = s + bias_ref[2 * hp + half]
            p = jnp.exp2(s - jnp.max(s, axis=-1, keepdims=True))
            o = jnp.dot(p.astype(BF16), v2, preferred_element_type=F32)
            halves.append(o[:, :LANES] / o[:, LANES:])
        o_ref[:, cols] = jnp.where(low_half, halves[0], halves[1]).astype(BF16)


def _na(q, k, v, rpb, tile_cache, u, dw_w, dw_b, ln_g, ln_b, layer):
    bsz, seq, _ = q.shape
    per_tile = NA_R * GRID_W // CONV_HALO
    n_halo = seq // CONV_HALO
    cvec = lambda: _const_spec((None, 1, CONV_CH), lambda b, g, var, ws: (layer, 0, 0))
    rows = seq // GRID_W
    row_off, var_idx, win_start = _na_plan(rows)
    key = row_off.tobytes()
    if key not in tile_cache:
        tile_cache[key] = _na_bias_tiles(rpb, row_off)
    bias = tile_cache[key]
    n_q = NA_R * GRID_W
    n_k = NA_KR * GRID_W
    grid_spec = pltpu.PrefetchScalarGridSpec(
        num_scalar_prefetch=2,
        grid=(bsz, rows // NA_R),
        in_specs=[
            pl.BlockSpec((None, n_q, NA_WIDTH), lambda b, g, var, ws: (b, g, 0)),
            pl.BlockSpec((None, seq, NA_WIDTH), lambda b, g, var, ws: (b, 0, 0)),
            pl.BlockSpec((None, seq, NA_WIDTH), lambda b, g, var, ws: (b, 0, 0)),
            pl.BlockSpec((None, NA_HEADS, n_q, n_k), lambda b, g, var, ws: (var[g], 0, 0, 0)),
            pl.BlockSpec((None, n_q, CONV_CH), lambda b, g, var, ws: (b, g, 0)),
            pl.BlockSpec((None, CONV_HALO, CONV_CH), lambda b, g, var, ws: (b, jnp.maximum(g * per_tile - 1, 0), 0)),
            pl.BlockSpec((None, CONV_HALO, CONV_CH), lambda b, g, var, ws: (b, jnp.minimum((g + 1) * per_tile, n_halo - 1), 0)),
            _const_spec((None, CONV_WIDTH, CONV_CH), lambda b, g, var, ws: (layer, 0, 0)),
            cvec(), cvec(), cvec(),
        ],
        out_specs=[pl.BlockSpec((None, n_q, NA_WIDTH), lambda b, g, var, ws: (b, g, 0)),
                   pl.BlockSpec((None, n_q, CONV_CH), lambda b, g, var, ws: (b, g, 0))],
        scratch_shapes=[pltpu.VMEM((n_q + 2 * CONV_HALO, CONV_CH), F32)],
    )
    return pl.pallas_call(
        _na_kernel,
        grid_spec=grid_spec,
        out_shape=[jax.ShapeDtypeStruct((bsz, seq, NA_WIDTH), BF16), jax.ShapeDtypeStruct((bsz, seq, CONV_CH), BF16)],
        compiler_params=pltpu.CompilerParams(
            dimension_semantics=("arbitrary", "arbitrary"), vmem_limit_bytes=VMEM_LIMIT),
        name="na",
    )(jnp.asarray(var_idx), jnp.asarray(win_start), q, k, v, bias, u, u, u, dw_w, dw_b, ln_g, ln_b)


def _t5_bucket(rel):
    nb = T5_BUCKETS // 2
    max_exact = nb // 2
    ret = jnp.where(rel > 0, nb, 0)
    n = jnp.abs(rel)
    nf = jnp.maximum(n, 1).astype(jnp.float32)
    large = max_exact + (jnp.log(nf / max_exact) / math.log(T5_MAX_DIST / max_exact)
                         * (nb - max_exact)).astype(jnp.int32)
    large = jnp.minimum(large, nb - 1)
    return ret + jnp.where(n < max_exact, n, large)


def _toeplitz(g, n):
    lead = g.shape[:-1]
    flat = jnp.tile(g, (1,) * len(lead) + (n,))
    return flat[..., :n * (2 * n - 1)].reshape(lead + (n, 2 * n - 1))[..., :n]


def _diff_bias(t5_bias):
    t = DIFF_T
    assert t >= T5_MAX_DIST
    m = np.arange(2 * t)
    q_minus_k = np.where(m < t, m, m - 2 * t)
    rel = np.stack([d * t - q_minus_k for d in (-1, 0, 1)]).astype(np.int32)
    g = t5_bias[_t5_bucket(jnp.asarray(rel))]
    near = _toeplitz(g.transpose(2, 0, 1), t)
    far = t5_bias[_t5_bucket(jnp.asarray([-T5_MAX_DIST, T5_MAX_DIST], jnp.int32))]
    far = jnp.broadcast_to(far.T[:, :, None, None], (DIFF_HEADS, 2, t, t))
    return jnp.concatenate([far[:, :1], near, far[:, 1:]], axis=1) * LOG2E


def _diff_kernel(lam_ref, qt_ref, k_ref, vt_ref, bias_ref, g_ref, o_ref,
                 s_ref, smax_ref, p_ref, alpha_ref, m_ref, acc_ref, vt1_ref, *, lam_init, n_k, n_seq):
    t = DIFF_T
    n_blocks = n_seq * n_k
    n_pairs = n_blocks * n_k

    strips =[slice(c0, c0 + DIFF_STRIP) for c0 in range(0, 2 * t, DIFF_STRIP)]

    def scores(e, par, cols):
        qb, ki = e // n_k, e % n_k
        sq, qi = qb // n_k, qb % n_k
        kk = k_ref[sq, pl.ds(pl.multiple_of(ki * t, t), t), :]
        s = jnp.dot(kk, qt_ref[qb, :, cols], preferred_element_type=F32)
        q0 = cols.start % t
        s = s + bias_ref[jnp.clip(ki - qi, -2, 2) + 2, :, q0:q0 + DIFF_STRIP]
        s_ref[par, :, cols] = s
        smax_ref[par, :, cols] = jnp.max(s, axis=0, keepdims=True)

    def softmax(e, par, cols):
        m_old = jnp.where(e % n_k == 0, NEG, m_ref[:, cols])
        m_new = jnp.maximum(m_old, smax_ref[par, :, cols])
        m_ref[:, cols] = m_new
        alpha_ref[par, :, cols] = jnp.exp2(m_old - m_new)
        p_ref[par, :, cols] = jnp.exp2(s_ref[par, :, cols] - m_new).astype(BF16)

    def accumulate(e, par, cols):
        qb = e // n_k
        kb = qb // n_k * n_k + e % n_k
        pv = jnp.dot(vt1_ref[kb], p_ref[par, :, cols], preferred_element_type=F32)
        acc_ref[qb, :, cols] = alpha_ref[par, :, cols] * acc_ref[qb, :, cols] + pv

    def finalize(qb, carry):
        inv_l = 1.0 / acc_ref[qb, LANES:LANES + 1, :]
        ot = (acc_ref[qb, :LANES, :t] * inv_l[:, :t]
              - lam_ref[0] * (acc_ref[qb, :LANES, t:] * inv_l[:, t:]))
        o = _rms(ot.T, g_ref[...]) * (1.0 - lam_init)
        o_ref[qb // n_k, pl.ds(pl.multiple_of(qb % n_k * t, t), t), :] = o.astype(BF16)
        return carry

    vt1_ref[:, :LANES, :] = vt_ref[...]
    vt1_ref[:, LANES:, :] = jnp.ones((n_blocks, DIFF_ONES, t), BF16)
    acc_ref[...] = jnp.zeros(acc_ref.shape, F32)
    alpha_ref[1] = jnp.zeros(alpha_ref.shape[1:], F32)
    p_ref[1] = jnp.zeros(p_ref.shape[1:], BF16)
    for cols in strips:
        scores(0, 0, cols)

    def step(e, par):
        e_next = jnp.minimum(e + 1, n_pairs - 1)
        for cols in strips:
            accumulate(jnp.maximum(e - 1, 0), 1 - par, cols)
            softmax(e, par, cols)
            scores(e_next, 1 - par, cols)

    unroll = n_k if n_k <= DIFF_MAX_UNROLL else DIFF_UNROLL
    inline_finalize = unroll == n_k

    def steps(j, carry):
        for u in range(unroll):
            step(unroll * j + u, u % 2)
            if inline_finalize and u == 0:
                finalize(jnp.maximum(j - 1, 0), carry)
        return carry

    lax.fori_loop(0, n_pairs // unroll, steps, 0)
    for cols in strips:
        accumulate(n_pairs - 1, 1, cols)
    if inline_finalize:
        finalize(n_blocks - 1, 0)
    else:
        lax.fori_loop(0, n_blocks, finalize, 0)


def _diff(qt2, k, vt, bias, lam_full, subln_g, lam_init, layer):
    bsz, seq, _ = k.shape
    t = DIFF_T
    assert seq % t == 0 and IN_TM == t
    n_k = seq // t
    assert DIFF_UNROLL % 2 == 0 and n_k % 2 == 0
    n_seq = max(d for d in range(1, bsz + 1) if bsz % d == 0 and (d == 1 or d * n_k * n_k <= DIFF_PAIRS_PER_STEP))
    n_blocks = n_seq * n_k
    smem = pl.BlockSpec(memory_space=pltpu.SMEM)
    return pl.pallas_call(
        functools.partial(_diff_kernel, lam_init=lam_init, n_k=n_k, n_seq=n_seq),
        grid=(DIFF_HEADS, bsz // n_seq),
        in_specs=[
            smem,
            pl.BlockSpec((n_blocks, LANES, 2 * t), lambda h, b: (b, h, 0)),
            pl.BlockSpec((n_seq, seq, LANES), lambda h, b: (b, 0, h)),
            pl.BlockSpec((n_blocks, LANES, t), lambda h, b: (b, h, 0)),
            pl.BlockSpec((None, 5, t, t), lambda h, b: (h, 0, 0, 0)),
            _const_spec((None, 1, LANES), lambda h, b: (layer, 0, 0)),
        ],
        out_specs=pl.BlockSpec((n_seq, seq, LANES), lambda h, b: (b, 0, h)),
        out_shape=jax.ShapeDtypeStruct((bsz, seq, DIFF_WIDTH), BF16),
        scratch_shapes=[
            pltpu.VMEM((2, t, 2 * t), F32),
            pltpu.VMEM((2, 1, 2 * t), F32),
            pltpu.VMEM((2, t, 2 * t), BF16),
            pltpu.VMEM((2, 1, 2 * t), F32),
            pltpu.VMEM((1, 2 * t), F32),
            pltpu.VMEM((n_blocks, LANES + DIFF_ONES, 2 * t), F32),
            pltpu.VMEM((n_blocks, LANES + DIFF_ONES, t), BF16),
        ],
        compiler_params=pltpu.CompilerParams(
            dimension_semantics=("arbitrary", "arbitrary"), vmem_limit_bytes=VMEM_LIMIT),
        name="diff",
    )(lam_full, qt2, k, vt, bias, subln_g)


_FFN_CHUNKS = ((0, 768), (768, 1536), (1536, 2304), (2304, FFN_HIDDEN))


def _post_kernel(x_ref, ona_ref, ocv_ref, odf_ref, gate_ref, wb_ref, wo_ref,
                 g_mix_ref, g_pre_ref, g_post_ref, wfi_ref, wfo_ref, o_ref, act_ref):
    merged = None
    for b, br_ref in enumerate((ona_ref, ocv_ref, odf_ref)):
        proj = jnp.dot(br_ref[...], wb_ref[b], preferred_element_type=F32)
        term = gate_ref[:, b * D_MODEL:(b + 1) * D_MODEL].astype(F32) * proj
        merged = term if merged is None else merged + term
    y = jnp.dot(merged.astype(BF16), wo_ref[...], preferred_element_type=F32)
    x1 = x_ref[...] + _rms(y, g_mix_ref[...])
    hf = _rms(x1, g_pre_ref[...]).astype(BF16)
    for c0, c1 in _FFN_CHUNKS:
        gate = jnp.dot(hf, wfi_ref[:, c0:c1], preferred_element_type=F32)
        up = jnp.dot(hf, wfi_ref[:, FFN_HIDDEN + c0:FFN_HIDDEN + c1], preferred_element_type=F32)
        act_ref[:, c0:c1] = (gate * jax.nn.sigmoid(gate) * up).astype(BF16)
    z = jnp.dot(act_ref[...], wfo_ref[...], preferred_element_type=F32)
    o_ref[...] = x1 + _rms(z, g_post_ref[...])


def _post(x, o_na, o_cv, o_df, gates, w_branch, w_out, g_mix, g_pre, g_post, w_ffn_in, w_ffn_out,
          layer):
    n_tok = x.shape[0]
    tm = POST_TM
    assert n_tok % tm == 0
    vec = lambda: _const_spec((None, 1, D_MODEL), lambda i: (layer, 0, 0))
    br = lambda: pl.BlockSpec((tm, BRANCH_W), lambda i: (i, 0))
    return pl.pallas_call(
        _post_kernel,
        grid=(n_tok // tm,),
        in_specs=[
            pl.BlockSpec((tm, D_MODEL), lambda i: (i, 0)),
            br(), br(), br(),
            pl.BlockSpec((tm, N_BRANCH * D_MODEL), lambda i: (i, 0)),
            _const_spec((None, N_BRANCH, BRANCH_W, D_MODEL), lambda i: (layer, 0, 0, 0)),
            _const_spec((None, D_MODEL, D_MODEL), lambda i: (layer, 0, 0)),
            vec(), vec(), vec(),
            _const_spec((None, D_MODEL, 2 * FFN_HIDDEN), lambda i: (layer, 0, 0)),
            _const_spec((None, FFN_HIDDEN, D_MODEL), lambda i: (layer, 0, 0)),
        ],
        out_specs=pl.BlockSpec((tm, D_MODEL), lambda i: (i, 0)),
        out_shape=jax.ShapeDtypeStruct((n_tok, D_MODEL), F32),
        scratch_shapes=[pltpu.VMEM((tm, FFN_HIDDEN), BF16)],
        compiler_params=pltpu.CompilerParams(
            dimension_semantics=("arbitrary",), vmem_limit_bytes=VMEM_LIMIT),
        name="post",
    )(x, o_na, o_cv, o_df, gates, w_branch, w_out, g_mix, g_pre, g_post, w_ffn_in, w_ffn_out)


def _trunk(x, p):
    bsz, seq, _ = x.shape
    n_tok = bsz * seq
    depth = p["w_in"].shape[0]
    x = x.reshape(n_tok, D_MODEL)
    for l in range(depth):
        lam_init = 0.8 - 0.6 * math.exp(-0.3 * l)
        naq, nak, nav, cvu, dqt, dk, dvt, gates = _in_proj(x, p["ln_mix_pre"], p["w_in"], p["b_gate"], l)
        seq3 = lambda a: a.reshape(bsz, seq, a.shape[-1])
        o_na, o_cv = _na(seq3(naq), seq3(nak), seq3(nav), p["na_rpb"][l], p["na_tiles"][l],
                         seq3(cvu), p["conv_dw_w"], p["conv_dw_b"], p["conv_ln_g"], p["conv_ln_b"], l)
        o_df = _diff(dqt, seq3(dk), dvt, p["diff_bias"], p["lam_full"][l], p["diff_subln_g"],
                     lam_init, l)
        flat = lambda a: a.reshape(n_tok, a.shape[-1])
        x = _post(x, flat(o_na), flat(o_cv), flat(o_df), gates, p["w_branch"], p["w_out"],
                  p["ln_mix_post"], p["ln_ffn_pre"], p["ln_ffn_post"], p["w_ffn_in"], p["w_ffn_out"], l)
    return x.reshape(bsz, seq, D_MODEL)


def _prepare(w_in, b_gate, na_rpb, conv_dw_w, conv_dw_b, conv_ln_g, conv_ln_b,
             diff_lambda, diff_subln_g, t5_bias, w_branch, w_out,
             ln_mix_pre, ln_mix_post, ln_ffn_pre, ln_ffn_post, w_ffn_in, w_ffn_out):
    depth = w_in.shape[0]
    row = lambda a: a.reshape(depth, 1, a.shape[-1])
    lf = diff_lambda.astype(F32)
    lam_init = jnp.asarray([0.8 - 0.6 * math.exp(-0.3 * l) for l in range(depth)], F32)
    lam_full = (jnp.exp(jnp.sum(lf[:, 0] * lf[:, 1], axis=-1))
                - jnp.exp(jnp.sum(lf[:, 2] * lf[:, 3], axis=-1)) + lam_init)
    return dict(
        w_in=w_in.astype(BF16), b_gate=row(b_gate), na_rpb=na_rpb, na_tiles=[{} for _ in range(depth)],
        conv_dw_w=conv_dw_w, conv_dw_b=row(conv_dw_b), conv_ln_g=row(conv_ln_g), conv_ln_b=row(conv_ln_b),
        lam_full=lam_full.reshape(depth, 1), diff_subln_g=row(diff_subln_g),
        diff_bias=_diff_bias(t5_bias),
        w_branch=w_branch.astype(BF16), w_out=w_out.astype(BF16),
        ln_mix_pre=row(ln_mix_pre), ln_mix_post=row(ln_mix_post),
        ln_ffn_pre=row(ln_ffn_pre), ln_ffn_post=row(ln_ffn_post),
        w_ffn_in=w_ffn_in.astype(BF16), w_ffn_out=w_ffn_out.astype(BF16),
    )


def kernel(x_prompt, x_sample, w_in, b_gate, na_rpb, conv_dw_w, conv_dw_b, conv_ln_g, conv_ln_b,
           diff_lambda, diff_subln_g, t5_bias, w_branch, w_out,
           ln_mix_pre, ln_mix_post, ln_ffn_pre, ln_ffn_post, w_ffn_in, w_ffn_out):
    p = _prepare(w_in, b_gate, na_rpb, conv_dw_w, conv_dw_b, conv_ln_g, conv_ln_b,
                 diff_lambda, diff_subln_g, t5_bias, w_branch, w_out,
                 ln_mix_pre, ln_mix_post, ln_ffn_pre, ln_ffn_post, w_ffn_in, w_ffn_out)
    return (_trunk(x_prompt, p), _trunk(x_sample, p))
```

```python
import functools
import math

import numpy as np
import jax
import jax.numpy as jnp
from jax import lax
from jax.experimental import pallas as pl
from jax.experimental.pallas import tpu as pltpu

F32 = jnp.float32
BF16 = jnp.bfloat16

D_MODEL = 1024
GRID_W = 64
NA_HEADS = 8
NA_HEAD_DIM = 64
NA_WIDTH = NA_HEADS * NA_HEAD_DIM
NA_WIN_ROWS_MAX = 8
NA_WIN_COLS = 16
CONV_CH = 512
CONV_WIDTH = 31
DIFF_HEADS = 4
DIFF_HEAD_DIM = 64
DIFF_WIDTH = DIFF_HEADS * 2 * DIFF_HEAD_DIM
N_BRANCH = 3
BRANCH_W = 512
T5_BUCKETS = 32
T5_MAX_DIST = 128
FFN_HIDDEN = 2816
EPS = 1e-6

OFF_CONV = 3 * NA_WIDTH
OFF_DIFF = OFF_CONV + 2 * CONV_CH
OFF_GATE = OFF_DIFF + 3 * DIFF_WIDTH
IN_COLS = OFF_GATE + N_BRANCH * D_MODEL

LANES = 128
SUBLANES = 8
VMEM_LIMIT = 56 * 1024 * 1024

IN_TM = 512
POST_TM = 512
NA_R = 4
NA_KR = 12
CONV_HALO = 16
CONV_RC = 128
DIFF_T = 512
DIFF_STRIP = 256
DIFF_ONES = 16
DIFF_UNROLL = 2
DIFF_MAX_UNROLL = 4
DIFF_PAIRS_PER_STEP = 64
NEG = -1e30
LOG2E = math.log2(math.e)

_EXACT = lax.Precision.HIGHEST


def _rms(x, g):
    return x * lax.rsqrt(jnp.mean(x * x, axis=-1, keepdims=True) + EPS) * g


def _const_spec(shape, index_map):
    return pl.BlockSpec(shape, index_map, pipeline_mode=pl.Buffered(1))


def _in_proj_kernel(x_ref, g_ref, w_ref, bg_ref,
                    naq_ref, nak_ref, nav_ref, cv_ref, dqt_ref, dk_ref, dvt_ref, gate_ref):
    h = _rms(x_ref[...], g_ref[...]).astype(BF16)

    def proj(c0, c1):
        return jnp.dot(h, w_ref[:, c0:c1], preferred_element_type=F32)

    for b in range(N_BRANCH):
        c0 = OFF_GATE + b * D_MODEL
        gate = jax.nn.sigmoid(proj(c0, c0 + D_MODEL) + bg_ref[:, b * D_MODEL:(b + 1) * D_MODEL])
        gate_ref[:, b * D_MODEL:(b + 1) * D_MODEL] = gate.astype(BF16)
    dqt = (proj(OFF_DIFF, OFF_DIFF + DIFF_WIDTH) * (DIFF_HEAD_DIM ** -0.5 * LOG2E)).T
    comp0 = lax.broadcasted_iota(jnp.int32, dqt.shape, 0) % (2 * DIFF_HEAD_DIM) < DIFF_HEAD_DIM
    dqt_ref[:, :IN_TM] = jnp.where(comp0, dqt, 0.0).astype(BF16)
    dqt_ref[:, IN_TM:] = jnp.where(comp0, 0.0, dqt).astype(BF16)
    dvt_ref[...] = proj(OFF_DIFF + 2 * DIFF_WIDTH, OFF_GATE).T.astype(BF16)
    naq_ref[...] = (proj(0, NA_WIDTH) * (NA_HEAD_DIM ** -0.5 * LOG2E)).astype(BF16)
    u = proj(OFF_CONV, OFF_DIFF)
    cv_ref[...] = (u[:, :CONV_CH] * jax.nn.sigmoid(u[:, CONV_CH:])).astype(BF16)
    dk_ref[...] = proj(OFF_DIFF + DIFF_WIDTH, OFF_DIFF + 2 * DIFF_WIDTH).astype(BF16)
    nak_ref[...] = proj(NA_WIDTH, 2 * NA_WIDTH).astype(BF16)
    nav_ref[...] = proj(2 * NA_WIDTH, 3 * NA_WIDTH).astype(BF16)


def _in_proj(x, g_pre, w_in, b_gate, layer):
    n_tok = x.shape[0]
    tm = IN_TM
    assert n_tok % tm == 0
    n_t = n_tok // tm
    tok = lambda w: (pl.BlockSpec((tm, w), lambda i: (i, 0)), jax.ShapeDtypeStruct((n_tok, w), BF16))
    tok_t = lambda w, c: (pl.BlockSpec((None, w, c), lambda i: (i, 0, 0)),
                          jax.ShapeDtypeStruct((n_t, w, c), BF16))
    outs = [tok(NA_WIDTH), tok(NA_WIDTH), tok(NA_WIDTH), tok(CONV_CH),
            tok_t(DIFF_WIDTH, 2 * tm), tok(DIFF_WIDTH), tok_t(DIFF_WIDTH, tm), tok(N_BRANCH * D_MODEL)]
    return pl.pallas_call(
        _in_proj_kernel,
        grid=(n_t,),
        in_specs=[
            pl.BlockSpec((tm, D_MODEL), lambda i: (i, 0)),
            _const_spec((None, 1, D_MODEL), lambda i: (layer, 0, 0)),
            _const_spec((None, D_MODEL, IN_COLS), lambda i: (layer, 0, 0)),
            _const_spec((None, 1, N_BRANCH * D_MODEL), lambda i: (layer, 0, 0)),
        ],
        out_specs=[o[0] for o in outs],
        out_shape=[o[1] for o in outs],
        compiler_params=pltpu.CompilerParams(
            dimension_semantics=("arbitrary",), vmem_limit_bytes=VMEM_LIMIT),
        name="in_proj",
    )(x, g_pre, w_in, b_gate)


def _na_plan(rows):
    assert rows % NA_R == 0 and rows >= NA_KR
    kr = min(NA_WIN_ROWS_MAX, rows)
    n_groups = rows // NA_R
    variants, var_idx, win_start = [], [], []
    for g in range(n_groups):
        r0 = g * NA_R
        ws = int(np.clip(r0 - kr // 2, 0, rows - NA_KR))
        ro = -np.ones((NA_R, NA_KR), np.int32)
        for i in range(NA_R):
            r = r0 + i
            rs = int(np.clip(r - kr // 2, 0, rows - kr))
            assert ws <= rs and rs + kr <= ws + NA_KR
            for a in range(NA_KR):
                if rs <= ws + a < rs + kr:
                    ro[i, a] = ws + a - r + (NA_WIN_ROWS_MAX - 1)
        for v, known in enumerate(variants):
            if np.array_equal(known, ro):
                var_idx.append(v)
                break
        else:
            var_idx.append(len(variants))
            variants.append(ro)
        win_start.append(ws)
    return np.stack(variants), np.asarray(var_idx, np.int32), np.asarray(win_start, np.int32)


def _na_bias_tiles(rpb, row_off):
    n_var = row_off.shape[0]
    n_rows = 2 * NA_WIN_ROWS_MAX - 1
    n_cols = 2 * NA_WIN_COLS - 1
    col = np.arange(GRID_W)
    col_start = np.clip(col - NA_WIN_COLS // 2, 0, GRID_W - NA_WIN_COLS)
    kc = col[None, :]
    col_ok = (kc >= col_start[:, None]) & (kc < col_start[:, None] + NA_WIN_COLS)
    col_off = kc - col[:, None] + (NA_WIN_COLS - 1)
    col_hot = (col_ok[None] & (col_off[None] == np.arange(n_cols)[:, None, None])).astype(np.float32)
    row_hot = (row_off.reshape(-1)[:, None] == np.arange(n_rows)[None, :]).astype(np.float32)
    t = jnp.einsum("xr,hrc->hxc", row_hot, rpb, precision=_EXACT)
    t = jnp.einsum("hxc,cqk->hxqk", t, col_hot, precision=_EXACT)
    ok = (row_off >= 0).reshape(-1)[:, None, None] & col_ok[None]
    t = jnp.where(ok[None], t * LOG2E, NEG).reshape(NA_HEADS, n_var, NA_R, NA_KR, GRID_W, GRID_W)
    return t.transpose(1, 0, 2, 4, 3, 5).reshape(n_var, NA_HEADS, NA_R * GRID_W, NA_KR * GRID_W)


def _conv_rows(xs_ref, r0, w_ref, b_ref, lg_ref, lb_ref):
    first = CONV_HALO - CONV_WIDTH // 2
    ext = CONV_RC + SUBLANES
    groups = []
    for c0 in range(0, CONV_CH, LANES):
        slab = xs_ref[r0:r0 + CONV_RC + 2 * CONV_HALO, c0:c0 + LANES]
        acc = None
        for s in range(SUBLANES):
            part = None
            for j in range(s, CONV_WIDTH, SUBLANES):
                term = w_ref[j:j + 1, c0:c0 + LANES] * slab[j - s:j - s + ext]
                part = term if part is None else part + term
            shift = first + s
            part = part[shift:shift + CONV_RC]
            acc = part if acc is None else acc + part
        groups.append(acc)
    acc = jnp.concatenate(groups, axis=1) + b_ref[...]
    mu = jnp.mean(acc, axis=-1, keepdims=True)
    xc = acc - mu
    y = xc * lax.rsqrt(jnp.mean(xc * xc, axis=-1, keepdims=True) + EPS)
    y = y * lg_ref[...] + lb_ref[...]
    return (y * jax.nn.sigmoid(y)).astype(BF16)


def _na_kernel(var_ref, ws_ref, q_ref, k_ref, v_ref, bias_ref, u_ref, ul_ref, ur_ref, cw_ref, cb_ref, clg_ref, clb_ref,
               o_ref, ocv_ref, xs_ref):
    del var_ref
    n_q = NA_R * GRID_W
    g = pl.program_id(1)
    xs_ref[0:CONV_HALO, :] = jnp.where(g > 0, ul_ref[...].astype(F32), 0.0)
    xs_ref[CONV_HALO:CONV_HALO + n_q, :] = u_ref[...].astype(F32)
    xs_ref[CONV_HALO + n_q:, :] = jnp.where(g < pl.num_programs(1) - 1, ur_ref[...].astype(F32), 0.0)
    for rc in range(n_q // CONV_RC):
        ocv_ref[rc * CONV_RC:(rc + 1) * CONV_RC, :] = _conv_rows(xs_ref, rc * CONV_RC, cw_ref, cb_ref, clg_ref, clb_ref)
    n_k = NA_KR * GRID_W
    k0 = pl.multiple_of(ws_ref[pl.program_id(1)] * GRID_W, GRID_W)
    low_half = lax.broadcasted_iota(jnp.int32, (n_q, LANES), 1) < NA_HEAD_DIM
    for hp in range(NA_HEADS // 2):
        cols = slice(hp * LANES, (hp + 1) * LANES)
        q2 = q_ref[:, cols]
        k2 = k_ref[pl.ds(k0, n_k), cols]
        v2 = jnp.concatenate([v_ref[pl.ds(k0, n_k), cols], jnp.ones((n_k, LANES), BF16)], axis=1)
        halves = []
        for half in range(2):
            keep = low_half if half == 0 else jnp.logical_not(low_half)
            qm = jnp.where(keep, q2, jnp.zeros_like(q2))
            s = lax.dot_general(qm, k2, (((1,), (1,)), ((), ())), preferred_element_type=F32)
            s = s + bias_ref[2 * hp + half]
            p = jnp.exp2(s - jnp.max(s, axis=-1, keepdims=True))
            o = jnp.dot(p.astype(BF16), v2, preferred_element_type=F32)
            halves.append(o[:, :LANES] / o[:, LANES:])
        o_ref[:, cols] = jnp.where(low_half, halves[0], halves[1]).astype(BF16)


def _na(q, k, v, rpb, tile_cache, u, dw_w, dw_b, ln_g, ln_b, layer):
    bsz, seq, _ = q.shape
    assert (NA_R * GRID_W) % CONV_RC == 0 and (NA_R * GRID_W) % CONV_HALO == 0 and seq % CONV_HALO == 0
    assert CONV_HALO >= CONV_WIDTH // 2 and 2 * CONV_HALO >= SUBLANES + (CONV_WIDTH - 1) // SUBLANES * SUBLANES
    per_tile = NA_R * GRID_W // CONV_HALO
    n_halo = seq // CONV_HALO
    cvec = lambda: _const_spec((None, 1, CONV_CH), lambda b, g, var, ws: (layer, 0, 0))
    rows = seq // GRID_W
    row_off, var_idx, win_start = _na_plan(rows)
    key = row_off.tobytes()
    if key not in tile_cache:
        tile_cache[key] = _na_bias_tiles(rpb, row_off)
    bias = tile_cache[key]
    n_q = NA_R * GRID_W
    n_k = NA_KR * GRID_W
    grid_spec = pltpu.PrefetchScalarGridSpec(
        num_scalar_prefetch=2,
        grid=(bsz, rows // NA_R),
        in_specs=[
            pl.BlockSpec((None, n_q, NA_WIDTH), lambda b, g, var, ws: (b, g, 0)),
            pl.BlockSpec((None, seq, NA_WIDTH), lambda b, g, var, ws: (b, 0, 0)),
            pl.BlockSpec((None, seq, NA_WIDTH), lambda b, g, var, ws: (b, 0, 0)),
            pl.BlockSpec((None, NA_HEADS, n_q, n_k), lambda b, g, var, ws: (var[g], 0, 0, 0)),
            pl.BlockSpec((None, n_q, CONV_CH), lambda b, g, var, ws: (b, g, 0)),
            pl.BlockSpec((None, CONV_HALO, CONV_CH), lambda b, g, var, ws: (b, jnp.maximum(g * per_tile - 1, 0), 0)),
            pl.BlockSpec((None, CONV_HALO, CONV_CH), lambda b, g, var, ws: (b, jnp.minimum((g + 1) * per_tile, n_halo - 1), 0)),
            _const_spec((None, CONV_WIDTH, CONV_CH), lambda b, g, var, ws: (layer, 0, 0)),
            cvec(), cvec(), cvec(),
        ],
        out_specs=[pl.BlockSpec((None, n_q, NA_WIDTH), lambda b, g, var, ws: (b, g, 0)),
                   pl.BlockSpec((None, n_q, CONV_CH), lambda b, g, var, ws: (b, g, 0))],
        scratch_shapes=[pltpu.VMEM((n_q + 2 * CONV_HALO, CONV_CH), F32)],
    )
    return pl.pallas_call(
        _na_kernel,
        grid_spec=grid_spec,
        out_shape=[jax.ShapeDtypeStruct((bsz, seq, NA_WIDTH), BF16), jax.ShapeDtypeStruct((bsz, seq, CONV_CH), BF16)],
        compiler_params=pltpu.CompilerParams(
            dimension_semantics=("arbitrary", "arbitrary"), vmem_limit_bytes=VMEM_LIMIT),
        name="na",
    )(jnp.asarray(var_idx), jnp.asarray(win_start), q, k, v, bias, u, u, u, dw_w, dw_b, ln_g, ln_b)


def _t5_bucket(rel):
    nb = T5_BUCKETS // 2
    max_exact = nb // 2
    ret = jnp.where(rel > 0, nb, 0)
    n = jnp.abs(rel)
    nf = jnp.maximum(n, 1).astype(jnp.float32)
    large = max_exact + (jnp.log(nf / max_exact) / math.log(T5_MAX_DIST / max_exact)
                         * (nb - max_exact)).astype(jnp.int32)
    large = jnp.minimum(large, nb - 1)
    return ret + jnp.where(n < max_exact, n, large)


def _toeplitz(g, n):
    lead = g.shape[:-1]
    flat = jnp.tile(g, (1,) * len(lead) + (n,))
    return flat[..., :n * (2 * n - 1)].reshape(lead + (n, 2 * n - 1))[..., :n]


def _diff_bias(t5_bias):
    t = DIFF_T
    assert t >= T5_MAX_DIST
    m = np.arange(2 * t)
    q_minus_k = np.where(m < t, m, m - 2 * t)
    rel = np.stack([d * t - q_minus_k for d in (-1, 0, 1)]).astype(np.int32)
    g = t5_bias[_t5_bucket(jnp.asarray(rel))]
    near = _toeplitz(g.transpose(2, 0, 1), t)
    far = t5_bias[_t5_bucket(jnp.asarray([-T5_MAX_DIST, T5_MAX_DIST], jnp.int32))]
    far = jnp.broadcast_to(far.T[:, :, None, None], (DIFF_HEADS, 2, t, t))
    return jnp.concatenate([far[:, :1], near, far[:, 1:]], axis=1) * LOG2E


def _diff_kernel(lam_ref, qt_ref, k_ref, vt_ref, bias_ref, g_ref, o_ref,
                 s_ref, smax_ref, p_ref, alpha_ref, m_ref, acc_ref, vt1_ref, *, lam_init, n_k, n_seq):
    t = DIFF_T
    n_blocks = n_seq * n_k
    n_pairs = n_blocks * n_k

    strips =[slice(c0, c0 + DIFF_STRIP) for c0 in range(0, 2 * t, DIFF_STRIP)]

    def scores(e, par, cols):
        qb, ki = e // n_k, e % n_k
        sq, qi = qb // n_k, qb % n_k
        kk = k_ref[sq, pl.ds(pl.multiple_of(ki * t, t), t), :]
        s = jnp.dot(kk, qt_ref[qb, :, cols], preferred_element_type=F32)
        q0 = cols.start % t
        s = s + bias_ref[jnp.clip(ki - qi, -2, 2) + 2, :, q0:q0 + DIFF_STRIP]
        s_ref[par, :, cols] = s
        smax_ref[par, :, cols] = jnp.max(s, axis=0, keepdims=True)

    def softmax(e, par, cols):
        m_old = jnp.where(e % n_k == 0, NEG, m_ref[:, cols])
        m_new = jnp.maximum(m_old, smax_ref[par, :, cols])
        m_ref[:, cols] = m_new
        alpha_ref[par, :, cols] = jnp.exp2(m_old - m_new)
        p_ref[par, :, cols] = jnp.exp2(s_ref[par, :, cols] - m_new).astype(BF16)

    def accumulate(e, par, cols):
        qb = e // n_k
        kb = qb // n_k * n_k + e % n_k
        pv = jnp.dot(vt1_ref[kb], p_ref[par, :, cols], preferred_element_type=F32)
        acc_ref[qb, :, cols] = alpha_ref[par, :, cols] * acc_ref[qb, :, cols] + pv

    def finalize(qb, carry):
        inv_l = 1.0 / acc_ref[qb, LANES:LANES + 1, :]
        ot = (acc_ref[qb, :LANES, :t] * inv_l[:, :t]
              - lam_ref[0] * (acc_ref[qb, :LANES, t:] * inv_l[:, t:]))
        o = _rms(ot.T, g_ref[...]) * (1.0 - lam_init)
        o_ref[qb // n_k, pl.ds(pl.multiple_of(qb % n_k * t, t), t), :] = o.astype(BF16)
        return carry

    vt1_ref[:, :LANES, :] = vt_ref[...]
    vt1_ref[:, LANES:, :] = jnp.ones((n_blocks, DIFF_ONES, t), BF16)
    acc_ref[...] = jnp.zeros(acc_ref.shape, F32)
    alpha_ref[1] = jnp.zeros(alpha_ref.shape[1:], F32)
    p_ref[1] = jnp.zeros(p_ref.shape[1:], BF16)
    for cols in strips:
        scores(0, 0, cols)

    def step(e, par):
        e_next = jnp.minimum(e + 1, n_pairs - 1)
        for cols in strips:
            accumulate(jnp.maximum(e - 1, 0), 1 - par, cols)
            softmax(e, par, cols)
            scores(e_next, 1 - par, cols)

    unroll = n_k if n_k <= DIFF_MAX_UNROLL else DIFF_UNROLL
    inline_finalize = unroll == n_k

    def steps(j, carry):
        for u in range(unroll):
            step(unroll * j + u, u % 2)
            if inline_finalize and u == 0:
                finalize(jnp.maximum(j - 1, 0), carry)
        return carry

    lax.fori_loop(0, n_pairs // unroll, steps, 0)
    for cols in strips:
        accumulate(n_pairs - 1, 1, cols)
    if inline_finalize:
        finalize(n_blocks - 1, 0)
    else:
        lax.fori_loop(0, n_blocks, finalize, 0)


def _diff(qt2, k, vt, bias, lam_full, subln_g, lam_init, layer):
    bsz, seq, _ = k.shape
    t = DIFF_T
    assert seq % t == 0 and IN_TM == t
    n_k = seq // t
    assert DIFF_UNROLL % 2 == 0 and n_k % 2 == 0
    n_seq = max(d for d in range(1, bsz + 1) if bsz % d == 0 and (d == 1 or d * n_k * n_k <= DIFF_PAIRS_PER_STEP))
    n_blocks = n_seq * n_k
    smem = pl.BlockSpec(memory_space=pltpu.SMEM)
    return pl.pallas_call(
        functools.partial(_diff_kernel, lam_init=lam_init, n_k=n_k, n_seq=n_seq),
        grid=(DIFF_HEADS, bsz // n_seq),
        in_specs=[
            smem,
            pl.BlockSpec((n_blocks, LANES, 2 * t), lambda h, b: (b, h, 0)),
            pl.BlockSpec((n_seq, seq, LANES), lambda h, b: (b, 0, h)),
            pl.BlockSpec((n_blocks, LANES, t), lambda h, b: (b, h, 0)),
            pl.BlockSpec((None, 5, t, t), lambda h, b: (h, 0, 0, 0)),
            _const_spec((None, 1, LANES), lambda h, b: (layer, 0, 0)),
        ],
        out_specs=pl.BlockSpec((n_seq, seq, LANES), lambda h, b: (b, 0, h)),
        out_shape=jax.ShapeDtypeStruct((bsz, seq, DIFF_WIDTH), BF16),
        scratch_shapes=[
            pltpu.VMEM((2, t, 2 * t), F32),
            pltpu.VMEM((2, 1, 2 * t), F32),
            pltpu.VMEM((2, t, 2 * t), BF16),
            pltpu.VMEM((2, 1, 2 * t), F32),
            pltpu.VMEM((1, 2 * t), F32),
            pltpu.VMEM((n_blocks, LANES + DIFF_ONES, 2 * t), F32),
            pltpu.VMEM((n_blocks, LANES + DIFF_ONES, t), BF16),
        ],
        compiler_params=pltpu.CompilerParams(
            dimension_semantics=("arbitrary", "arbitrary"), vmem_limit_bytes=VMEM_LIMIT),
        name="diff",
    )(lam_full, qt2, k, vt, bias, subln_g)


_FFN_CHUNKS = ((0, 768), (768, 1536), (1536, 2304), (2304, FFN_HIDDEN))


def _post_kernel(x_ref, ona_ref, ocv_ref, odf_ref, gate_ref, wb_ref, wo_ref,
                 g_mix_ref, g_pre_ref, g_post_ref, wfi_ref, wfo_ref, o_ref, act_ref):
    merged = None
    for b, br_ref in enumerate((ona_ref, ocv_ref, odf_ref)):
        proj = jnp.dot(br_ref[...], wb_ref[b], preferred_element_type=F32)
        term = gate_ref[:, b * D_MODEL:(b + 1) * D_MODEL].astype(F32) * proj
        merged = term if merged is None else merged + term
    y = jnp.dot(merged.astype(BF16), wo_ref[...], preferred_element_type=F32)
    x1 = x_ref[...] + _rms(y, g_mix_ref[...])
    hf = _rms(x1, g_pre_ref[...]).astype(BF16)
    for c0, c1 in _FFN_CHUNKS:
        gate = jnp.dot(hf, wfi_ref[:, c0:c1], preferred_element_type=F32)
        up = jnp.dot(hf, wfi_ref[:, FFN_HIDDEN + c0:FFN_HIDDEN + c1], preferred_element_type=F32)
        act_ref[:, c0:c1] = (gate * jax.nn.sigmoid(gate) * up).astype(BF16)
    z = jnp.dot(act_ref[...], wfo_ref[...], preferred_element_type=F32)
    o_ref[...] = x1 + _rms(z, g_post_ref[...])


def _post(x, o_na, o_cv, o_df, gates, w_branch, w_out, g_mix, g_pre, g_post, w_ffn_in, w_ffn_out,
          layer):
    n_tok = x.shape[0]
    tm = POST_TM
    assert n_tok % tm == 0
    vec = lambda: _const_spec((None, 1, D_MODEL), lambda i: (layer, 0, 0))
    br = lambda: pl.BlockSpec((tm, BRANCH_W), lambda i: (i, 0))
    return pl.pallas_call(
        _post_kernel,
        grid=(n_tok // tm,),
        in_specs=[
            pl.BlockSpec((tm, D_MODEL), lambda i: (i, 0)),
            br(), br(), br(),
            pl.BlockSpec((tm, N_BRANCH * D_MODEL), lambda i: (i, 0)),
            _const_spec((None, N_BRANCH, BRANCH_W, D_MODEL), lambda i: (layer, 0, 0, 0)),
            _const_spec((None, D_MODEL, D_MODEL), lambda i: (layer, 0, 0)),
            vec(), vec(), vec(),
            _const_spec((None, D_MODEL, 2 * FFN_HIDDEN), lambda i: (layer, 0, 0)),
            _const_spec((None, FFN_HIDDEN, D_MODEL), lambda i: (layer, 0, 0)),
        ],
        out_specs=pl.BlockSpec((tm, D_MODEL), lambda i: (i, 0)),
        out_shape=jax.ShapeDtypeStruct((n_tok, D_MODEL), F32),
        scratch_shapes=[pltpu.VMEM((tm, FFN_HIDDEN), BF16)],
        compiler_params=pltpu.CompilerParams(
            dimension_semantics=("arbitrary",), vmem_limit_bytes=VMEM_LIMIT),
        name="post",
    )(x, o_na, o_cv, o_df, gates, w_branch, w_out, g_mix, g_pre, g_post, w_ffn_in, w_ffn_out)


def _trunk(x, p):
    bsz, seq, _ = x.shape
    n_tok = bsz * seq
    depth = p["w_in"].shape[0]
    x = x.reshape(n_tok, D_MODEL)
    for l in range(depth):
        lam_init = 0.8 - 0.6 * math.exp(-0.3 * l)
        naq, nak, nav, glu, dqt, dk, dvt, gates = _in_proj(x, p["ln_mix_pre"], p["w_in"], p["b_gate"], l)
        seq3 = lambda a: a.reshape(bsz, seq, a.shape[-1])
        o_na, o_cv = _na(seq3(naq), seq3(nak), seq3(nav), p["na_rpb"][l], p["na_tiles"][l],
                         seq3(glu), p["conv_dw_w"], p["conv_dw_b"], p["conv_ln_g"], p["conv_ln_b"], l)
        o_df = _diff(dqt, seq3(dk), dvt, p["diff_bias"], p["lam_full"][l], p["diff_subln_g"],
                     lam_init, l)
        flat = lambda a: a.reshape(n_tok, a.shape[-1])
        x = _post(x, flat(o_na), flat(o_cv), flat(o_df), gates, p["w_branch"], p["w_out"],
                  p["ln_mix_post"], p["ln_ffn_pre"], p["ln_ffn_post"], p["w_ffn_in"], p["w_ffn_out"], l)
    return x.reshape(bsz, seq, D_MODEL)


def _prepare(w_in, b_gate, na_rpb, conv_dw_w, conv_dw_b, conv_ln_g, conv_ln_b,
             diff_lambda, diff_subln_g, t5_bias, w_branch, w_out,
             ln_mix_pre, ln_mix_post, ln_ffn_pre, ln_ffn_post, w_ffn_in, w_ffn_out):
    depth = w_in.shape[0]
    row = lambda a: a.reshape(depth, 1, a.shape[-1])
    lf = diff_lambda.astype(F32)
    lam_init = jnp.asarray([0.8 - 0.6 * math.exp(-0.3 * l) for l in range(depth)], F32)
    lam_full = (jnp.exp(jnp.sum(lf[:, 0] * lf[:, 1], axis=-1))
                - jnp.exp(jnp.sum(lf[:, 2] * lf[:, 3], axis=-1)) + lam_init)
    return dict(
        w_in=w_in.astype(BF16), b_gate=row(b_gate), na_rpb=na_rpb, na_tiles=[{} for _ in range(depth)],
        conv_dw_w=conv_dw_w, conv_dw_b=row(conv_dw_b), conv_ln_g=row(conv_ln_g), conv_ln_b=row(conv_ln_b),
        lam_full=lam_full.reshape(depth, 1), diff_subln_g=row(diff_subln_g),
        diff_bias=_diff_bias(t5_bias),
        w_branch=w_branch.astype(BF16), w_out=w_out.astype(BF16),
        ln_mix_pre=row(ln_mix_pre), ln_mix_post=row(ln_mix_post),
        ln_ffn_pre=row(ln_ffn_pre), ln_ffn_post=row(ln_ffn_post),
        w_ffn_in=w_ffn_in.astype(BF16), w_ffn_out=w_ffn_out.astype(BF16),
    )


def kernel(x_prompt, x_sample, w_in, b_gate, na_rpb, conv_dw_w, conv_dw_b, conv_ln_g, conv_ln_b,
           diff_lambda, diff_subln_g, t5_bias, w_branch, w_out,
           ln_mix_pre, ln_mix_post, ln_ffn_pre, ln_ffn_post, w_ffn_in, w_ffn_out):
    p = _prepare(w_in, b_gate, na_rpb, conv_dw_w, conv_dw_b, conv_ln_g, conv_ln_b,
                 diff_lambda, diff_subln_g, t5_bias, w_branch, w_out,
                 ln_mix_pre, ln_mix_post, ln_ffn_pre, ln_ffn_post, w_ffn_in, w_ffn_out)
    return (_trunk(x_prompt, p), _trunk(x_sample, p))
```

```python
import functools
import math

import numpy as np
import jax
import jax.numpy as jnp
from jax import lax
from jax.experimental import pallas as pl
from jax.experimental.pallas import tpu as pltpu

F32 = jnp.float32
BF16 = jnp.bfloat16

D_MODEL = 1024
GRID_W = 64
NA_HEADS = 8
NA_HEAD_DIM = 64
NA_WIDTH = NA_HEADS * NA_HEAD_DIM
NA_WIN_ROWS_MAX = 8
NA_WIN_COLS = 16
CONV_CH = 512
CONV_WIDTH = 31
DIFF_HEADS = 4
DIFF_HEAD_DIM = 64
DIFF_WIDTH = DIFF_HEADS * 2 * DIFF_HEAD_DIM
N_BRANCH = 3
BRANCH_W = 512
T5_BUCKETS = 32
T5_MAX_DIST = 128
FFN_HIDDEN = 2816
EPS = 1e-6

OFF_CONV = 3 * NA_WIDTH
OFF_DIFF = OFF_CONV + 2 * CONV_CH
OFF_GATE = OFF_DIFF + 3 * DIFF_WIDTH
IN_COLS = OFF_GATE + N_BRANCH * D_MODEL

LANES = 128
SUBLANES = 8
VMEM_LIMIT = 56 * 1024 * 1024

IN_TM = 512
POST_TM = 512
NA_R = 4
NA_KR = 12
CONV_HALO = 16
CONV_RC = 128
DIFF_T = 512
DIFF_STRIP = 256
DIFF_ONES = 16
DIFF_UNROLL = 2
DIFF_MAX_UNROLL = 4
DIFF_PAIRS_PER_STEP = 64
NEG = -1e30
LOG2E = math.log2(math.e)

_EXACT = lax.Precision.HIGHEST


def _rms(x, g):
    return x * lax.rsqrt(jnp.mean(x * x, axis=-1, keepdims=True) + EPS) * g


def _const_spec(shape, index_map):
    return pl.BlockSpec(shape, index_map, pipeline_mode=pl.Buffered(1))


def _in_proj_kernel(x_ref, g_ref, w_ref, bg_ref,
                    naq_ref, nak_ref, nav_ref, cv_ref, dqt_ref, dk_ref, dvt_ref, gate_ref):
    h = _rms(x_ref[...], g_ref[...]).astype(BF16)

    def proj(c0, c1):
        return jnp.dot(h, w_ref[:, c0:c1], preferred_element_type=F32)

    for b in range(N_BRANCH):
        c0 = OFF_GATE + b * D_MODEL
        gate = jax.nn.sigmoid(proj(c0, c0 + D_MODEL) + bg_ref[:, b * D_MODEL:(b + 1) * D_MODEL])
        gate_ref[:, b * D_MODEL:(b + 1) * D_MODEL] = gate.astype(BF16)
    dqt = (proj(OFF_DIFF, OFF_DIFF + DIFF_WIDTH) * (DIFF_HEAD_DIM ** -0.5 * LOG2E)).T
    comp0 = lax.broadcasted_iota(jnp.int32, dqt.shape, 0) % (2 * DIFF_HEAD_DIM) < DIFF_HEAD_DIM
    dqt_ref[:, :IN_TM] = jnp.where(comp0, dqt, 0.0).astype(BF16)
    dqt_ref[:, IN_TM:] = jnp.where(comp0, 0.0, dqt).astype(BF16)
    dvt_ref[...] = proj(OFF_DIFF + 2 * DIFF_WIDTH, OFF_GATE).T.astype(BF16)
    naq_ref[...] = (proj(0, NA_WIDTH) * (NA_HEAD_DIM ** -0.5 * LOG2E)).astype(BF16)
    u = proj(OFF_CONV, OFF_DIFF)
    cv_ref[...] = (u[:, :CONV_CH] * jax.nn.sigmoid(u[:, CONV_CH:])).astype(BF16)
    dk_ref[...] = proj(OFF_DIFF + DIFF_WIDTH, OFF_DIFF + 2 * DIFF_WIDTH).astype(BF16)
    nak_ref[...] = proj(NA_WIDTH, 2 * NA_WIDTH).astype(BF16)
    nav_ref[...] = proj(2 * NA_WIDTH, 3 * NA_WIDTH).astype(BF16)


def _in_proj(x, g_pre, w_in, b_gate, layer):
    n_tok = x.shape[0]
    tm = IN_TM
    assert n_tok % tm == 0
    n_t = n_tok // tm
    tok = lambda w: (pl.BlockSpec((tm, w), lambda i: (i, 0)), jax.ShapeDtypeStruct((n_tok, w), BF16))
    tok_t = lambda w, c: (pl.BlockSpec((None, w, c), lambda i: (i, 0, 0)),
                          jax.ShapeDtypeStruct((n_t, w, c), BF16))
    outs = [tok(NA_WIDTH), tok(NA_WIDTH), tok(NA_WIDTH), tok(CONV_CH),
            tok_t(DIFF_WIDTH, 2 * tm), tok(DIFF_WIDTH), tok_t(DIFF_WIDTH, tm), tok(N_BRANCH * D_MODEL)]
    return pl.pallas_call(
        _in_proj_kernel,
        grid=(n_t,),
        in_specs=[
            pl.BlockSpec((tm, D_MODEL), lambda i: (i, 0)),
            _const_spec((None, 1, D_MODEL), lambda i: (layer, 0, 0)),
            _const_spec((None, D_MODEL, IN_COLS), lambda i: (layer, 0, 0)),
            _const_spec((None, 1, N_BRANCH * D_MODEL), lambda i: (layer, 0, 0)),
        ],
        out_specs=[o[0] for o in outs],
        out_shape=[o[1] for o in outs],
        compiler_params=pltpu.CompilerParams(
            dimension_semantics=("arbitrary",), vmem_limit_bytes=VMEM_LIMIT),
        name="in_proj",
    )(x, g_pre, w_in, b_gate)


def _na_plan(rows):
    assert rows % NA_R == 0 and rows >= NA_KR
    kr = min(NA_WIN_ROWS_MAX, rows)
    n_groups = rows // NA_R
    variants, var_idx, win_start = [], [], []
    for g in range(n_groups):
        r0 = g * NA_R
        ws = int(np.clip(r0 - kr // 2, 0, rows - NA_KR))
        ro = -np.ones((NA_R, NA_KR), np.int32)
        for i in range(NA_R):
            r = r0 + i
            rs = int(np.clip(r - kr // 2, 0, rows - kr))
            assert ws <= rs and rs + kr <= ws + NA_KR
            for a in range(NA_KR):
                if rs <= ws + a < rs + kr:
                    ro[i, a] = ws + a - r + (NA_WIN_ROWS_MAX - 1)
        for v, known in enumerate(variants):
            if np.array_equal(known, ro):
                var_idx.append(v)
                break
        else:
            var_idx.append(len(variants))
            variants.append(ro)
        win_start.append(ws)
    return np.stack(variants), np.asarray(var_idx, np.int32), np.asarray(win_start, np.int32)


def _na_bias_tiles(rpb, row_off):
    n_var = row_off.shape[0]
    n_rows = 2 * NA_WIN_ROWS_MAX - 1
    n_cols = 2 * NA_WIN_COLS - 1
    col = np.arange(GRID_W)
    col_start = np.clip(col - NA_WIN_COLS // 2, 0, GRID_W - NA_WIN_COLS)
    kc = col[None, :]
    col_ok = (kc >= col_start[:, None]) & (kc < col_start[:, None] + NA_WIN_COLS)
    col_off = kc - col[:, None] + (NA_WIN_COLS - 1)
    col_hot = (col_ok[None] & (col_off[None] == np.arange(n_cols)[:, None, None])).astype(np.float32)
    row_hot = (row_off.reshape(-1)[:, None] == np.arange(n_rows)[None, :]).astype(np.float32)
    t = jnp.einsum("xr,hrc->hxc", row_hot, rpb, precision=_EXACT)
    t = jnp.einsum("hxc,cqk->hxqk", t, col_hot, precision=_EXACT)
    ok = (row_off >= 0).reshape(-1)[:, None, None] & col_ok[None]
    t = jnp.where(ok[None], t * LOG2E, NEG).reshape(NA_HEADS, n_var, NA_R, NA_KR, GRID_W, GRID_W)
    return t.transpose(1, 0, 2, 4, 3, 5).reshape(n_var, NA_HEADS, NA_R * GRID_W, NA_KR * GRID_W)


def _conv_rows(xs_ref, r0, w_ref, b_ref, lg_ref, lb_ref):
    first = CONV_HALO - CONV_WIDTH // 2
    ext = CONV_RC + SUBLANES
    groups = []
    for c0 in range(0, CONV_CH, LANES):
        slab = xs_ref[r0:r0 + CONV_RC + 2 * CONV_HALO, c0:c0 + LANES]
        acc = None
        for s in range(SUBLANES):
            part = None
            for j in range(s, CONV_WIDTH, SUBLANES):
                term = w_ref[j:j + 1, c0:c0 + LANES] * slab[j - s:j - s + ext]
                part = term if part is None else part + term
            shift = first + s
            part = part[shift:shift + CONV_RC]
            acc = part if acc is None else acc + part
        groups.append(acc)
    acc = jnp.concatenate(groups, axis=1) + b_ref[...]
    mu = jnp.mean(acc, axis=-1, keepdims=True)
    xc = acc - mu
    y = xc * lax.rsqrt(jnp.mean(xc * xc, axis=-1, keepdims=True) + EPS)
    y = y * lg_ref[...] + lb_ref[...]
    return (y * jax.nn.sigmoid(y)).astype(BF16)


def _na_kernel(var_ref, ws_ref, q_ref, k_ref, v_ref, bias_ref, u_ref, ul_ref, ur_ref, cw_ref, cb_ref, clg_ref, clb_ref,
               o_ref, ocv_ref, xs_ref):
    del var_ref
    n_q = NA_R * GRID_W
    g = pl.program_id(1)
    xs_ref[0:CONV_HALO, :] = jnp.where(g > 0, ul_ref[...].astype(F32), 0.0)
    xs_ref[CONV_HALO:CONV_HALO + n_q, :] = u_ref[...].astype(F32)
    xs_ref[CONV_HALO + n_q:, :] = jnp.where(g < pl.num_programs(1) - 1, ur_ref[...].astype(F32), 0.0)
    for rc in range(n_q // CONV_RC):
        ocv_ref[rc * CONV_RC:(rc + 1) * CONV_RC, :] = _conv_rows(xs_ref, rc * CONV_RC, cw_ref, cb_ref, clg_ref, clb_ref)
    n_k = NA_KR * GRID_W
    k0 = pl.multiple_of(ws_ref[pl.program_id(1)] * GRID_W, GRID_W)
    low_half = lax.broadcasted_iota(jnp.int32, (n_q, LANES), 1) < NA_HEAD_DIM
    for hp in range(NA_HEADS // 2):
        cols = slice(hp * LANES, (hp + 1) * LANES)
        q2 = q_ref[:, cols]
        k2 = k_ref[pl.ds(k0, n_k), cols]
        v2 = jnp.concatenate([v_ref[pl.ds(k0, n_k), cols], jnp.ones((n_k, LANES), BF16)], axis=1)
        halves = []
        for half in range(2):
            keep = low_half if half == 0 else jnp.logical_not(low_half)
            qm = jnp.where(keep, q2, jnp.zeros_like(q2))
            s = lax.dot_general(qm, k2, (((1,), (1,)), ((), ())), preferred_element_type=F32)
            s = s + bias_ref[2 * hp + half]
            p = jnp.exp2(s - jnp.max(s, axis=-1, keepdims=True))
            o = jnp.dot(p.astype(BF16), v2, preferred_element_type=F32)
            halves.append(o[:, :LANES] / o[:, LANES:])
        o_ref[:, cols] = jnp.where(low_half, halves[0], halves[1]).astype(BF16)


def _na(q, k, v, rpb, tile_cache, u, dw_w, dw_b, ln_g, ln_b, layer):
    bsz, seq, _ = q.shape
    assert (NA_R * GRID_W) % CONV_RC == 0 and (NA_R * GRID_W) % CONV_HALO == 0 and seq % CONV_HALO == 0
    assert CONV_HALO >= CONV_WIDTH // 2 and 2 * CONV_HALO >= SUBLANES + (CONV_WIDTH - 1) // SUBLANES * SUBLANES
    per_tile = NA_R * GRID_W // CONV_HALO
    n_halo = seq // CONV_HALO
    cvec = lambda: _const_spec((None, 1, CONV_CH), lambda b, g, var, ws: (layer, 0, 0))
    rows = seq // GRID_W
    row_off, var_idx, win_start = _na_plan(rows)
    key = row_off.tobytes()
    if key not in tile_cache:
        tile_cache[key] = _na_bias_tiles(rpb, row_off)
    bias = tile_cache[key]
    n_q = NA_R * GRID_W
    n_k = NA_KR * GRID_W
    grid_spec = pltpu.PrefetchScalarGridSpec(
        num_scalar_prefetch=2,
        grid=(bsz, rows // NA_R),
        in_specs=[
            pl.BlockSpec((None, n_q, NA_WIDTH), lambda b, g, var, ws: (b, g, 0)),
            pl.BlockSpec((None, seq, NA_WIDTH), lambda b, g, var, ws: (b, 0, 0)),
            pl.BlockSpec((None, seq, NA_WIDTH), lambda b, g, var, ws: (b, 0, 0)),
            pl.BlockSpec((None, NA_HEADS, n_q, n_k), lambda b, g, var, ws: (var[g], 0, 0, 0)),
            pl.BlockSpec((None, n_q, CONV_CH), lambda b, g, var, ws: (b, g, 0)),
            pl.BlockSpec((None, CONV_HALO, CONV_CH), lambda b, g, var, ws: (b, jnp.maximum(g * per_tile - 1, 0), 0)),
            pl.BlockSpec((None, CONV_HALO, CONV_CH), lambda b, g, var, ws: (b, jnp.minimum((g + 1) * per_tile, n_halo - 1), 0)),
            _const_spec((None, CONV_WIDTH, CONV_CH), lambda b, g, var, ws: (layer, 0, 0)),
            cvec(), cvec(), cvec(),
        ],
        out_specs=[pl.BlockSpec((None, n_q, NA_WIDTH), lambda b, g, var, ws: (b, g, 0)),
                   pl.BlockSpec((None, n_q, CONV_CH), lambda b, g, var, ws: (b, g, 0))],
        scratch_shapes=[pltpu.VMEM((n_q + 2 * CONV_HALO, CONV_CH), F32)],
    )
    return pl.pallas_call(
        _na_kernel,
        grid_spec=grid_spec,
        out_shape=[jax.ShapeDtypeStruct((bsz, seq, NA_WIDTH), BF16), jax.ShapeDtypeStruct((bsz, seq, CONV_CH), BF16)],
        compiler_params=pltpu.CompilerParams(
            dimension_semantics=("arbitrary", "arbitrary"), vmem_limit_bytes=VMEM_LIMIT),
        name="na",
    )(jnp.asarray(var_idx), jnp.asarray(win_start), q, k, v, bias, u, u, u, dw_w, dw_b, ln_g, ln_b)


def _t5_bucket(rel):
    nb = T5_BUCKETS // 2
    max_exact = nb // 2
    ret = jnp.where(rel > 0, nb, 0)
    n = jnp.abs(rel)
    nf = jnp.maximum(n, 1).astype(jnp.float32)
    large = max_exact + (jnp.log(nf / max_exact) / math.log(T5_MAX_DIST / max_exact)
                         * (nb - max_exact)).astype(jnp.int32)
    large = jnp.minimum(large, nb - 1)
    return ret + jnp.where(n < max_exact, n, large)


def _toeplitz(g, n):
    lead = g.shape[:-1]
    flat = jnp.tile(g, (1,) * len(lead) + (n,))
    return flat[..., :n * (2 * n - 1)].reshape(lead + (n, 2 * n - 1))[..., :n]


def _diff_bias(t5_bias):
    t = DIFF_T
    assert t >= T5_MAX_DIST
    m = np.arange(2 * t)
    q_minus_k = np.where(m < t, m, m - 2 * t)
    rel = np.stack([d * t - q_minus_k for d in (-1, 0, 1)]).astype(np.int32)
    g = t5_bias[_t5_bucket(jnp.asarray(rel))]
    near = _toeplitz(g.transpose(2, 0, 1), t)
    far = t5_bias[_t5_bucket(jnp.asarray([-T5_MAX_DIST, T5_MAX_DIST], jnp.int32))]
    far = jnp.broadcast_to(far.T[:, :, None, None], (DIFF_HEADS, 2, t, t))
    return jnp.concatenate([far[:, :1], near, far[:, 1:]], axis=1) * LOG2E


def _diff_kernel(lam_ref, qt_ref, k_ref, vt_ref, bias_ref, g_ref, o_ref,
                 s_ref, smax_ref, p_ref, alpha_ref, m_ref, acc_ref, vt1_ref, *, lam_init, n_k, n_seq):
    t = DIFF_T
    n_blocks = n_seq * n_k
    n_pairs = n_blocks * n_k

    strips =[slice(c0, c0 + DIFF_STRIP) for c0 in range(0, 2 * t, DIFF_STRIP)]

    def scores(e, par, cols):
        qb, ki = e // n_k, e % n_k
        sq, qi = qb // n_k, qb % n_k
        kk = k_ref[sq, pl.ds(pl.multiple_of(ki * t, t), t), :]
        s = jnp.dot(kk, qt_ref[qb, :, cols], preferred_element_type=F32)
        q0 = cols.start % t
        s = s + bias_ref[jnp.clip(ki - qi, -2, 2) + 2, :, q0:q0 + DIFF_STRIP]
        s_ref[par, :, cols] = s
        smax_ref[par, :, cols] = jnp.max(s, axis=0, keepdims=True)

    def softmax(e, par, cols):
        m_old = jnp.where(e % n_k == 0, NEG, m_ref[:, cols])
        m_new = jnp.maximum(m_old, smax_ref[par, :, cols])
        m_ref[:, cols] = m_new
        alpha_ref[par, :, cols] = jnp.exp2(m_old - m_new)
        p_ref[par, :, cols] = jnp.exp2(s_ref[par, :, cols] - m_new).astype(BF16)

    def accumulate(e, par, cols):
        qb = e // n_k
        kb = qb // n_k * n_k + e % n_k
        pv = lax.dot_general(p_ref[par, :, cols], vt1_ref[kb], (((0,), (1,)), ((), ())),
                             preferred_element_type=F32).T
        acc_ref[qb, :, cols] = alpha_ref[par, :, cols] * acc_ref[qb, :, cols] + pv

    def finalize(qb, carry):
        inv_l = 1.0 / acc_ref[qb, LANES:LANES + 1, :]
        ot = (acc_ref[qb, :LANES, :t] * inv_l[:, :t]
              - lam_ref[0] * (acc_ref[qb, :LANES, t:] * inv_l[:, t:]))
        o = _rms(ot.T, g_ref[...]) * (1.0 - lam_init)
        o_ref[qb // n_k, pl.ds(pl.multiple_of(qb % n_k * t, t), t), :] = o.astype(BF16)
        return carry

    vt1_ref[:, :LANES, :] = vt_ref[...]
    vt1_ref[:, LANES:, :] = jnp.ones((n_blocks, DIFF_ONES, t), BF16)
    acc_ref[...] = jnp.zeros(acc_ref.shape, F32)
    alpha_ref[1] = jnp.zeros(alpha_ref.shape[1:], F32)
    p_ref[1] = jnp.zeros(p_ref.shape[1:], BF16)
    for cols in strips:
        scores(0, 0, cols)

    def step(e, par):
        e_next = jnp.minimum(e + 1, n_pairs - 1)
        for cols in strips:
            accumulate(jnp.maximum(e - 1, 0), 1 - par, cols)
            softmax(e, par, cols)
            scores(e_next, 1 - par, cols)

    unroll = n_k if n_k <= DIFF_MAX_UNROLL else DIFF_UNROLL
    inline_finalize = unroll == n_k

    def steps(j, carry):
        for u in range(unroll):
            step(unroll * j + u, u % 2)
            if inline_finalize and u == 0:
                finalize(jnp.maximum(j - 1, 0), carry)
        return carry

    lax.fori_loop(0, n_pairs // unroll, steps, 0)
    for cols in strips:
        accumulate(n_pairs - 1, 1, cols)
    if inline_finalize:
        finalize(n_blocks - 1, 0)
    else:
        lax.fori_loop(0, n_blocks, finalize, 0)


def _diff(qt2, k, vt, bias, lam_full, subln_g, lam_init, layer):
    bsz, seq, _ = k.shape
    t = DIFF_T
    assert seq % t == 0 and IN_TM == t
    n_k = seq // t
    assert DIFF_UNROLL % 2 == 0 and n_k % 2 == 0
    n_seq = max(d for d in range(1, bsz + 1) if bsz % d == 0 and (d == 1 or d * n_k * n_k <= DIFF_PAIRS_PER_STEP))
    n_blocks = n_seq * n_k
    smem = pl.BlockSpec(memory_space=pltpu.SMEM)
    return pl.pallas_call(
        functools.partial(_diff_kernel, lam_init=lam_init, n_k=n_k, n_seq=n_seq),
        grid=(DIFF_HEADS, bsz // n_seq),
        in_specs=[
            smem,
            pl.BlockSpec((n_blocks, LANES, 2 * t), lambda h, b: (b, h, 0)),
            pl.BlockSpec((n_seq, seq, LANES), lambda h, b: (b, 0, h)),
            pl.BlockSpec((n_blocks, LANES, t), lambda h, b: (b, h, 0)),
            pl.BlockSpec((None, 5, t, t), lambda h, b: (h, 0, 0, 0)),
            _const_spec((None, 1, LANES), lambda h, b: (layer, 0, 0)),
        ],
        out_specs=pl.BlockSpec((n_seq, seq, LANES), lambda h, b: (b, 0, h)),
        out_shape=jax.ShapeDtypeStruct((bsz, seq, DIFF_WIDTH), BF16),
        scratch_shapes=[
            pltpu.VMEM((2, t, 2 * t), F32),
            pltpu.VMEM((2, 1, 2 * t), F32),
            pltpu.VMEM((2, t, 2 * t), BF16),
            pltpu.VMEM((2, 1, 2 * t), F32),
            pltpu.VMEM((1, 2 * t), F32),
            pltpu.VMEM((n_blocks, LANES + DIFF_ONES, 2 * t), F32),
            pltpu.VMEM((n_blocks, LANES + DIFF_ONES, t), BF16),
        ],
        compiler_params=pltpu.CompilerParams(
            dimension_semantics=("arbitrary", "arbitrary"), vmem_limit_bytes=VMEM_LIMIT),
        name="diff",
    )(lam_full, qt2, k, vt, bias, subln_g)


_FFN_CHUNKS = ((0, 768), (768, 1536), (1536, 2304), (2304, FFN_HIDDEN))


def _post_kernel(x_ref, ona_ref, ocv_ref, odf_ref, gate_ref, wb_ref, wo_ref,
                 g_mix_ref, g_pre_ref, g_post_ref, wfi_ref, wfo_ref, o_ref, act_ref):
    merged = None
    for b, br_ref in enumerate((ona_ref, ocv_ref, odf_ref)):
        proj = jnp.dot(br_ref[...], wb_ref[b], preferred_element_type=F32)
        term = gate_ref[:, b * D_MODEL:(b + 1) * D_MODEL].astype(F32) * proj
        merged = term if merged is None else merged + term
    y = jnp.dot(merged.astype(BF16), wo_ref[...], preferred_element_type=F32)
    x1 = x_ref[...] + _rms(y, g_mix_ref[...])
    hf = _rms(x1, g_pre_ref[...]).astype(BF16)
    for c0, c1 in _FFN_CHUNKS:
        gate = jnp.dot(hf, wfi_ref[:, c0:c1], preferred_element_type=F32)
        up = jnp.dot(hf, wfi_ref[:, FFN_HIDDEN + c0:FFN_HIDDEN + c1], preferred_element_type=F32)
        act_ref[:, c0:c1] = (gate * jax.nn.sigmoid(gate) * up).astype(BF16)
    z = jnp.dot(act_ref[...], wfo_ref[...], preferred_element_type=F32)
    o_ref[...] = x1 + _rms(z, g_post_ref[...])


def _post(x, o_na, o_cv, o_df, gates, w_branch, w_out, g_mix, g_pre, g_post, w_ffn_in, w_ffn_out,
          layer):
    n_tok = x.shape[0]
    tm = POST_TM
    assert n_tok % tm == 0
    vec = lambda: _const_spec((None, 1, D_MODEL), lambda i: (layer, 0, 0))
    br = lambda: pl.BlockSpec((tm, BRANCH_W), lambda i: (i, 0))
    return pl.pallas_call(
        _post_kernel,
        grid=(n_tok // tm,),
        in_specs=[
            pl.BlockSpec((tm, D_MODEL), lambda i: (i, 0)),
            br(), br(), br(),
            pl.BlockSpec((tm, N_BRANCH * D_MODEL), lambda i: (i, 0)),
            _const_spec((None, N_BRANCH, BRANCH_W, D_MODEL), lambda i: (layer, 0, 0, 0)),
            _const_spec((None, D_MODEL, D_MODEL), lambda i: (layer, 0, 0)),
            vec(), vec(), vec(),
            _const_spec((None, D_MODEL, 2 * FFN_HIDDEN), lambda i: (layer, 0, 0)),
            _const_spec((None, FFN_HIDDEN, D_MODEL), lambda i: (layer, 0, 0)),
        ],
        out_specs=pl.BlockSpec((tm, D_MODEL), lambda i: (i, 0)),
        out_shape=jax.ShapeDtypeStruct((n_tok, D_MODEL), F32),
        scratch_shapes=[pltpu.VMEM((tm, FFN_HIDDEN), BF16)],
        compiler_params=pltpu.CompilerParams(
            dimension_semantics=("arbitrary",), vmem_limit_bytes=VMEM_LIMIT),
        name="post",
    )(x, o_na, o_cv, o_df, gates, w_branch, w_out, g_mix, g_pre, g_post, w_ffn_in, w_ffn_out)


def _trunk(x, p):
    bsz, seq, _ = x.shape
    n_tok = bsz * seq
    depth = p["w_in"].shape[0]
    x = x.reshape(n_tok, D_MODEL)
    for l in range(depth):
        lam_init = 0.8 - 0.6 * math.exp(-0.3 * l)
        naq, nak, nav, glu, dqt, dk, dvt, gates = _in_proj(x, p["ln_mix_pre"], p["w_in"], p["b_gate"], l)
        seq3 = lambda a: a.reshape(bsz, seq, a.shape[-1])
        o_na, o_cv = _na(seq3(naq), seq3(nak), seq3(nav), p["na_rpb"][l], p["na_tiles"][l],
                         seq3(glu), p["conv_dw_w"], p["conv_dw_b"], p["conv_ln_g"], p["conv_ln_b"], l)
        o_df = _diff(dqt, seq3(dk), dvt, p["diff_bias"], p["lam_full"][l], p["diff_subln_g"],
                     lam_init, l)
        flat = lambda a: a.reshape(n_tok, a.shape[-1])
        x = _post(x, flat(o_na), flat(o_cv), flat(o_df), gates, p["w_branch"], p["w_out"],
                  p["ln_mix_post"], p["ln_ffn_pre"], p["ln_ffn_post"], p["w_ffn_in"], p["w_ffn_out"], l)
    return x.reshape(bsz, seq, D_MODEL)


def _prepare(w_in, b_gate, na_rpb, conv_dw_w, conv_dw_b, conv_ln_g, conv_ln_b,
             diff_lambda, diff_subln_g, t5_bias, w_branch, w_out,
             ln_mix_pre, ln_mix_post, ln_ffn_pre, ln_ffn_post, w_ffn_in, w_ffn_out):
    depth = w_in.shape[0]
    row = lambda a: a.reshape(depth, 1, a.shape[-1])
    lf = diff_lambda.astype(F32)
    lam_init = jnp.asarray([0.8 - 0.6 * math.exp(-0.3 * l) for l in range(depth)], F32)
    lam_full = (jnp.exp(jnp.sum(lf[:, 0] * lf[:, 1], axis=-1))
                - jnp.exp(jnp.sum(lf[:, 2] * lf[:, 3], axis=-1)) + lam_init)
    return dict(
        w_in=w_in.astype(BF16), b_gate=row(b_gate), na_rpb=na_rpb, na_tiles=[{} for _ in range(depth)],
        conv_dw_w=conv_dw_w, conv_dw_b=row(conv_dw_b), conv_ln_g=row(conv_ln_g), conv_ln_b=row(conv_ln_b),
        lam_full=lam_full.reshape(depth, 1), diff_subln_g=row(diff_subln_g),
        diff_bias=_diff_bias(t5_bias),
        w_branch=w_branch.astype(BF16), w_out=w_out.astype(BF16),
        ln_mix_pre=row(ln_mix_pre), ln_mix_post=row(ln_mix_post),
        ln_ffn_pre=row(ln_ffn_pre), ln_ffn_post=row(ln_ffn_post),
        w_ffn_in=w_ffn_in.astype(BF16), w_ffn_out=w_ffn_out.astype(BF16),
    )


def kernel(x_prompt, x_sample, w_in, b_gate, na_rpb, conv_dw_w, conv_dw_b, conv_ln_g, conv_ln_b,
           diff_lambda, diff_subln_g, t5_bias, w_branch, w_out,
           ln_mix_pre, ln_mix_post, ln_ffn_pre, ln_ffn_post, w_ffn_in, w_ffn_out):
    p = _prepare(w_in, b_gate, na_rpb, conv_dw_w, conv_dw_b, conv_ln_g, conv_ln_b,
                 diff_lambda, diff_subln_g, t5_bias, w_branch, w_out,
                 ln_mix_pre, ln_mix_post, ln_ffn_pre, ln_ffn_post, w_ffn_in, w_ffn_out)
    return (_trunk(x_prompt, p), _trunk(x_sample, p))
```

```python
import functools
import math

import numpy as np
import jax
import jax.numpy as jnp
from jax import lax
from jax.experimental import pallas as pl
from jax.experimental.pallas import tpu as pltpu

F32 = jnp.float32
BF16 = jnp.bfloat16

D_MODEL = 1024
GRID_W = 64
NA_HEADS = 8
NA_HEAD_DIM = 64
NA_WIDTH = NA_HEADS * NA_HEAD_DIM
NA_WIN_ROWS_MAX = 8
NA_WIN_COLS = 16
CONV_CH = 512
CONV_WIDTH = 31
DIFF_HEADS = 4
DIFF_HEAD_DIM = 64
DIFF_WIDTH = DIFF_HEADS * 2 * DIFF_HEAD_DIM
N_BRANCH = 3
BRANCH_W = 512
T5_BUCKETS = 32
T5_MAX_DIST = 128
FFN_HIDDEN = 2816
EPS = 1e-6

OFF_CONV = 3 * NA_WIDTH
OFF_DIFF = OFF_CONV + 2 * CONV_CH
OFF_GATE = OFF_DIFF + 3 * DIFF_WIDTH
IN_COLS = OFF_GATE + N_BRANCH * D_MODEL

LANES = 128
SUBLANES = 8
VMEM_LIMIT = 56 * 1024 * 1024

IN_TM = 512
POST_TM = 512
NA_R = 4
NA_KR = 12
CONV_HALO = 16
CONV_RC = 128
DIFF_T = 512
DIFF_STRIP = 256
DIFF_ONES = 16
DIFF_UNROLL = 2
DIFF_MAX_UNROLL = 4
DIFF_PAIRS_PER_STEP = 64
NEG = -1e30
LOG2E = math.log2(math.e)

_EXACT = lax.Precision.HIGHEST


def _rms(x, g):
    return x * lax.rsqrt(jnp.mean(x * x, axis=-1, keepdims=True) + EPS) * g


def _const_spec(shape, index_map):
    return pl.BlockSpec(shape, index_map, pipeline_mode=pl.Buffered(1))


def _in_proj_kernel(x_ref, g_ref, w_ref, bg_ref,
                    naq_ref, nak_ref, nav_ref, cv_ref, dqt_ref, dk_ref, dvt_ref, gate_ref):
    h = _rms(x_ref[...], g_ref[...]).astype(BF16)

    def proj(c0, c1):
        return jnp.dot(h, w_ref[:, c0:c1], preferred_element_type=F32)

    for b in range(N_BRANCH):
        c0 = OFF_GATE + b * D_MODEL
        gate = jax.nn.sigmoid(proj(c0, c0 + D_MODEL) + bg_ref[:, b * D_MODEL:(b + 1) * D_MODEL])
        gate_ref[:, b * D_MODEL:(b + 1) * D_MODEL] = gate.astype(BF16)
    dqt = (proj(OFF_DIFF, OFF_DIFF + DIFF_WIDTH) * (DIFF_HEAD_DIM ** -0.5 * LOG2E)).T
    comp0 = lax.broadcasted_iota(jnp.int32, dqt.shape, 0) % (2 * DIFF_HEAD_DIM) < DIFF_HEAD_DIM
    dqt_ref[:, :IN_TM] = jnp.where(comp0, dqt, 0.0).astype(BF16)
    dqt_ref[:, IN_TM:] = jnp.where(comp0, 0.0, dqt).astype(BF16)
    dvt_ref[...] = proj(OFF_DIFF + 2 * DIFF_WIDTH, OFF_GATE).T.astype(BF16)
    naq_ref[...] = (proj(0, NA_WIDTH) * (NA_HEAD_DIM ** -0.5 * LOG2E)).astype(BF16)
    u = proj(OFF_CONV, OFF_DIFF)
    cv_ref[...] = (u[:, :CONV_CH] * jax.nn.sigmoid(u[:, CONV_CH:])).astype(BF16)
    dk_ref[...] = proj(OFF_DIFF + DIFF_WIDTH, OFF_DIFF + 2 * DIFF_WIDTH).astype(BF16)
    nak_ref[...] = proj(NA_WIDTH, 2 * NA_WIDTH).astype(BF16)
    nav_ref[...] = proj(2 * NA_WIDTH, 3 * NA_WIDTH).astype(BF16)


def _in_proj(x, g_pre, w_in, b_gate, layer):
    n_tok = x.shape[0]
    tm = IN_TM
    assert n_tok % tm == 0
    n_t = n_tok // tm
    tok = lambda w: (pl.BlockSpec((tm, w), lambda i: (i, 0)), jax.ShapeDtypeStruct((n_tok, w), BF16))
    tok_t = lambda w, c: (pl.BlockSpec((None, w, c), lambda i: (i, 0, 0)),
                          jax.ShapeDtypeStruct((n_t, w, c), BF16))
    outs = [tok(NA_WIDTH), tok(NA_WIDTH), tok(NA_WIDTH), tok(CONV_CH),
            tok_t(DIFF_WIDTH, 2 * tm), tok(DIFF_WIDTH), tok_t(DIFF_WIDTH, tm), tok(N_BRANCH * D_MODEL)]
    return pl.pallas_call(
        _in_proj_kernel,
        grid=(n_t,),
        in_specs=[
            pl.BlockSpec((tm, D_MODEL), lambda i: (i, 0)),
            _const_spec((None, 1, D_MODEL), lambda i: (layer, 0, 0)),
            _const_spec((None, D_MODEL, IN_COLS), lambda i: (layer, 0, 0)),
            _const_spec((None, 1, N_BRANCH * D_MODEL), lambda i: (layer, 0, 0)),
        ],
        out_specs=[o[0] for o in outs],
        out_shape=[o[1] for o in outs],
        compiler_params=pltpu.CompilerParams(
            dimension_semantics=("arbitrary",), vmem_limit_bytes=VMEM_LIMIT),
        name="in_proj",
    )(x, g_pre, w_in, b_gate)


def _na_plan(rows):
    assert rows % NA_R == 0 and rows >= NA_KR
    kr = min(NA_WIN_ROWS_MAX, rows)
    n_groups = rows // NA_R
    variants, var_idx, win_start = [], [], []
    for g in range(n_groups):
        r0 = g * NA_R
        ws = int(np.clip(r0 - kr // 2, 0, rows - NA_KR))
        ro = -np.ones((NA_R, NA_KR), np.int32)
        for i in range(NA_R):
            r = r0 + i
            rs = int(np.clip(r - kr // 2, 0, rows - kr))
            assert ws <= rs and rs + kr <= ws + NA_KR
            for a in range(NA_KR):
                if rs <= ws + a < rs + kr:
                    ro[i, a] = ws + a - r + (NA_WIN_ROWS_MAX - 1)
        for v, known in enumerate(variants):
            if np.array_equal(known, ro):
                var_idx.append(v)
                break
        else:
            var_idx.append(len(variants))
            variants.append(ro)
        win_start.append(ws)
    return np.stack(variants), np.asarray(var_idx, np.int32), np.asarray(win_start, np.int32)


def _na_bias_tiles(rpb, row_off):
    n_var = row_off.shape[0]
    n_rows = 2 * NA_WIN_ROWS_MAX - 1
    n_cols = 2 * NA_WIN_COLS - 1
    col = np.arange(GRID_W)
    col_start = np.clip(col - NA_WIN_COLS // 2, 0, GRID_W - NA_WIN_COLS)
    kc = col[None, :]
    col_ok = (kc >= col_start[:, None]) & (kc < col_start[:, None] + NA_WIN_COLS)
    col_off = kc - col[:, None] + (NA_WIN_COLS - 1)
    col_hot = (col_ok[None] & (col_off[None] == np.arange(n_cols)[:, None, None])).astype(np.float32)
    row_hot = (row_off.reshape(-1)[:, None] == np.arange(n_rows)[None, :]).astype(np.float32)
    t = jnp.einsum("xr,hrc->hxc", row_hot, rpb, precision=_EXACT)
    t = jnp.einsum("hxc,cqk->hxqk", t, col_hot, precision=_EXACT)
    ok = (row_off >= 0).reshape(-1)[:, None, None] & col_ok[None]
    t = jnp.where(ok[None], t * LOG2E, NEG).reshape(NA_HEADS, n_var, NA_R, NA_KR, GRID_W, GRID_W)
    return t.transpose(1, 0, 2, 4, 3, 5).reshape(n_var, NA_HEADS, NA_R * GRID_W, NA_KR * GRID_W)


def _conv_rows(xs_ref, r0, w_ref, b_ref, lg_ref, lb_ref):
    first = CONV_HALO - CONV_WIDTH // 2
    ext = CONV_RC + SUBLANES
    groups = []
    for c0 in range(0, CONV_CH, LANES):
        slab = xs_ref[r0:r0 + CONV_RC + 2 * CONV_HALO, c0:c0 + LANES]
        acc = None
        for s in range(SUBLANES):
            part = None
            for j in range(s, CONV_WIDTH, SUBLANES):
                term = w_ref[j:j + 1, c0:c0 + LANES] * slab[j - s:j - s + ext]
                part = term if part is None else part + term
            shift = first + s
            part = part[shift:shift + CONV_RC]
            acc = part if acc is None else acc + part
        groups.append(acc)
    acc = jnp.concatenate(groups, axis=1) + b_ref[...]
    mu = jnp.mean(acc, axis=-1, keepdims=True)
    xc = acc - mu
    y = xc * lax.rsqrt(jnp.mean(xc * xc, axis=-1, keepdims=True) + EPS)
    y = y * lg_ref[...] + lb_ref[...]
    return (y * jax.nn.sigmoid(y)).astype(BF16)


def _na_kernel(var_ref, ws_ref, q_ref, k_ref, v_ref, bias_ref, u_ref, ul_ref, ur_ref, cw_ref, cb_ref, clg_ref, clb_ref,
               o_ref, ocv_ref, xs_ref):
    del var_ref
    n_q = NA_R * GRID_W
    g = pl.program_id(1)
    xs_ref[0:CONV_HALO, :] = jnp.where(g > 0, ul_ref[...].astype(F32), 0.0)
    xs_ref[CONV_HALO:CONV_HALO + n_q, :] = u_ref[...].astype(F32)
    xs_ref[CONV_HALO + n_q:, :] = jnp.where(g < pl.num_programs(1) - 1, ur_ref[...].astype(F32), 0.0)
    for rc in range(n_q // CONV_RC):
        ocv_ref[rc * CONV_RC:(rc + 1) * CONV_RC, :] = _conv_rows(xs_ref, rc * CONV_RC, cw_ref, cb_ref, clg_ref, clb_ref)
    n_k = NA_KR * GRID_W
    k0 = pl.multiple_of(ws_ref[pl.program_id(1)] * GRID_W, GRID_W)
    low_half = lax.broadcasted_iota(jnp.int32, (n_q, LANES), 1) < NA_HEAD_DIM
    for hp in range(NA_HEADS // 2):
        cols = slice(hp * LANES, (hp + 1) * LANES)
        q2 = q_ref[:, cols]
        k2 = k_ref[pl.ds(k0, n_k), cols]
        v2 = jnp.concatenate([v_ref[pl.ds(k0, n_k), cols], jnp.ones((n_k, LANES), BF16)], axis=1)
        halves = []
        for half in range(2):
            keep = low_half if half == 0 else jnp.logical_not(low_half)
            qm = jnp.where(keep, q2, jnp.zeros_like(q2))
            s = lax.dot_general(qm, k2, (((1,), (1,)), ((), ())), preferred_element_type=F32)
            s = s + bias_ref[2 * hp + half]
            p = jnp.exp2(s - jnp.max(s, axis=-1, keepdims=True))
            o = jnp.dot(p.astype(BF16), v2, preferred_element_type=F32)
            halves.append(o[:, :LANES] / o[:, LANES:])
        o_ref[:, cols] = jnp.where(low_half, halves[0], halves[1]).astype(BF16)


def _na(q, k, v, rpb, tile_cache, u, dw_w, dw_b, ln_g, ln_b, layer):
    bsz, seq, _ = q.shape
    assert (NA_R * GRID_W) % CONV_RC == 0 and (NA_R * GRID_W) % CONV_HALO == 0 and seq % CONV_HALO == 0
    assert CONV_HALO >= CONV_WIDTH // 2 and 2 * CONV_HALO >= SUBLANES + (CONV_WIDTH - 1) // SUBLANES * SUBLANES
    per_tile = NA_R * GRID_W // CONV_HALO
    n_halo = seq // CONV_HALO
    cvec = lambda: _const_spec((None, 1, CONV_CH), lambda b, g, var, ws: (layer, 0, 0))
    rows = seq // GRID_W
    row_off, var_idx, win_start = _na_plan(rows)
    key = row_off.tobytes()
    if key not in tile_cache:
        tile_cache[key] = _na_bias_tiles(rpb, row_off)
    bias = tile_cache[key]
    n_q = NA_R * GRID_W
    n_k = NA_KR * GRID_W
    grid_spec = pltpu.PrefetchScalarGridSpec(
        num_scalar_prefetch=2,
        grid=(bsz, rows // NA_R),
        in_specs=[
            pl.BlockSpec((None, n_q, NA_WIDTH), lambda b, g, var, ws: (b, g, 0)),
            pl.BlockSpec((None, seq, NA_WIDTH), lambda b, g, var, ws: (b, 0, 0)),
            pl.BlockSpec((None, seq, NA_WIDTH), lambda b, g, var, ws: (b, 0, 0)),
            pl.BlockSpec((None, NA_HEADS, n_q, n_k), lambda b, g, var, ws: (var[g], 0, 0, 0)),
            pl.BlockSpec((None, n_q, CONV_CH), lambda b, g, var, ws: (b, g, 0)),
            pl.BlockSpec((None, CONV_HALO, CONV_CH), lambda b, g, var, ws: (b, jnp.maximum(g * per_tile - 1, 0), 0)),
            pl.BlockSpec((None, CONV_HALO, CONV_CH), lambda b, g, var, ws: (b, jnp.minimum((g + 1) * per_tile, n_halo - 1), 0)),
            _const_spec((None, CONV_WIDTH, CONV_CH), lambda b, g, var, ws: (layer, 0, 0)),
            cvec(), cvec(), cvec(),
        ],
        out_specs=[pl.BlockSpec((None, n_q, NA_WIDTH), lambda b, g, var, ws: (b, g, 0)),
                   pl.BlockSpec((None, n_q, CONV_CH), lambda b, g, var, ws: (b, g, 0))],
        scratch_shapes=[pltpu.VMEM((n_q + 2 * CONV_HALO, CONV_CH), F32)],
    )
    return pl.pallas_call(
        _na_kernel,
        grid_spec=grid_spec,
        out_shape=[jax.ShapeDtypeStruct((bsz, seq, NA_WIDTH), BF16), jax.ShapeDtypeStruct((bsz, seq, CONV_CH), BF16)],
        compiler_params=pltpu.CompilerParams(
            dimension_semantics=("arbitrary", "arbitrary"), vmem_limit_bytes=VMEM_LIMIT),
        name="na",
    )(jnp.asarray(var_idx), jnp.asarray(win_start), q, k, v, bias, u, u, u, dw_w, dw_b, ln_g, ln_b)


def _t5_bucket(rel):
    nb = T5_BUCKETS // 2
    max_exact = nb // 2
    ret = jnp.where(rel > 0, nb, 0)
    n = jnp.abs(rel)
    nf = jnp.maximum(n, 1).astype(jnp.float32)
    large = max_exact + (jnp.log(nf / max_exact) / math.log(T5_MAX_DIST / max_exact)
                         * (nb - max_exact)).astype(jnp.int32)
    large = jnp.minimum(large, nb - 1)
    return ret + jnp.where(n < max_exact, n, large)


def _toeplitz(g, n):
    lead = g.shape[:-1]
    flat = jnp.tile(g, (1,) * len(lead) + (n,))
    return flat[..., :n * (2 * n - 1)].reshape(lead + (n, 2 * n - 1))[..., :n]


def _diff_bias(t5_bias):
    t = DIFF_T
    assert t >= T5_MAX_DIST
    m = np.arange(2 * t)
    q_minus_k = np.where(m < t, m, m - 2 * t)
    rel = np.stack([d * t - q_minus_k for d in (-1, 0, 1)]).astype(np.int32)
    g = t5_bias[_t5_bucket(jnp.asarray(rel))]
    near = _toeplitz(g.transpose(2, 0, 1), t)
    far = t5_bias[_t5_bucket(jnp.asarray([-T5_MAX_DIST, T5_MAX_DIST], jnp.int32))]
    far = jnp.broadcast_to(far.T[:, :, None, None], (DIFF_HEADS, 2, t, t))
    return jnp.concatenate([far[:, :1], near, far[:, 1:]], axis=1) * LOG2E


def _diff_kernel(lam_ref, qt_ref, k_ref, vt_ref, bias_ref, g_ref, o_ref,
                 s_ref, smax_ref, p_ref, alpha_ref, m_ref, acc_ref, vt1_ref, *, lam_init, n_k, n_seq):
    t = DIFF_T
    n_blocks = n_seq * n_k
    n_pairs = n_blocks * n_k

    strips =[slice(c0, c0 + DIFF_STRIP) for c0 in range(0, 2 * t, DIFF_STRIP)]

    def scores(e, par, cols):
        qb, ki = e // n_k, e % n_k
        sq, qi = qb // n_k, qb % n_k
        kk = k_ref[sq, pl.ds(pl.multiple_of(ki * t, t), t), :]
        s = jnp.dot(kk, qt_ref[qb, :, cols], preferred_element_type=F32)
        q0 = cols.start % t
        s = s + bias_ref[jnp.clip(ki - qi, -2, 2) + 2, :, q0:q0 + DIFF_STRIP]
        s_ref[par, :, cols] = s
        smax_ref[par, :, cols] = jnp.max(s, axis=0, keepdims=True)

    def softmax(e, par, cols):
        m_old = jnp.where(e % n_k == 0, NEG, m_ref[:, cols])
        m_new = jnp.maximum(m_old, smax_ref[par, :, cols])
        m_ref[:, cols] = m_new
        alpha_ref[par, :, cols] = jnp.exp2(m_old - m_new)
        p_ref[par, :, cols] = jnp.exp2(s_ref[par, :, cols] - m_new).astype(BF16)

    def accumulate(e, par, cols):
        qb = e // n_k
        kb = qb // n_k * n_k + e % n_k
        pv = jnp.dot(vt1_ref[kb], p_ref[par, :, cols], preferred_element_type=F32)
        acc_ref[qb, :, cols] = alpha_ref[par, :, cols] * acc_ref[qb, :, cols] + pv

    def finalize(qb, carry):
        inv_l = 1.0 / acc_ref[qb, LANES:LANES + 1, :]
        ot = (acc_ref[qb, :LANES, :t] * inv_l[:, :t]
              - lam_ref[0] * (acc_ref[qb, :LANES, t:] * inv_l[:, t:]))
        o = _rms(ot.T, g_ref[...]) * (1.0 - lam_init)
        o_ref[qb // n_k, pl.ds(pl.multiple_of(qb % n_k * t, t), t), :] = o.astype(BF16)
        return carry

    vt1_ref[:, :LANES, :] = vt_ref[...]
    vt1_ref[:, LANES:, :] = jnp.ones((n_blocks, DIFF_ONES, t), BF16)
    acc_ref[...] = jnp.zeros(acc_ref.shape, F32)
    alpha_ref[1] = jnp.zeros(alpha_ref.shape[1:], F32)
    p_ref[1] = jnp.zeros(p_ref.shape[1:], BF16)
    for cols in strips:
        scores(0, 0, cols)

    def step(e, par):
        e_next = jnp.minimum(e + 1, n_pairs - 1)
        for cols in strips:
            accumulate(jnp.maximum(e - 1, 0), 1 - par, cols)
            softmax(e, par, cols)
            scores(e_next, 1 - par, cols)

    unroll = n_k if n_k <= DIFF_MAX_UNROLL else DIFF_UNROLL
    inline_finalize = unroll == n_k

    def steps(j, carry):
        for u in range(unroll):
            step(unroll * j + u, u % 2)
            if inline_finalize and u == 0:
                finalize(jnp.maximum(j - 1, 0), carry)
        return carry

    lax.fori_loop(0, n_pairs // unroll, steps, 0)
    for cols in strips:
        accumulate(n_pairs - 1, 1, cols)
    if inline_finalize:
        finalize(n_blocks - 1, 0)
    else:
        lax.fori_loop(0, n_blocks, finalize, 0)


def _diff(qt2, k, vt, bias, lam_full, subln_g, lam_init, layer):
    bsz, seq, _ = k.shape
    t = DIFF_T
    assert seq % t == 0 and IN_TM == t
    n_k = seq // t
    assert DIFF_UNROLL % 2 == 0 and n_k % 2 == 0
    n_seq = max(d for d in range(1, bsz + 1) if bsz % d == 0 and (d == 1 or d * n_k * n_k <= DIFF_PAIRS_PER_STEP))
    n_blocks = n_seq * n_k
    smem = pl.BlockSpec(memory_space=pltpu.SMEM)
    return pl.pallas_call(
        functools.partial(_diff_kernel, lam_init=lam_init, n_k=n_k, n_seq=n_seq),
        grid=(DIFF_HEADS, bsz // n_seq),
        in_specs=[
            smem,
            pl.BlockSpec((n_blocks, LANES, 2 * t), lambda h, b: (b, h, 0)),
            pl.BlockSpec((n_seq, seq, LANES), lambda h, b: (b, 0, h)),
            pl.BlockSpec((n_blocks, LANES, t), lambda h, b: (b, h, 0)),
            pl.BlockSpec((None, 5, t, t), lambda h, b: (h, 0, 0, 0)),
            _const_spec((None, 1, LANES), lambda h, b: (layer, 0, 0)),
        ],
        out_specs=pl.BlockSpec((n_seq, seq, LANES), lambda h, b: (b, 0, h)),
        out_shape=jax.ShapeDtypeStruct((bsz, seq, DIFF_WIDTH), BF16),
        scratch_shapes=[
            pltpu.VMEM((2, t, 2 * t), F32),
            pltpu.VMEM((2, 1, 2 * t), F32),
            pltpu.VMEM((2, t, 2 * t), BF16),
            pltpu.VMEM((2, 1, 2 * t), F32),
            pltpu.VMEM((1, 2 * t), F32),
            pltpu.VMEM((n_blocks, LANES + DIFF_ONES, 2 * t), F32),
            pltpu.VMEM((n_blocks, LANES + DIFF_ONES, t), BF16),
        ],
        compiler_params=pltpu.CompilerParams(
            dimension_semantics=("arbitrary", "arbitrary"), vmem_limit_bytes=VMEM_LIMIT),
        name="diff",
    )(lam_full, qt2, k, vt, bias, subln_g)


_FFN_CHUNKS = ((0, 768), (768, 1536), (1536, 2304), (2304, FFN_HIDDEN))


def _post_kernel(x_ref, ona_ref, ocv_ref, odf_ref, gate_ref, wb_ref, wo_ref,
                 g_mix_ref, g_pre_ref, g_post_ref, wfi_ref, wfo_ref, o_ref, act_ref):
    half = POST_TM // 2
    for r0 in (0, half):
        rows = slice(r0, r0 + half)
        merged = None
        for b, br_ref in enumerate((ona_ref, ocv_ref, odf_ref)):
            proj = jnp.dot(br_ref[rows, :], wb_ref[b], preferred_element_type=F32)
            term = gate_ref[rows, b * D_MODEL:(b + 1) * D_MODEL].astype(F32) * proj
            merged = term if merged is None else merged + term
        y = jnp.dot(merged.astype(BF16), wo_ref[...], preferred_element_type=F32)
        x1 = x_ref[rows, :] + _rms(y, g_mix_ref[...])
        hf = _rms(x1, g_pre_ref[...]).astype(BF16)
        for c0, c1 in _FFN_CHUNKS:
            gate = jnp.dot(hf, wfi_ref[:, c0:c1], preferred_element_type=F32)
            up = jnp.dot(hf, wfi_ref[:, FFN_HIDDEN + c0:FFN_HIDDEN + c1], preferred_element_type=F32)
            act_ref[rows, c0:c1] = (gate * jax.nn.sigmoid(gate) * up).astype(BF16)
        z = jnp.dot(act_ref[rows, :], wfo_ref[...], preferred_element_type=F32)
        o_ref[rows, :] = x1 + _rms(z, g_post_ref[...])


def _post(x, o_na, o_cv, o_df, gates, w_branch, w_out, g_mix, g_pre, g_post, w_ffn_in, w_ffn_out,
          layer):
    n_tok = x.shape[0]
    tm = POST_TM
    assert n_tok % tm == 0
    vec = lambda: _const_spec((None, 1, D_MODEL), lambda i: (layer, 0, 0))
    br = lambda: pl.BlockSpec((tm, BRANCH_W), lambda i: (i, 0))
    return pl.pallas_call(
        _post_kernel,
        grid=(n_tok // tm,),
        in_specs=[
            pl.BlockSpec((tm, D_MODEL), lambda i: (i, 0)),
            br(), br(), br(),
            pl.BlockSpec((tm, N_BRANCH * D_MODEL), lambda i: (i, 0)),
            _const_spec((None, N_BRANCH, BRANCH_W, D_MODEL), lambda i: (layer, 0, 0, 0)),
            _const_spec((None, D_MODEL, D_MODEL), lambda i: (layer, 0, 0)),
            vec(), vec(), vec(),
            _const_spec((None, D_MODEL, 2 * FFN_HIDDEN), lambda i: (layer, 0, 0)),
            _const_spec((None, FFN_HIDDEN, D_MODEL), lambda i: (layer, 0, 0)),
        ],
        out_specs=pl.BlockSpec((tm, D_MODEL), lambda i: (i, 0)),
        out_shape=jax.ShapeDtypeStruct((n_tok, D_MODEL), F32),
        scratch_shapes=[pltpu.VMEM((tm, FFN_HIDDEN), BF16)],
        compiler_params=pltpu.CompilerParams(
            dimension_semantics=("arbitrary",), vmem_limit_bytes=VMEM_LIMIT),
        name="post",
    )(x, o_na, o_cv, o_df, gates, w_branch, w_out, g_mix, g_pre, g_post, w_ffn_in, w_ffn_out)


def _trunk(x, p):
    bsz, seq, _ = x.shape
    n_tok = bsz * seq
    depth = p["w_in"].shape[0]
    x = x.reshape(n_tok, D_MODEL)
    for l in range(depth):
        lam_init = 0.8 - 0.6 * math.exp(-0.3 * l)
        naq, nak, nav, glu, dqt, dk, dvt, gates = _in_proj(x, p["ln_mix_pre"], p["w_in"], p["b_gate"], l)
        seq3 = lambda a: a.reshape(bsz, seq, a.shape[-1])
        o_na, o_cv = _na(seq3(naq), seq3(nak), seq3(nav), p["na_rpb"][l], p["na_tiles"][l],
                         seq3(glu), p["conv_dw_w"], p["conv_dw_b"], p["conv_ln_g"], p["conv_ln_b"], l)
        o_df = _diff(dqt, seq3(dk), dvt, p["diff_bias"], p["lam_full"][l], p["diff_subln_g"],
                     lam_init, l)
        flat = lambda a: a.reshape(n_tok, a.shape[-1])
        x = _post(x, flat(o_na), flat(o_cv), flat(o_df), gates, p["w_branch"], p["w_out"],
                  p["ln_mix_post"], p["ln_ffn_pre"], p["ln_ffn_post"], p["w_ffn_in"], p["w_ffn_out"], l)
    return x.reshape(bsz, seq, D_MODEL)


def _prepare(w_in, b_gate, na_rpb, conv_dw_w, conv_dw_b, conv_ln_g, conv_ln_b,
             diff_lambda, diff_subln_g, t5_bias, w_branch, w_out,
             ln_mix_pre, ln_mix_post, ln_ffn_pre, ln_ffn_post, w_ffn_in, w_ffn_out):
    depth = w_in.shape[0]
    row = lambda a: a.reshape(depth, 1, a.shape[-1])
    lf = diff_lambda.astype(F32)
    lam_init = jnp.asarray([0.8 - 0.6 * math.exp(-0.3 * l) for l in range(depth)], F32)
    lam_full = (jnp.exp(jnp.sum(lf[:, 0] * lf[:, 1], axis=-1))
                - jnp.exp(jnp.sum(lf[:, 2] * lf[:, 3], axis=-1)) + lam_init)
    return dict(
        w_in=w_in.astype(BF16), b_gate=row(b_gate), na_rpb=na_rpb, na_tiles=[{} for _ in range(depth)],
        conv_dw_w=conv_dw_w, conv_dw_b=row(conv_dw_b), conv_ln_g=row(conv_ln_g), conv_ln_b=row(conv_ln_b),
        lam_full=lam_full.reshape(depth, 1), diff_subln_g=row(diff_subln_g),
        diff_bias=_diff_bias(t5_bias),
        w_branch=w_branch.astype(BF16), w_out=w_out.astype(BF16),
        ln_mix_pre=row(ln_mix_pre), ln_mix_post=row(ln_mix_post),
        ln_ffn_pre=row(ln_ffn_pre), ln_ffn_post=row(ln_ffn_post),
        w_ffn_in=w_ffn_in.astype(BF16), w_ffn_out=w_ffn_out.astype(BF16),
    )


def kernel(x_prompt, x_sample, w_in, b_gate, na_rpb, conv_dw_w, conv_dw_b, conv_ln_g, conv_ln_b,
           diff_lambda, diff_subln_g, t5_bias, w_branch, w_out,
           ln_mix_pre, ln_mix_post, ln_ffn_pre, ln_ffn_post, w_ffn_in, w_ffn_out):
    p = _prepare(w_in, b_gate, na_rpb, conv_dw_w, conv_dw_b, conv_ln_g, conv_ln_b,
                 diff_lambda, diff_subln_g, t5_bias, w_branch, w_out,
                 ln_mix_pre, ln_mix_post, ln_ffn_pre, ln_ffn_post, w_ffn_in, w_ffn_out)
    return (_trunk(x_prompt, p), _trunk(x_sample, p))
```
